```python
import math
import jax, jax.numpy as jnp
from jax import lax
import numpy as np

D_MODEL = 2048
BATCH = 4
SEQ = 4096
DEPTH = 4

GRID_W = 64
CTX_LEN = 256
EPS = 1e-6
ROPE_THETA = 10000.0
N_MOD = 9
D_FF = 5632

SSD_HEADS = 16
SSD_HEADDIM = 64
SSD_INNER = SSD_HEADS * SSD_HEADDIM
SSD_GROUPS = 2
SSD_STATE = 128
SSD_CONV = 5
SSD_CHUNK = 128
SSD_CONV_CH = SSD_INNER + 2 * SSD_GROUPS * SSD_STATE
DT_MIN = 0.001
DT_MAX = 0.1

MLA_HEADS = 8
MLA_Q_RANK = 384
MLA_KV_RANK = 256
MLA_NOPE = 64
MLA_ROPE = 32
MLA_V = 64
MLA_BLOCK = 128

SWA_HEADS = 8
SWA_KV_HEADS = 2
SWA_HEADDIM = 64
SWA_WINDOW = 128
SWA_BLOCK = 128

D_MIX = SSD_INNER + MLA_HEADS * MLA_V + SWA_HEADS * SWA_HEADDIM
IN_SIZES = (SSD_INNER, SSD_CONV_CH, 2 * SSD_HEADS, MLA_Q_RANK, MLA_KV_RANK, MLA_ROPE,
            SWA_HEADS * SWA_HEADDIM, SWA_KV_HEADS * SWA_HEADDIM, SWA_KV_HEADS * SWA_HEADDIM)
D_IN = (SSD_INNER + SSD_CONV_CH + 2 * SSD_HEADS + MLA_Q_RANK + MLA_KV_RANK + MLA_ROPE
        + SWA_HEADS * SWA_HEADDIM + 2 * SWA_KV_HEADS * SWA_HEADDIM)

kernel_name = "hybrid_parallel_group_diffusion_trunk"


def rms_norm(x, g):
    xf = x.astype(jnp.float32)
    y = xf * lax.rsqrt(jnp.mean(xf * xf, axis=-1, keepdims=True) + EPS)
    return (y * g.astype(jnp.float32)).astype(x.dtype)


def modulate(u, shift, scale):
    return u * (1 + scale) + shift


def swiglu(u, wg, wu, wd):
    return (jax.nn.silu(u @ wg) * (u @ wu)) @ wd


def split_cols(p):
    offs = np.cumsum(IN_SIZES)[:-1].tolist()
    return jnp.split(p, offs, axis=-1)


def axial_rope_tables(pos_row, pos_col, dim):
    quarter = dim // 4
    inv_freq = ROPE_THETA ** (-jnp.arange(quarter, dtype=jnp.float32) / quarter)
    ang_r = pos_row.astype(jnp.float32)[:, None] * inv_freq[None, :]
    ang_c = pos_col.astype(jnp.float32)[:, None] * inv_freq[None, :]
    ang = jnp.concatenate([ang_r, ang_r, ang_c, ang_c], axis=-1)
    return jnp.cos(ang), jnp.sin(ang)


def _rotate_half(v):
    a, b = jnp.split(v, 2, axis=-1)
    return jnp.concatenate([-b, a], axis=-1)


def apply_axial_rope(x, cos, sin):
    xr, xc = jnp.split(x, 2, axis=-1)
    rot = jnp.concatenate([_rotate_half(xr), _rotate_half(xc)], axis=-1)
    return x * cos[None, :, None, :].astype(x.dtype) + rot * sin[None, :, None, :].astype(x.dtype)


def dwconv_centred(x, w, b):
    y = lax.conv_general_dilated(
        x, w[:, None, :].astype(x.dtype), window_strides=(1,),
        padding=[(SSD_CONV // 2, SSD_CONV // 2)],
        dimension_numbers=("NWC", "WIO", "NWC"), feature_group_count=x.shape[-1])
    return y + b.astype(x.dtype)


def ssd_chunk_terms(xs, dt, A, Bm):
    b, l, h, pdim = xs.shape
    g, n = Bm.shape[2], Bm.shape[3]
    r, nc = h // g, l // SSD_CHUNK
    xdt = (xs * dt[..., None]).reshape(b, nc, SSD_CHUNK, g, r, pdim)
    a_cum = jnp.cumsum((dt * A).reshape(b, nc, SSD_CHUNK, g, r), axis=2)
    Bc = Bm.reshape(b, nc, SSD_CHUNK, g, n)
    return xdt, a_cum, Bc


def ssd_chunk_states(xdt, a_cum, Bc, h0):
    decay_to_end = jnp.exp(a_cum[:, :, -1:] - a_cum).astype(xdt.dtype)
    s_local = jnp.einsum('bcsgn,bcsgr,bcsgrp->bcgrpn', Bc, decay_to_end, xdt).astype(jnp.float32)
    chunk_decay = jnp.exp(a_cum[:, :, -1])

    def step(h, inp):
        s, dec = inp
        return dec[..., None, None] * h + s, h

    h_final, h_in = lax.scan(step, h0, (jnp.moveaxis(s_local, 1, 0), jnp.moveaxis(chunk_decay, 1, 0)))
    return jnp.moveaxis(h_in, 0, 1), h_final


def ssd_chunk_outputs(xdt, a_cum, Bc, Cm, h_in):
    b, nc, q, g, r, pdim = xdt.shape
    Cc = Cm.reshape(b, nc, q, g, -1)
    seg = a_cum[:, :, :, None] - a_cum[:, :, None, :]
    lower_tri = jnp.tril(jnp.ones((q, q), bool))[None, None, :, :, None, None]
    decay = jnp.exp(jnp.where(lower_tri, seg, -jnp.inf)).astype(xdt.dtype)
    cb = jnp.einsum('bclgn,bcsgn->bclsg', Cc, Bc)
    y_diag = jnp.einsum('bclsg,bclsgr,bcsgrp->bclgrp', cb, decay, xdt)
    y_off = jnp.einsum('bclgn,bclgr,bcgrpn->bclgrp', Cc, jnp.exp(a_cum).astype(xdt.dtype),
                       h_in.astype(xdt.dtype))
    return (y_diag + y_off).reshape(b, nc * q, g * r, pdim)


def ssd_scan(xs, dt, A, Bm, Cm, h0):
    xdt, a_cum, Bc = ssd_chunk_terms(xs, dt, A, Bm)
    h_in, _ = ssd_chunk_states(xdt, a_cum, Bc, h0)
    return ssd_chunk_outputs(xdt, a_cum, Bc, Cm, h_in)


def _orient(t, d):
    return t if d == 0 else jnp.flip(t, axis=1)


def _ssd_inputs(xbc, dt_raw, p):
    b, l = xbc.shape[:2]
    xbc = jax.nn.silu(dwconv_centred(xbc, p['conv_w'], p['conv_b']))
    xs, Bm, Cm = jnp.split(xbc, [SSD_INNER, SSD_INNER + SSD_GROUPS * SSD_STATE], axis=-1)
    xs = xs.reshape(b, l, SSD_HEADS, SSD_HEADDIM)
    Bm = Bm.reshape(b, l, SSD_GROUPS, SSD_STATE)
    Cm = Cm.reshape(b, l, SSD_GROUPS, SSD_STATE)
    dt = jax.nn.softplus(dt_raw.reshape(b, l, 2, SSD_HEADS).astype(jnp.float32)
                         + p['dt_bias'].astype(jnp.float32))
    return xs, dt, Bm, Cm


def _ssd_finish(y_dirs, xs, z, p):
    b, l = xs.shape[:2]
    y = y_dirs[0] + y_dirs[1] + p['d_skip'][:, None].astype(xs.dtype) * xs
    y = y.reshape(b, l, SSD_INNER) * jax.nn.silu(z)
    y = rms_norm(y.reshape(b, l, SSD_GROUPS, SSD_INNER // SSD_GROUPS),
                 p['ssd_norm'].reshape(SSD_GROUPS, SSD_INNER // SSD_GROUPS))
    return y.reshape(b, l, SSD_INNER)


def ssd_group(zx, xbcx, dtx, zc, xbcc, dtc, p, ctx_out):
    A = -jnp.exp(p['a_log'].astype(jnp.float32))
    xs_x, dt_x, B_x, C_x = _ssd_inputs(xbcx, dtx, p)
    xs_c, dt_c, B_c, C_c = _ssd_inputs(xbcc, dtc, p)
    b = xs_c.shape[0]
    h0 = jnp.zeros((b, SSD_GROUPS, SSD_HEADS // SSD_GROUPS, SSD_HEADDIM, SSD_STATE), jnp.float32)
    y_x, y_c = [], []
    for d in range(2):
        terms = ssd_chunk_terms(_orient(xs_c, d), _orient(dt_c[:, :, d], d), A[d], _orient(B_c, d))
        h_in_c, h_ctx = ssd_chunk_states(*terms, h0)
        if ctx_out:
            y_c.append(_orient(ssd_chunk_outputs(*terms, _orient(C_c, d), h_in_c), d))
        y = ssd_scan(_orient(xs_x, d), _orient(dt_x[:, :, d], d), A[d], _orient(B_x, d),
                     _orient(C_x, d), h_ctx)
        y_x.append(_orient(y, d))
    out_x = _ssd_finish(y_x, xs_x, zx, p)
    out_c = _ssd_finish(y_c, xs_c, zc, p) if ctx_out else None
    return out_x, out_c


def mla_qkv(cq, ckv, kr, p, rope):
    b, l = cq.shape[:2]
    q = (rms_norm(cq, p['q_norm']) @ p['w_uq']).reshape(b, l, MLA_HEADS, MLA_NOPE + MLA_ROPE)
    kv = (rms_norm(ckv, p['kv_norm']) @ p['w_ukv']).reshape(b, l, MLA_HEADS, MLA_NOPE + MLA_V)
    q_nope, q_rope = jnp.split(q, [MLA_NOPE], axis=-1)
    k_nope, v = jnp.split(kv, [MLA_NOPE], axis=-1)
    k_rope = kr[:, :, None, :]
    if rope is not None:
        q_rope = apply_axial_rope(q_rope, *rope)
        k_rope = apply_axial_rope(k_rope, *rope)
    q = jnp.concatenate([q_nope, q_rope], axis=-1)
    k = jnp.concatenate([k_nope, jnp.broadcast_to(k_rope, (b, l, MLA_HEADS, MLA_ROPE))], axis=-1)
    return q, k, v


def dense_attention_blocks(q, k, v):
    b, L, h, d = q.shape
    n = L // MLA_BLOCK
    scale = d ** -0.5
    qb = jnp.moveaxis(q.reshape(b, n, MLA_BLOCK, h, d), 1, 0)

    def one_block(qblk):
        s = jnp.einsum('bqhd,bkhd->bhqk', qblk, k).astype(jnp.float32) * scale
        pr = jax.nn.softmax(s, axis=-1).astype(v.dtype)
        return jnp.einsum('bhqk,bkhd->bqhd', pr, v)

    o = lax.map(one_block, qb)
    return jnp.moveaxis(o, 0, 1).reshape(b, L, h * v.shape[-1])


def swa_latent(q, k, v, kc, vc, sink):
    b, L, H, d = q.shape
    G = SWA_KV_HEADS
    r, W, n, Cn = H // G, SWA_BLOCK, L // SWA_BLOCK, kc.shape[1]
    scale = d ** -0.5
    qb = q.reshape(b, n, W, G, r, d)
    pad = ((0, 0), (W, W), (0, 0), (0, 0))
    kp = jnp.pad(k, pad).reshape(b, n + 2, W, G, d)
    vp = jnp.pad(v, pad).reshape(b, n + 2, W, G, d)
    kwin = jnp.concatenate([kp[:, :-2], kp[:, 1:-1], kp[:, 2:]], axis=2)
    vwin = jnp.concatenate([vp[:, :-2], vp[:, 1:-1], vp[:, 2:]], axis=2)
    qi = jnp.arange(W)
    kj = jnp.arange(3 * W)
    rel = kj[None, :] - W - qi[:, None]
    key_pos = jnp.arange(n)[:, None] * W - W + kj[None, :]
    mask = (jnp.abs(rel) <= SWA_WINDOW)[None] & ((key_pos >= 0) & (key_pos < L))[:, None, :]
    s_loc = jnp.einsum('bnqgrd,bnkgd->bngrqk', qb, kwin).astype(jnp.float32) * scale
    s_loc = jnp.where(mask[None, :, None, None], s_loc, -jnp.inf)
    s_ctx = jnp.einsum('bnqgrd,bcgd->bngrqc', qb, kc).astype(jnp.float32) * scale
    s_sink = jnp.broadcast_to(sink.reshape(G, r)[None, None, :, :, None, None].astype(jnp.float32),
                              s_ctx.shape[:-1] + (1,))
    pr = jax.nn.softmax(jnp.concatenate([s_ctx, s_loc, s_sink], axis=-1), axis=-1).astype(v.dtype)
    p_ctx, p_loc = pr[..., :Cn], pr[..., Cn:Cn + 3 * W]
    o = (jnp.einsum('bngrqc,bcgd->bnqgrd', p_ctx, vc)
         + jnp.einsum('bngrqk,bnkgd->bnqgrd', p_loc, vwin))
    return o.reshape(b, L, H * d)


def sink_attention_ctx(q, k, v, sink):
    b, Cn, H, d = q.shape
    G = k.shape[2]
    r = H // G
    qg = q.reshape(b, Cn, G, r, d)
    s = jnp.einsum('bqgrd,bkgd->bgrqk', qg, k).astype(jnp.float32) * (d ** -0.5)
    s_sink = jnp.broadcast_to(sink.reshape(G, r)[None, :, :, None, None].astype(jnp.float32),
                              s.shape[:-1] + (1,))
    pr = jax.nn.softmax(jnp.concatenate([s, s_sink], axis=-1), axis=-1)[..., :-1].astype(v.dtype)
    return jnp.einsum('bgrqk,bkgd->bqgrd', pr, v).reshape(b, Cn, H * d)


def head_group_mixing(ux, uc, p, rope_mla, rope_swa, ctx_out):
    b, L = ux.shape[:2]
    Cn = uc.shape[1]
    zx, xbcx, dtx, cqx, ckvx, krx, qsx, ksx, vsx = split_cols(ux @ p['w_in'])
    zc, xbcc, dtc, cqc, ckvc, krc, qsc, ksc, vsc = split_cols(uc @ p['w_in'])
    ssd_x, ssd_c = ssd_group(zx, xbcx, dtx, zc, xbcc, dtc, p, ctx_out)
    q_mx, k_mx, v_mx = mla_qkv(cqx, ckvx, krx, p, rope_mla)
    q_mc, k_mc, v_mc = mla_qkv(cqc, ckvc, krc, p, None)
    k_all = jnp.concatenate([k_mx, k_mc], axis=1)
    v_all = jnp.concatenate([v_mx, v_mc], axis=1)
    mla_x = dense_attention_blocks(q_mx, k_all, v_all)
    q_sx = apply_axial_rope(qsx.reshape(b, L, SWA_HEADS, SWA_HEADDIM), *rope_swa)
    k_sx = apply_axial_rope(ksx.reshape(b, L, SWA_KV_HEADS, SWA_HEADDIM), *rope_swa)
    v_sx = vsx.reshape(b, L, SWA_KV_HEADS, SWA_HEADDIM)
    k_sc = ksc.reshape(b, Cn, SWA_KV_HEADS, SWA_HEADDIM)
    v_sc = vsc.reshape(b, Cn, SWA_KV_HEADS, SWA_HEADDIM)
    swa_x = swa_latent(q_sx, k_sx, v_sx, k_sc, v_sc, p['sink'])
    ox = jnp.concatenate([ssd_x, rms_norm(mla_x, p['mla_out_norm']),
                          rms_norm(swa_x, p['swa_out_norm'])], axis=-1)
    if not ctx_out:
        return ox, None
    mla_c = dense_attention_blocks(q_mc, k_mc, v_mc)
    swa_c = sink_attention_ctx(qsc.reshape(b, Cn, SWA_HEADS, SWA_HEADDIM), k_sc, v_sc, p['sink'])
    oc = jnp.concatenate([ssd_c, rms_norm(mla_c, p['mla_out_norm']),
                          rms_norm(swa_c, p['swa_out_norm'])], axis=-1)
    return ox, oc


def setup_inputs(seed: int = 0) -> dict:
    key = jax.random.key(seed)
    ks = jax.random.split(key, 32)
    f32 = jnp.float32

    def nrm(k, shape, scale):
        return jax.random.normal(k, shape, f32) * scale

    def gain(k, shape):
        return 1.0 + 0.05 * jax.random.normal(k, shape, f32)

    u = jax.random.uniform(ks[14], (DEPTH, 2, SSD_HEADS), f32)
    dt0 = jnp.exp(u * (math.log(DT_MAX) - math.log(DT_MIN)) + math.log(DT_MIN))
    return {
        "x": nrm(ks[0], (BATCH, SEQ, D_MODEL), 1.0),
        "c": nrm(ks[1], (BATCH, D_MODEL), 1.0),
        "ctx": nrm(ks[2], (BATCH, CTX_LEN, D_MODEL), 1.0),
        "c_ctx": nrm(ks[3], (D_MODEL,), 1.0),
        "w_mod": nrm(ks[4], (DEPTH, D_MODEL, N_MOD * D_MODEL), 0.5 * D_MODEL ** -0.5),
        "b_mod": nrm(ks[5], (DEPTH, N_MOD * D_MODEL), 0.02),
        "norm_g": gain(ks[6], (DEPTH, 3, D_MODEL)),
        "ffn_w_gate": nrm(ks[7], (DEPTH, 2, D_MODEL, D_FF), D_MODEL ** -0.5),
        "ffn_w_up": nrm(ks[8], (DEPTH, 2, D_MODEL, D_FF), D_MODEL ** -0.5),
        "ffn_w_down": nrm(ks[9], (DEPTH, 2, D_FF, D_MODEL), D_FF ** -0.5),
        "w_in": nrm(ks[10], (DEPTH, D_MODEL, D_IN), D_MODEL ** -0.5),
        "w_out": nrm(ks[11], (DEPTH, D_MIX, D_MODEL), D_MIX ** -0.5),
        "ssd_conv_w": nrm(ks[12], (DEPTH, SSD_CONV, SSD_CONV_CH), SSD_CONV ** -0.5),
        "ssd_conv_b": nrm(ks[13], (DEPTH, SSD_CONV_CH), 0.02),
        "ssd_dt_bias": dt0 + jnp.log(-jnp.expm1(-dt0)),
        "ssd_a_log": jnp.log(jax.random.uniform(ks[15], (DEPTH, 2, SSD_HEADS), f32, 1.0, 16.0)),
        "ssd_d": 1.0 + 0.1 * jax.random.normal(ks[16], (DEPTH, SSD_HEADS), f32),
        "ssd_norm": gain(ks[17], (DEPTH, SSD_INNER)),
        "mla_q_norm": gain(ks[18], (DEPTH, MLA_Q_RANK)),
        "mla_w_uq": nrm(ks[19], (DEPTH, MLA_Q_RANK, MLA_HEADS * (MLA_NOPE + MLA_ROPE)), MLA_Q_RANK ** -0.5),
        "mla_kv_norm": gain(ks[20], (DEPTH, MLA_KV_RANK)),
        "mla_w_ukv": nrm(ks[21], (DEPTH, MLA_KV_RANK, MLA_HEADS * (MLA_NOPE + MLA_V)), MLA_KV_RANK ** -0.5),
        "mla_out_norm": gain(ks[22], (DEPTH, MLA_HEADS * MLA_V)),
        "swa_sink": nrm(ks[23], (DEPTH, SWA_HEADS), 0.5),
        "swa_out_norm": gain(ks[24], (DEPTH, SWA_HEADS * SWA_HEADDIM)),
        "final_norm": gain(ks[25], (D_MODEL,)),
    }


def reference(x, c, ctx, c_ctx, w_mod, b_mod, norm_g, ffn_w_gate, ffn_w_up, ffn_w_down,
              w_in, w_out, ssd_conv_w, ssd_conv_b, ssd_dt_bias, ssd_a_log, ssd_d, ssd_norm,
              mla_q_norm, mla_w_uq, mla_kv_norm, mla_w_ukv, mla_out_norm, swa_sink,
              swa_out_norm, final_norm):
    L = x.shape[1]
    ROWS = L // GRID_W
    pos_row = jnp.repeat(jnp.arange(ROWS, dtype=jnp.int32), GRID_W)
    pos_col = jnp.tile(jnp.arange(GRID_W, dtype=jnp.int32), ROWS)
    rope_mla = axial_rope_tables(pos_row, pos_col, MLA_ROPE)
    rope_swa = axial_rope_tables(pos_row, pos_col, SWA_HEADDIM)

    hx, hc = x, ctx
    for l in range(DEPTH):
        last = l == DEPTH - 1
        mod_x = jnp.split((jax.nn.silu(c) @ w_mod[l] + b_mod[l])[:, None, :], N_MOD, axis=-1)
        mod_c = jnp.split((jax.nn.silu(c_ctx) @ w_mod[l] + b_mod[l])[None, None, :], N_MOD, axis=-1)
        hx = hx + 0.5 * mod_x[2] * swiglu(modulate(rms_norm(hx, norm_g[l, 0]), mod_x[0], mod_x[1]),
                                          ffn_w_gate[l, 0], ffn_w_up[l, 0], ffn_w_down[l, 0])
        hc = hc + 0.5 * mod_c[2] * swiglu(modulate(rms_norm(hc, norm_g[l, 0]), mod_c[0], mod_c[1]),
                                          ffn_w_gate[l, 0], ffn_w_up[l, 0], ffn_w_down[l, 0])
        p = {"w_in": w_in[l], "conv_w": ssd_conv_w[l], "conv_b": ssd_conv_b[l],
             "dt_bias": ssd_dt_bias[l], "a_log": ssd_a_log[l], "d_skip": ssd_d[l],
             "ssd_norm": ssd_norm[l], "q_norm": mla_q_norm[l], "w_uq": mla_w_uq[l],
             "kv_norm": mla_kv_norm[l], "w_ukv": mla_w_ukv[l], "mla_out_norm": mla_out_norm[l],
             "sink": swa_sink[l], "swa_out_norm": swa_out_norm[l]}
        ux = modulate(rms_norm(hx, norm_g[l, 1]), mod_x[3], mod_x[4])
        uc = modulate(rms_norm(hc, norm_g[l, 1]), mod_c[3], mod_c[4])
        ox, oc = head_group_mixing(ux, uc, p, rope_mla, rope_swa, not last)
        hx = hx + mod_x[5] * (ox @ w_out[l])
        hx = hx + 0.5 * mod_x[8] * swiglu(modulate(rms_norm(hx, norm_g[l, 2]), mod_x[6], mod_x[7]),
                                          ffn_w_gate[l, 1], ffn_w_up[l, 1], ffn_w_down[l, 1])
        if not last:
            hc = hc + mod_c[5] * (oc @ w_out[l])
            hc = hc + 0.5 * mod_c[8] * swiglu(modulate(rms_norm(hc, norm_g[l, 2]), mod_c[6], mod_c[7]),
                                              ffn_w_gate[l, 1], ffn_w_up[l, 1], ffn_w_down[l, 1])
    return rms_norm(hx, final_norm)
```

```python
import functools
import math

import jax
import jax.numpy as jnp
import numpy as np
from jax import lax
from jax.experimental import pallas as pl
from jax.experimental.pallas import tpu as pltpu

F32 = jnp.float32
BF16 = jnp.bfloat16

D_MODEL = 2048
BATCH = 4
SEQ = 4096
DEPTH = 4
GRID_W = 64
CTX_LEN = 256
EPS = 1e-6
ROPE_THETA = 10000.0
N_MOD = 9
D_FF = 5632

SSD_HEADS = 16
SSD_HEADDIM = 64
SSD_INNER = SSD_HEADS * SSD_HEADDIM
SSD_GROUPS = 2
SSD_STATE = 128
SSD_CONV = 5
SSD_CHUNK = 128
SSD_CONV_CH = SSD_INNER + 2 * SSD_GROUPS * SSD_STATE

MLA_HEADS = 8
MLA_Q_RANK = 384
MLA_KV_RANK = 256
MLA_NOPE = 64
MLA_ROPE = 32
MLA_V = 64

SWA_HEADS = 8
SWA_KV_HEADS = 2
SWA_HEADDIM = 64
SWA_WINDOW = 128
SWA_BLOCK = 128

D_MIX = SSD_INNER + MLA_HEADS * MLA_V + SWA_HEADS * SWA_HEADDIM
IN_SIZES = (SSD_INNER, SSD_CONV_CH, 2 * SSD_HEADS, MLA_Q_RANK, MLA_KV_RANK, MLA_ROPE,
            SWA_HEADS * SWA_HEADDIM, SWA_KV_HEADS * SWA_HEADDIM, SWA_KV_HEADS * SWA_HEADDIM)

T_X = BATCH * SEQ
T_C = BATCH * CTX_LEN
T_ALL = T_X + T_C
N_GROUPS = BATCH + 1
GROUP_PAD = 8

LANE = 128
VMEM_LIMIT = 56 * 1024 * 1024

FFN_TM = 512
FFN_TF = 512
PROJ_TM = 256
OUT_TM = 512
PREP_TM = 512
ATT_TQ = 256
MOD_TN = 1024

SEG_Z = (0, 1024)
SEG_XBC = (1024, 2560)
SEG_CQ = (2560, 2944)
SEG_CKV = (2944, 3200)
SEG_QS = (3200, 4224)
SEG_KS = (4224, 4480)
SEG_VS = (4480, 4608)
SEG_SMALL = (4608, 4736)
W_IN_COLS = 4736
IN_SEGS = (SEG_Z, SEG_XBC, SEG_CQ, SEG_CKV, SEG_QS, SEG_KS, SEG_VS, SEG_SMALL)


def _cparams(sem):
    return pltpu.CompilerParams(dimension_semantics=sem, vmem_limit_bytes=VMEM_LIMIT)


def _group_of_tile(i, tm):
    return jnp.minimum((i * tm) // SEQ, BATCH)


def _rms(x):
    return x * lax.rsqrt(jnp.mean(x * x, axis=-1, keepdims=True) + EPS)


def _silu(x):
    return x * jax.nn.sigmoid(x)


def _mod_kernel(c_ref, w_ref, b_ref, o_ref):
    sc = _silu(c_ref[...]).astype(BF16)
    o_ref[...] = jnp.dot(sc, w_ref[...].astype(BF16), preferred_element_type=F32) + b_ref[...]


def _modulation(c_all, w_mod, b_mod):
    n = N_MOD * D_MODEL
    return pl.pallas_call(
        _mod_kernel,
        out_shape=jax.ShapeDtypeStruct((DEPTH, GROUP_PAD, n), F32),
        grid=(DEPTH, n // MOD_TN),
        in_specs=[
            pl.BlockSpec((GROUP_PAD, D_MODEL), lambda l, j: (0, 0)),
            pl.BlockSpec((None, D_MODEL, MOD_TN), lambda l, j: (l, 0, j)),
            pl.BlockSpec((None, 1, MOD_TN), lambda l, j: (l, 0, j)),
        ],
        out_specs=pl.BlockSpec((None, GROUP_PAD, MOD_TN), lambda l, j: (l, 0, j)),
        compiler_params=_cparams(("parallel", "parallel")),
        name="modulation",
    )(c_all, w_mod, b_mod.reshape(DEPTH, 1, n))


def _ffn_kernel(x_ref, mod_ref, g_ref, wg_ref, wu_ref, wd_ref, fin_ref, o_ref, u_ref, *,
                base, n_ff, final_norm):
    j = pl.program_id(1)

    @pl.when(j == 0)
    def _():
        y = _rms(x_ref[...]) * g_ref[...]
        u = y * (1.0 + mod_ref[base + 1:base + 2, :]) + mod_ref[base:base + 1, :]
        u_ref[...] = u.astype(BF16)
        o_ref[...] = jnp.zeros_like(o_ref)

    u = u_ref[...]
    gate = jnp.dot(u, wg_ref[...], preferred_element_type=F32)
    up = jnp.dot(u, wu_ref[...], preferred_element_type=F32)
    a = (_silu(gate) * up).astype(BF16)
    o_ref[...] += jnp.dot(a, wd_ref[...], preferred_element_type=F32)

    @pl.when(j == n_ff - 1)
    def _():
        h = x_ref[...] + (0.5 * mod_ref[base + 2:base + 3, :]) * o_ref[...]
        if final_norm:
            h = _rms(h) * fin_ref[...]
        o_ref[...] = h


def _ffn(h, mod_l, g, wg, wu, wd, fin, *, l, k, base, rows, final_norm=False):
    n_ff = D_FF // FFN_TF
    tm = FFN_TM
    kern = functools.partial(_ffn_kernel, base=base, n_ff=n_ff, final_norm=final_norm)
    return pl.pallas_call(
        kern,
        out_shape=jax.ShapeDtypeStruct((rows, D_MODEL), F32),
        grid=(rows // tm, n_ff),
        in_specs=[
            pl.BlockSpec((tm, D_MODEL), lambda i, j: (i, 0)),
            pl.BlockSpec((None, N_MOD, D_MODEL), lambda i, j: (_group_of_tile(i, tm), 0, 0)),
            pl.BlockSpec((1, D_MODEL), lambda i, j: (0, 0)),
            pl.BlockSpec((None, None, D_MODEL, FFN_TF), lambda i, j: (l, k, 0, j)),
            pl.BlockSpec((None, None, D_MODEL, FFN_TF), lambda i, j: (l, k, 0, j)),
            pl.BlockSpec((None, None, FFN_TF, D_MODEL), lambda i, j: (l, k, j, 0)),
            pl.BlockSpec((1, D_MODEL), lambda i, j: (0, 0)),
        ],
        out_specs=pl.BlockSpec((tm, D_MODEL), lambda i, j: (i, 0)),
        scratch_shapes=[pltpu.VMEM((tm, D_MODEL), BF16)],
        compiler_params=_cparams(("parallel", "arbitrary")),
        name="ffn",
    )(h, mod_l, g, wg, wu, wd, fin)


def _inproj_kernel(x_ref, mod_ref, g_ref, w_ref, *o_refs):
    y = _rms(x_ref[...]) * g_ref[...]
    u = (y * (1.0 + mod_ref[4:5, :]) + mod_ref[3:4, :]).astype(BF16)
    for (a, b), o_ref in zip(IN_SEGS, o_refs):
        o_ref[...] = jnp.dot(u, w_ref[:, a:b], preferred_element_type=F32)


def _inproj(h, mod_l, g, w_in_p, *, l):
    tm = PROJ_TM
    return pl.pallas_call(
        _inproj_kernel,
        out_shape=[jax.ShapeDtypeStruct((T_ALL, b - a), F32) for a, b in IN_SEGS],
        grid=(T_ALL // tm,),
        in_specs=[
            pl.BlockSpec((tm, D_MODEL), lambda i: (i, 0)),
            pl.BlockSpec((None, N_MOD, D_MODEL), lambda i: (_group_of_tile(i, tm), 0, 0)),
            pl.BlockSpec((1, D_MODEL), lambda i: (0, 0)),
            pl.BlockSpec((None, D_MODEL, W_IN_COLS), lambda i: (l, 0, 0),
                         pipeline_mode=pl.Buffered(1)),
        ],
        out_specs=[pl.BlockSpec((tm, b - a), lambda i: (i, 0)) for a, b in IN_SEGS],
        compiler_params=_cparams(("parallel",)),
        name="inproj",
    )(h, mod_l, g, w_in_p)


def _mla_prep_kernel(cq_ref, ckv_ref, sm_ref, tab_ref, gq_ref, gkv_ref, wq_ref, wk_ref, wv_ref,
                     q_ref, k_ref, v_ref):
    cq_tab = tab_ref[:, 0:LANE]
    sq_tab = tab_ref[:, LANE:2 * LANE]
    ck_tab = tab_ref[:, 2 * LANE:3 * LANE]
    sk_tab = tab_ref[:, 3 * LANE:4 * LANE]
    nq = (_rms(cq_ref[...]) * gq_ref[...]).astype(BF16)
    q = jnp.dot(nq, wq_ref[...], preferred_element_type=F32)
    for h in range(MLA_HEADS):
        qh = q[:, h * LANE:(h + 1) * LANE]
        qh = qh * cq_tab + pltpu.roll(qh, LANE - MLA_ROPE, 1) * sq_tab
        q_ref[:, h * LANE:(h + 1) * LANE] = qh.astype(BF16)
    nkv = _rms(ckv_ref[...]) * gkv_ref[...]
    sm = sm_ref[...]
    kr = sm * ck_tab + pltpu.roll(sm, LANE - MLA_ROPE, 1) * sk_tab
    lhs = jnp.concatenate([nkv, kr], axis=1).astype(BF16)
    k_ref[...] = jnp.dot(lhs, wk_ref[...], preferred_element_type=F32).astype(BF16)
    v_ref[...] = jnp.dot(lhs[:, :MLA_KV_RANK], wv_ref[...], preferred_element_type=F32).astype(BF16)


def _mla_prep(cq, ckv, small, tab, gq, gkv, wq, wk, wv, *, l):
    tm = PREP_TM
    tiles_per_seq = SEQ // tm

    def tab_idx(i):
        return jnp.where(i < T_X // tm, i % tiles_per_seq, tiles_per_seq)

    return pl.pallas_call(
        _mla_prep_kernel,
        out_shape=[jax.ShapeDtypeStruct((T_ALL, MLA_HEADS * LANE), BF16),
                   jax.ShapeDtypeStruct((T_ALL, MLA_HEADS * LANE), BF16),
                   jax.ShapeDtypeStruct((T_ALL, MLA_HEADS * MLA_V), BF16)],
        grid=(T_ALL // tm,),
        in_specs=[
            pl.BlockSpec((tm, MLA_Q_RANK), lambda i: (i, 0)),
            pl.BlockSpec((tm, MLA_KV_RANK), lambda i: (i, 0)),
            pl.BlockSpec((tm, LANE), lambda i: (i, 0)),
            pl.BlockSpec((tm, 4 * LANE), lambda i: (tab_idx(i), 0)),
            pl.BlockSpec((None, 1, MLA_Q_RANK), lambda i: (l, 0, 0)),
            pl.BlockSpec((None, 1, MLA_KV_RANK), lambda i: (l, 0, 0)),
            pl.BlockSpec((None, MLA_Q_RANK, MLA_HEADS * LANE), lambda i: (l, 0, 0)),
            pl.BlockSpec((None, MLA_KV_RANK + LANE, MLA_HEADS * LANE), lambda i: (l, 0, 0)),
            pl.BlockSpec((None, MLA_KV_RANK, MLA_HEADS * MLA_V), lambda i: (l, 0, 0)),
        ],
        out_specs=[pl.BlockSpec((tm, MLA_HEADS * LANE), lambda i: (i, 0)),
                   pl.BlockSpec((tm, MLA_HEADS * LANE), lambda i: (i, 0)),
                   pl.BlockSpec((tm, MLA_HEADS * MLA_V), lambda i: (i, 0))],
        compiler_params=_cparams(("parallel",)),
        name="mla_prep",
    )(cq, ckv, small, tab, gq, gkv, wq, wk, wv)


def _mla_attn_kernel(*refs, n_kv):
    q_ref = refs[0]
    k_refs = refs[1:1 + n_kv]
    v_refs = refs[1 + n_kv:1 + 2 * n_kv]
    o_ref = refs[1 + 2 * n_kv]
    outs = []
    for h in range(2):
        q = q_ref[:, h * LANE:(h + 1) * LANE]
        s = [lax.dot_general(q, k_ref[:, h * LANE:(h + 1) * LANE], (((1,), (1,)), ((), ())),
                             preferred_element_type=F32) for k_ref in k_refs]
        m = s[0].max(axis=-1, keepdims=True)
        for si in s[1:]:
            m = jnp.maximum(m, si.max(axis=-1, keepdims=True))
        p = [jnp.exp(si - m) for si in s]
        den = p[0].sum(axis=-1, keepdims=True)
        for pi in p[1:]:
            den = den + pi.sum(axis=-1, keepdims=True)
        o = jnp.dot(p[0].astype(BF16), v_refs[0][...], preferred_element_type=F32)
        for pi, v_ref in zip(p[1:], v_refs[1:]):
            o = o + jnp.dot(pi.astype(BF16), v_ref[...], preferred_element_type=F32)
        outs.append(o / den)
    lane = lax.broadcasted_iota(jnp.int32, outs[0].shape, 1)
    o_ref[...] = jnp.where(lane < MLA_V, outs[0], outs[1])


def _mla_attn_latent(q, k, v):
    tq = ATT_TQ
    qt = SEQ // tq
    ctx_blk0 = T_X // CTX_LEN
    return pl.pallas_call(
        functools.partial(_mla_attn_kernel, n_kv=2),
        out_shape=jax.ShapeDtypeStruct((T_X, MLA_HEADS * MLA_V), F32),
        grid=(BATCH, MLA_HEADS // 2, qt),
        in_specs=[
            pl.BlockSpec((tq, 2 * LANE), lambda b, h, t: (b * qt + t, h)),
            pl.BlockSpec((SEQ, 2 * LANE), lambda b, h, t: (b, h)),
            pl.BlockSpec((CTX_LEN, 2 * LANE), lambda b, h, t: (ctx_blk0 + b, h)),
            pl.BlockSpec((SEQ, LANE), lambda b, h, t: (b, h)),
            pl.BlockSpec((CTX_LEN, LANE), lambda b, h, t: (ctx_blk0 + b, h)),
        ],
        out_specs=pl.BlockSpec((tq, LANE), lambda b, h, t: (b * qt + t, h)),
        compiler_params=_cparams(("parallel", "parallel", "arbitrary")),
        name="mla_attn_latent",
    )(q, k, k, v, v)


def _mla_attn_ctx(q, k, v):
    ctx_blk0 = T_X // CTX_LEN
    return pl.pallas_call(
        functools.partial(_mla_attn_kernel, n_kv=1),
        out_shape=jax.ShapeDtypeStruct((T_C, MLA_HEADS * MLA_V), F32),
        grid=(BATCH, MLA_HEADS // 2),
        in_specs=[
            pl.BlockSpec((CTX_LEN, 2 * LANE), lambda b, h: (ctx_blk0 + b, h)),
            pl.BlockSpec((CTX_LEN, 2 * LANE), lambda b, h: (ctx_blk0 + b, h)),
            pl.BlockSpec((CTX_LEN, LANE), lambda b, h: (ctx_blk0 + b, h)),
        ],
        out_specs=pl.BlockSpec((CTX_LEN, LANE), lambda b, h: (b, h)),
        compiler_params=_cparams(("parallel", "parallel")),
        name="mla_attn_ctx",
    )(q, k, v)


def _outproj_kernel(h_ref, mod_ref, ssd_ref, mla_ref, swa_ref, gm_ref, gs_ref, w_ref, o_ref):
    n_mla = MLA_HEADS * MLA_V
    acc = jnp.dot(ssd_ref[...].astype(BF16), w_ref[0:SSD_INNER, :], preferred_element_type=F32)
    mla = (_rms(mla_ref[...]) * gm_ref[...]).astype(BF16)
    acc += jnp.dot(mla, w_ref[SSD_INNER:SSD_INNER + n_mla, :], preferred_element_type=F32)
    swa = (_rms(swa_ref[...]) * gs_ref[...]).astype(BF16)
    acc += jnp.dot(swa, w_ref[SSD_INNER + n_mla:, :], preferred_element_type=F32)
    o_ref[...] = h_ref[...] + mod_ref[5:6, :] * acc


def _outproj(h, mod_l, ssd, mla, swa, gm, gs, w_out, *, l, rows):
    tm = OUT_TM
    n_mla = MLA_HEADS * MLA_V
    n_swa = SWA_HEADS * SWA_HEADDIM
    return pl.pallas_call(
        _outproj_kernel,
        out_shape=jax.ShapeDtypeStruct((rows, D_MODEL), F32),
        grid=(rows // tm,),
        in_specs=[
            pl.BlockSpec((tm, D_MODEL), lambda i: (i, 0)),
            pl.BlockSpec((None, N_MOD, D_MODEL), lambda i: (_group_of_tile(i, tm), 0, 0)),
            pl.BlockSpec((tm, SSD_INNER), lambda i: (i, 0)),
            pl.BlockSpec((tm, n_mla), lambda i: (i, 0)),
            pl.BlockSpec((tm, n_swa), lambda i: (i, 0)),
            pl.BlockSpec((None, 1, n_mla), lambda i: (l, 0, 0)),
            pl.BlockSpec((None, 1, n_swa), lambda i: (l, 0, 0)),
            pl.BlockSpec((None, D_MIX, D_MODEL), lambda i: (l, 0, 0)),
        ],
        out_specs=pl.BlockSpec((tm, D_MODEL), lambda i: (i, 0)),
        compiler_params=_cparams(("parallel",)),
        name="outproj",
    )(h, mod_l, ssd, mla, swa, gm, gs, w_out)


def _rot_src_sign(dim):
    q = dim // 4
    src = np.concatenate([np.arange(q, 2 * q), np.arange(0, q),
                          np.arange(3 * q, 4 * q), np.arange(2 * q, 3 * q)])
    sign = np.concatenate([-np.ones(q), np.ones(q), -np.ones(q), np.ones(q)])
    return src, sign


def _w_in_layout():
    offs = np.concatenate([[0], np.cumsum(IN_SIZES)])
    o_z, o_xbc, o_dt, o_cq, o_ckv, o_kr, o_qs, o_ks, o_vs = offs[:-1]
    idx, sgn = [], []

    def plain(start, n):
        idx.append(np.arange(start, start + n))
        sgn.append(np.ones(n))

    def rotated(start, heads, dim):
        src, sign = _rot_src_sign(dim)
        for h in range(heads):
            idx.append(start + h * dim + src)
            sgn.append(sign)

    plain(o_z, SSD_INNER)
    plain(o_xbc, SSD_CONV_CH)
    plain(o_cq, MLA_Q_RANK)
    plain(o_ckv, MLA_KV_RANK)
    plain(o_qs, SWA_HEADS * SWA_HEADDIM)
    rotated(o_qs, SWA_HEADS, SWA_HEADDIM)
    plain(o_ks, SWA_KV_HEADS * SWA_HEADDIM)
    rotated(o_ks, SWA_KV_HEADS, SWA_HEADDIM)
    plain(o_vs, SWA_KV_HEADS * SWA_HEADDIM)
    plain(o_dt, 2 * SSD_HEADS)
    plain(o_kr, MLA_ROPE)
    rotated(o_kr, 1, MLA_ROPE)
    idx.append(np.zeros(32, np.int64))
    sgn.append(np.zeros(32))
    idx = np.concatenate(idx).astype(np.int32)
    sgn = np.concatenate(sgn).astype(np.float32)
    assert idx.shape[0] == W_IN_COLS
    return idx, sgn


def _w_uq_layout():
    hd = MLA_NOPE + MLA_ROPE
    src, sign = _rot_src_sign(MLA_ROPE)
    idx, sgn = [], []
    for h in range(MLA_HEADS):
        idx.append(h * hd + np.arange(hd))
        sgn.append(np.ones(hd))
        idx.append(h * hd + MLA_NOPE + src)
        sgn.append(sign)
    return np.concatenate(idx).astype(np.int32), np.concatenate(sgn).astype(np.float32)


def _w_ukv_layout():
    hd = MLA_NOPE + MLA_V
    kidx, ksgn, vidx = [], [], []
    for h in range(MLA_HEADS):
        kidx.append(h * hd + np.arange(MLA_NOPE))
        ksgn.append(np.ones(MLA_NOPE))
        kidx.append(np.zeros(LANE - MLA_NOPE, np.int64))
        ksgn.append(np.zeros(LANE - MLA_NOPE))
        vidx.append(h * hd + MLA_NOPE + np.arange(MLA_V))
    return (np.concatenate(kidx).astype(np.int32), np.concatenate(ksgn).astype(np.float32),
            np.concatenate(vidx).astype(np.int32))


def _k_rope_expand():
    e = np.zeros((LANE, MLA_HEADS * LANE), np.float32)
    for h in range(MLA_HEADS):
        for i in range(MLA_ROPE):
            e[MLA_ROPE + i, h * LANE + MLA_NOPE + i] = 1.0
    return e


def _rope_angles(dim):
    pos = np.arange(SEQ)
    quarter = dim // 4
    inv_freq = ROPE_THETA ** (-jnp.arange(quarter, dtype=F32) / quarter)
    ang_r = jnp.asarray(pos // GRID_W, F32)[:, None] * inv_freq[None, :]
    ang_c = jnp.asarray(pos % GRID_W, F32)[:, None] * inv_freq[None, :]
    ang = jnp.concatenate([ang_r, ang_r, ang_c, ang_c], axis=-1)
    return jnp.cos(ang), jnp.sin(ang)


def _mla_rope_table():
    cos, sin = _rope_angles(MLA_ROPE)
    scale = (MLA_NOPE + MLA_ROPE) ** -0.5
    n = SEQ + PREP_TM
    ident = jnp.ones((PREP_TM, MLA_ROPE), F32)
    zeros = jnp.zeros((PREP_TM, MLA_ROPE), F32)
    cos = jnp.concatenate([cos, ident], axis=0)
    sin = jnp.concatenate([sin, zeros], axis=0)
    z32 = jnp.zeros((n, MLA_ROPE), F32)
    cq = jnp.concatenate([jnp.ones((n, MLA_NOPE), F32), cos, z32], axis=1) * scale
    sq = jnp.concatenate([jnp.zeros((n, MLA_NOPE), F32), sin, z32], axis=1) * scale
    ck = jnp.concatenate([z32, cos, z32, z32], axis=1)
    sk = jnp.concatenate([z32, sin, z32, z32], axis=1)
    return jnp.concatenate([cq, sq, ck, sk], axis=1)


def _dwconv_centred(x, w, b):
    y = lax.conv_general_dilated(
        x, w[:, None, :].astype(x.dtype), window_strides=(1,),
        padding=[(SSD_CONV // 2, SSD_CONV // 2)],
        dimension_numbers=("NWC", "WIO", "NWC"), feature_group_count=x.shape[-1])
    return y + b.astype(x.dtype)


def _ssd_chunk_terms(xs, dt, A, Bm):
    b, l, h, pdim = xs.shape
    g, n = Bm.shape[2], Bm.shape[3]
    r, nc = h // g, l // SSD_CHUNK
    xdt = (xs * dt[..., None]).reshape(b, nc, SSD_CHUNK, g, r, pdim)
    a_cum = jnp.cumsum((dt * A).reshape(b, nc, SSD_CHUNK, g, r), axis=2)
    Bc = Bm.reshape(b, nc, SSD_CHUNK, g, n)
    return xdt, a_cum, Bc


def _ssd_chunk_states(xdt, a_cum, Bc, h0):
    decay_to_end = jnp.exp(a_cum[:, :, -1:] - a_cum).astype(xdt.dtype)
    s_local = jnp.einsum('bcsgn,bcsgr,bcsgrp->bcgrpn', Bc, decay_to_end, xdt).astype(F32)
    chunk_decay = jnp.exp(a_cum[:, :, -1])

    def step(h, inp):
        s, dec = inp
        return dec[..., None, None] * h + s, h

    h_final, h_in = lax.scan(step, h0, (jnp.moveaxis(s_local, 1, 0), jnp.moveaxis(chunk_decay, 1, 0)))
    return jnp.moveaxis(h_in, 0, 1), h_final


def _ssd_chunk_outputs(xdt, a_cum, Bc, Cm, h_in):
    b, nc, q, g, r, pdim = xdt.shape
    Cc = Cm.reshape(b, nc, q, g, -1)
    seg = a_cum[:, :, :, None] - a_cum[:, :, None, :]
    lower_tri = jnp.tril(jnp.ones((q, q), bool))[None, None, :, :, None, None]
    decay = jnp.exp(jnp.where(lower_tri, seg, -jnp.inf)).astype(xdt.dtype)
    cb = jnp.einsum('bclgn,bcsgn->bclsg', Cc, Bc)
    y_diag = jnp.einsum('bclsg,bclsgr,bcsgrp->bclgrp', cb, decay, xdt)
    y_off = jnp.einsum('bclgn,bclgr,bcgrpn->bclgrp', Cc, jnp.exp(a_cum).astype(xdt.dtype),
                       h_in.astype(xdt.dtype))
    return (y_diag + y_off).reshape(b, nc * q, g * r, pdim)


def _orient(t, d):
    return t if d == 0 else jnp.flip(t, axis=1)


def _ssd_inputs(xbc, dt_raw, p):
    b, l = xbc.shape[:2]
    xbc = jax.nn.silu(_dwconv_centred(xbc, p['conv_w'], p['conv_b']))
    xs, Bm, Cm = jnp.split(xbc, [SSD_INNER, SSD_INNER + SSD_GROUPS * SSD_STATE], axis=-1)
    xs = xs.reshape(b, l, SSD_HEADS, SSD_HEADDIM)
    Bm = Bm.reshape(b, l, SSD_GROUPS, SSD_STATE)
    Cm = Cm.reshape(b, l, SSD_GROUPS, SSD_STATE)
    dt = jax.nn.softplus(dt_raw.reshape(b, l, 2, SSD_HEADS).astype(F32) + p['dt_bias'].astype(F32))
    return xs, dt, Bm, Cm


def _rms_g(x, g):
    return _rms(x) * g


def _ssd_finish(y_dirs, xs, z, p):
    b, l = xs.shape[:2]
    y = y_dirs[0] + y_dirs[1] + p['d_skip'][:, None].astype(xs.dtype) * xs
    y = y.reshape(b, l, SSD_INNER) * jax.nn.silu(z)
    y = _rms_g(y.reshape(b, l, SSD_GROUPS, SSD_INNER // SSD_GROUPS),
               p['ssd_norm'].reshape(SSD_GROUPS, SSD_INNER // SSD_GROUPS))
    return y.reshape(b, l, SSD_INNER)


def _ssd_group(zx, xbcx, dtx, zc, xbcc, dtc, p, ctx_out):
    A = -jnp.exp(p['a_log'].astype(F32))
    xs_x, dt_x, B_x, C_x = _ssd_inputs(xbcx, dtx, p)
    xs_c, dt_c, B_c, C_c = _ssd_inputs(xbcc, dtc, p)
    b = xs_c.shape[0]
    h0 = jnp.zeros((b, SSD_GROUPS, SSD_HEADS // SSD_GROUPS, SSD_HEADDIM, SSD_STATE), F32)
    y_x, y_c = [], []
    for d in range(2):
        terms = _ssd_chunk_terms(_orient(xs_c, d), _orient(dt_c[:, :, d], d), A[d], _orient(B_c, d))
        h_in_c, h_ctx = _ssd_chunk_states(*terms, h0)
        if ctx_out:
            y_c.append(_orient(_ssd_chunk_outputs(*terms, _orient(C_c, d), h_in_c), d))
        xdt, a_cum, Bc = _ssd_chunk_terms(_orient(xs_x, d), _orient(dt_x[:, :, d], d), A[d],
                                          _orient(B_x, d))
        h_in, _ = _ssd_chunk_states(xdt, a_cum, Bc, h_ctx)
        y = _ssd_chunk_outputs(xdt, a_cum, Bc, _orient(C_x, d), h_in)
        y_x.append(_orient(y, d))
    out_x = _ssd_finish(y_x, xs_x, zx, p)
    out_c = _ssd_finish(y_c, xs_c, zc, p) if ctx_out else None
    return out_x, out_c


def _swa_latent(q, k, v, kc, vc, sink):
    b, L, H, d = q.shape
    G = SWA_KV_HEADS
    r, W, n, Cn = H // G, SWA_BLOCK, L // SWA_BLOCK, kc.shape[1]
    scale = d ** -0.5
    qb = q.reshape(b, n, W, G, r, d)
    pad = ((0, 0), (W, W), (0, 0), (0, 0))
    kp = jnp.pad(k, pad).reshape(b, n + 2, W, G, d)
    vp = jnp.pad(v, pad).reshape(b, n + 2, W, G, d)
    kwin = jnp.concatenate([kp[:, :-2], kp[:, 1:-1], kp[:, 2:]], axis=2)
    vwin = jnp.concatenate([vp[:, :-2], vp[:, 1:-1], vp[:, 2:]], axis=2)
    qi = jnp.arange(W)
    kj = jnp.arange(3 * W)
    rel = kj[None, :] - W - qi[:, None]
    key_pos = jnp.arange(n)[:, None] * W - W + kj[None, :]
    mask = (jnp.abs(rel) <= SWA_WINDOW)[None] & ((key_pos >= 0) & (key_pos < L))[:, None, :]
    s_loc = jnp.einsum('bnqgrd,bnkgd->bngrqk', qb, kwin).astype(F32) * scale
    s_loc = jnp.where(mask[None, :, None, None], s_loc, -jnp.inf)
    s_ctx = jnp.einsum('bnqgrd,bcgd->bngrqc', qb, kc).astype(F32) * scale
    s_sink = jnp.broadcast_to(sink.reshape(G, r)[None, None, :, :, None, None].astype(F32),
                              s_ctx.shape[:-1] + (1,))
    pr = jax.nn.softmax(jnp.concatenate([s_ctx, s_loc, s_sink], axis=-1), axis=-1).astype(v.dtype)
    p_ctx, p_loc = pr[..., :Cn], pr[..., Cn:Cn + 3 * W]
    o = (jnp.einsum('bngrqc,bcgd->bnqgrd', p_ctx, vc)
         + jnp.einsum('bngrqk,bnkgd->bnqgrd', p_loc, vwin))
    return o.reshape(b, L, H * d)


def _sink_attention_ctx(q, k, v, sink):
    b, Cn, H, d = q.shape
    G = k.shape[2]
    r = H // G
    qg = q.reshape(b, Cn, G, r, d)
    s = jnp.einsum('bqgrd,bkgd->bgrqk', qg, k).astype(F32) * (d ** -0.5)
    s_sink = jnp.broadcast_to(sink.reshape(G, r)[None, :, :, None, None].astype(F32),
                              s.shape[:-1] + (1,))
    pr = jax.nn.softmax(jnp.concatenate([s, s_sink], axis=-1), axis=-1)[..., :-1].astype(v.dtype)
    return jnp.einsum('bgrqk,bkgd->bqgrd', pr, v).reshape(b, Cn, H * d)


def _split_rows(a):
    n = a.shape[-1]
    return a[:T_X].reshape(BATCH, SEQ, n), a[T_X:].reshape(BATCH, CTX_LEN, n)


def kernel(x, c, ctx, c_ctx, w_mod, b_mod, norm_g, ffn_w_gate, ffn_w_up, ffn_w_down,
           w_in, w_out, ssd_conv_w, ssd_conv_b, ssd_dt_bias, ssd_a_log, ssd_d, ssd_norm,
           mla_q_norm, mla_w_uq, mla_kv_norm, mla_w_ukv, mla_out_norm, swa_sink,
           swa_out_norm, final_norm):
    idx, sgn = _w_in_layout()
    w_in_p = (w_in[:, :, idx] * sgn).astype(BF16)
    qidx, qsgn = _w_uq_layout()
    wq = (mla_w_uq[:, :, qidx] * qsgn).astype(BF16)
    kidx, ksgn, vidx = _w_ukv_layout()
    wk = mla_w_ukv[:, :, kidx] * ksgn
    wk = jnp.concatenate([wk, jnp.broadcast_to(jnp.asarray(_k_rope_expand()), (DEPTH, LANE, MLA_HEADS * LANE))],
                         axis=1).astype(BF16)
    wv = mla_w_ukv[:, :, vidx].astype(BF16)
    wg = ffn_w_gate.astype(BF16)
    wu = ffn_w_up.astype(BF16)
    wd = ffn_w_down.astype(BF16)
    wo = w_out.astype(BF16)
    mla_tab = _mla_rope_table()
    cos_s, sin_s = _rope_angles(SWA_HEADDIM)

    c_all = jnp.concatenate([c, c_ctx[None, :], jnp.zeros((GROUP_PAD - N_GROUPS, D_MODEL), F32)], axis=0)
    mod_all = _modulation(c_all, w_mod, b_mod).reshape(DEPTH, GROUP_PAD, N_MOD, D_MODEL)

    h = jnp.concatenate([x.reshape(T_X, D_MODEL), ctx.reshape(T_C, D_MODEL)], axis=0)
    fin = final_norm.reshape(1, D_MODEL)

    for l in range(DEPTH):
        last = l == DEPTH - 1
        mod_l = mod_all[l]
        h = _ffn(h, mod_l, norm_g[l, 0].reshape(1, D_MODEL), wg, wu, wd, fin,
                 l=l, k=0, base=0, rows=T_ALL)
        z, xbc, cq, ckv, qs2, ks2, vs, small = _inproj(h, mod_l, norm_g[l, 1].reshape(1, D_MODEL),
                                                      w_in_p, l=l)
        p = {"conv_w": ssd_conv_w[l], "conv_b": ssd_conv_b[l], "dt_bias": ssd_dt_bias[l],
             "a_log": ssd_a_log[l], "d_skip": ssd_d[l], "ssd_norm": ssd_norm[l]}
        zx, zc = _split_rows(z)
        xbcx, xbcc = _split_rows(xbc)
        dtx, dtc = _split_rows(small[:, :2 * SSD_HEADS])
        ssd_x, ssd_c = _ssd_group(zx, xbcx, dtx, zc, xbcc, dtc, p, not last)
        q_m, k_m, v_m = _mla_prep(cq, ckv, small, mla_tab,
                                  mla_q_norm.reshape(DEPTH, 1, MLA_Q_RANK),
                                  mla_kv_norm.reshape(DEPTH, 1, MLA_KV_RANK), wq, wk, wv, l=l)
        mla_x = _mla_attn_latent(q_m, k_m, v_m)
        n_qs = SWA_HEADS * SWA_HEADDIM
        n_ks = SWA_KV_HEADS * SWA_HEADDIM
        qsx, qsc = _split_rows(qs2)
        ksx, ksc = _split_rows(ks2)
        vsx, vsc = _split_rows(vs)
        cs = cos_s[None, :, None, :]
        sn = sin_s[None, :, None, :]
        q_sx = (qsx[..., :n_qs].reshape(BATCH, SEQ, SWA_HEADS, SWA_HEADDIM) * cs
                + qsx[..., n_qs:].reshape(BATCH, SEQ, SWA_HEADS, SWA_HEADDIM) * sn)
        k_sx = (ksx[..., :n_ks].reshape(BATCH, SEQ, SWA_KV_HEADS, SWA_HEADDIM) * cs
                + ksx[..., n_ks:].reshape(BATCH, SEQ, SWA_KV_HEADS, SWA_HEADDIM) * sn)
        v_sx = vsx.reshape(BATCH, SEQ, SWA_KV_HEADS, SWA_HEADDIM)
        k_sc = ksc[..., :n_ks].reshape(BATCH, CTX_LEN, SWA_KV_HEADS, SWA_HEADDIM)
        v_sc = vsc.reshape(BATCH, CTX_LEN, SWA_KV_HEADS, SWA_HEADDIM)
        swa_x = _swa_latent(q_sx, k_sx, v_sx, k_sc, v_sc, swa_sink[l])

        if last:
            rows = T_X
            ssd_o = ssd_x.reshape(T_X, SSD_INNER)
            mla_o = mla_x
            swa_o = swa_x.reshape(T_X, n_qs)
        else:
            rows = T_ALL
            mla_c = _mla_attn_ctx(q_m, k_m, v_m)
            swa_c = _sink_attention_ctx(qsc[..., :n_qs].reshape(BATCH, CTX_LEN, SWA_HEADS, SWA_HEADDIM),
                                        k_sc, v_sc, swa_sink[l])
            ssd_o = jnp.concatenate([ssd_x.reshape(T_X, SSD_INNER), ssd_c.reshape(T_C, SSD_INNER)], axis=0)
            mla_o = jnp.concatenate([mla_x, mla_c], axis=0)
            swa_o = jnp.concatenate([swa_x.reshape(T_X, n_qs), swa_c.reshape(T_C, n_qs)], axis=0)
        h = _outproj(h, mod_l, ssd_o, mla_o, swa_o,
                     mla_out_norm.reshape(DEPTH, 1, -1), swa_out_norm.reshape(DEPTH, 1, -1), wo,
                     l=l, rows=rows)
        h = _ffn(h, mod_l, norm_g[l, 2].reshape(1, D_MODEL), wg, wu, wd, fin,
                 l=l, k=1, base=6, rows=rows, final_norm=last)
    return h.reshape(BATCH, SEQ, D_MODEL)
```

```python
import functools

import jax
import jax.numpy as jnp
import numpy as np
from jax import lax
from jax.experimental import pallas as pl
from jax.experimental.pallas import tpu as pltpu

F32 = jnp.float32
BF16 = jnp.bfloat16
HIGHEST = lax.Precision.HIGHEST

D_MODEL = 2048
BATCH = 4
SEQ = 4096
DEPTH = 4
GRID_W = 64
CTX_LEN = 256
EPS = 1e-6
ROPE_THETA = 10000.0
N_MOD = 9
D_FF = 5632

SSD_HEADS = 16
SSD_HEADDIM = 64
SSD_INNER = SSD_HEADS * SSD_HEADDIM
SSD_GROUPS = 2
SSD_STATE = 128
SSD_CONV = 5
SSD_CONV_CH = SSD_INNER + 2 * SSD_GROUPS * SSD_STATE

MLA_HEADS = 8
MLA_Q_RANK = 384
MLA_KV_RANK = 256
MLA_NOPE = 64
MLA_ROPE = 32
MLA_V = 64

SWA_HEADS = 8
SWA_KV_HEADS = 2
SWA_HEADDIM = 64
SWA_WINDOW = 128
SWA_BLOCK = 128

D_MIX = SSD_INNER + MLA_HEADS * MLA_V + SWA_HEADS * SWA_HEADDIM
IN_SIZES = (SSD_INNER, SSD_CONV_CH, 2 * SSD_HEADS, MLA_Q_RANK, MLA_KV_RANK, MLA_ROPE,
            SWA_HEADS * SWA_HEADDIM, SWA_KV_HEADS * SWA_HEADDIM, SWA_KV_HEADS * SWA_HEADDIM)

T_X = BATCH * SEQ
T_C = BATCH * CTX_LEN
T_ALL = T_X + T_C
N_GROUPS = BATCH + 1
GROUP_PAD = 8

LANE = 128
SUBLANE = 8
VMEM_LIMIT = 56 * 1024 * 1024
NEG = -1e30

FFN_TM = 512
FFN_TF = 512
PROJ_TM = 256
OUT_TM = 512
PREP_TM = 512
ATT_TQ = 256
MOD_TN = 1024

SSD_Q = 128
SEQ_CHUNKS = SEQ // SSD_Q
CTX_CHUNKS = CTX_LEN // SSD_Q
SSD_STEPS = SEQ_CHUNKS + CTX_CHUNKS
HALO = SUBLANE
N_SWA_Q = SWA_HEADS * SWA_HEADDIM
N_SWA_KV2 = 2 * SWA_KV_HEADS * SWA_HEADDIM

SEG_Z = (0, 1024)
SEG_XBC = (1024, 2560)
SEG_CQ = (2560, 2944)
SEG_CKV = (2944, 3200)
SEG_QS = (3200, 4224)
SEG_KS = (4224, 4736)
SEG_VS = (4736, 4992)
SEG_SMALL = (4992, 5120)
W_IN_COLS = 5120
IN_SEGS = (SEG_Z, SEG_XBC, SEG_CQ, SEG_CKV, SEG_QS, SEG_KS, SEG_VS, SEG_SMALL)


def _cparams(sem):
    return pltpu.CompilerParams(dimension_semantics=sem, vmem_limit_bytes=VMEM_LIMIT)


def _group_of_tile(i, tm):
    return jnp.minimum((i * tm) // SEQ, BATCH)


def _rms(x):
    return x * lax.rsqrt(jnp.mean(x * x, axis=-1, keepdims=True) + EPS)


def _silu(x):
    return x * jax.nn.sigmoid(x)


def _softplus(x):
    return jnp.maximum(x, 0.0) + jnp.log1p(jnp.exp(-jnp.abs(x)))


def _dot(a, b):
    return jnp.dot(a, b, preferred_element_type=F32)


def _dot_t(a, b):
    return lax.dot_general(a, b, (((1,), (1,)), ((), ())), preferred_element_type=F32)


def _mod_kernel(c_ref, w_ref, b_ref, o_ref):
    sc = _silu(c_ref[...]).astype(BF16)
    o_ref[...] = _dot(sc, w_ref[...].astype(BF16)) + b_ref[...]


def _modulation(c_all, w_mod, b_mod):
    n = N_MOD * D_MODEL
    return pl.pallas_call(
        _mod_kernel,
        out_shape=jax.ShapeDtypeStruct((DEPTH, GROUP_PAD, n), F32),
        grid=(DEPTH, n // MOD_TN),
        in_specs=[
            pl.BlockSpec((GROUP_PAD, D_MODEL), lambda l, j: (0, 0)),
            pl.BlockSpec((None, D_MODEL, MOD_TN), lambda l, j: (l, 0, j)),
            pl.BlockSpec((None, 1, MOD_TN), lambda l, j: (l, 0, j)),
        ],
        out_specs=pl.BlockSpec((None, GROUP_PAD, MOD_TN), lambda l, j: (l, 0, j)),
        compiler_params=_cparams(("parallel", "parallel")),
        name="modulation",
    )(c_all, w_mod, b_mod.reshape(DEPTH, 1, n))


def _ffn_kernel(x_ref, mod_ref, g_ref, wg_ref, wu_ref, wd_ref, fin_ref, o_ref, u_ref, *,
                base, n_ff, final_norm):
    j = pl.program_id(1)

    @pl.when(j == 0)
    def _():
        y = _rms(x_ref[...]) * g_ref[...]
        u = y * (1.0 + mod_ref[base + 1:base + 2, :]) + mod_ref[base:base + 1, :]
        u_ref[...] = u.astype(BF16)
        o_ref[...] = jnp.zeros_like(o_ref)

    u = u_ref[...]
    gate = _dot(u, wg_ref[...])
    up = _dot(u, wu_ref[...])
    a = (_silu(gate) * up).astype(BF16)
    o_ref[...] += _dot(a, wd_ref[...])

    @pl.when(j == n_ff - 1)
    def _():
        h = x_ref[...] + (0.5 * mod_ref[base + 2:base + 3, :]) * o_ref[...]
        if final_norm:
            h = _rms(h) * fin_ref[...]
        o_ref[...] = h


def _ffn(h, mod_l, g, wg, wu, wd, fin, *, l, k, base, rows, final_norm=False):
    n_ff = D_FF // FFN_TF
    tm = FFN_TM
    kern = functools.partial(_ffn_kernel, base=base, n_ff=n_ff, final_norm=final_norm)
    return pl.pallas_call(
        kern,
        out_shape=jax.ShapeDtypeStruct((rows, D_MODEL), F32),
        grid=(rows // tm, n_ff),
        in_specs=[
            pl.BlockSpec((tm, D_MODEL), lambda i, j: (i, 0)),
            pl.BlockSpec((None, N_MOD, D_MODEL), lambda i, j: (_group_of_tile(i, tm), 0, 0)),
            pl.BlockSpec((1, D_MODEL), lambda i, j: (0, 0)),
            pl.BlockSpec((None, None, D_MODEL, FFN_TF), lambda i, j: (l, k, 0, j)),
            pl.BlockSpec((None, None, D_MODEL, FFN_TF), lambda i, j: (l, k, 0, j)),
            pl.BlockSpec((None, None, FFN_TF, D_MODEL), lambda i, j: (l, k, j, 0)),
            pl.BlockSpec((1, D_MODEL), lambda i, j: (0, 0)),
        ],
        out_specs=pl.BlockSpec((tm, D_MODEL), lambda i, j: (i, 0)),
        scratch_shapes=[pltpu.VMEM((tm, D_MODEL), BF16)],
        compiler_params=_cparams(("parallel", "arbitrary")),
        name="ffn",
    )(h, mod_l, g, wg, wu, wd, fin)


def _inproj_kernel(x_ref, mod_ref, g_ref, w_ref, *o_refs):
    y = _rms(x_ref[...]) * g_ref[...]
    u = (y * (1.0 + mod_ref[4:5, :]) + mod_ref[3:4, :]).astype(BF16)
    for (a, b), o_ref in zip(IN_SEGS, o_refs):
        o_ref[...] = _dot(u, w_ref[:, a:b])


def _inproj(h, mod_l, g, w_in_p, *, l):
    tm = PROJ_TM
    return pl.pallas_call(
        _inproj_kernel,
        out_shape=[jax.ShapeDtypeStruct((T_ALL, b - a), F32) for a, b in IN_SEGS],
        grid=(T_ALL // tm,),
        in_specs=[
            pl.BlockSpec((tm, D_MODEL), lambda i: (i, 0)),
            pl.BlockSpec((None, N_MOD, D_MODEL), lambda i: (_group_of_tile(i, tm), 0, 0)),
            pl.BlockSpec((1, D_MODEL), lambda i: (0, 0)),
            pl.BlockSpec((None, D_MODEL, W_IN_COLS), lambda i: (l, 0, 0),
                         pipeline_mode=pl.Buffered(1)),
        ],
        out_specs=[pl.BlockSpec((tm, b - a), lambda i: (i, 0)) for a, b in IN_SEGS],
        compiler_params=_cparams(("parallel",)),
        name="inproj",
    )(h, mod_l, g, w_in_p)


def _ssd_chunk_ids(phase, s):
    fwd = phase == 1
    is_ctx = s < CTX_CHUNKS
    cc = jnp.where(fwd, s, CTX_CHUNKS - 1 - s)
    lc = jnp.where(fwd, s - CTX_CHUNKS, SSD_STEPS - 1 - s)
    return is_ctx, cc, lc


def _ssd_row_block(b, phase, s):
    is_ctx, cc, lc = _ssd_chunk_ids(phase, s)
    return jnp.where(is_ctx, T_X // SSD_Q + b * CTX_CHUNKS + cc, b * SEQ_CHUNKS + lc)


def _ssd_kernel(xbc_ref, prev_ref, next_ref, z_ref, dtc_ref, dtr_ref, cw_ref, cb_ref,
                bias_r_ref, bias_c_ref, alog_r_ref, alog_c_ref, dskip_ref, gn_ref, e16_ref,
                o_ref, xe_ref, state_ref, ybwd_ref, y_ref):
    phase = pl.program_id(1)
    s = pl.program_id(2)
    fwd = phase == 1
    is_ctx, cc, lc = _ssd_chunk_ids(phase, s)
    first = jnp.where(is_ctx, cc == 0, lc == 0)
    last = jnp.where(is_ctx, cc == CTX_CHUNKS - 1, lc == SEQ_CHUNKS - 1)
    voff = pl.multiple_of(jnp.where(is_ctx, cc, CTX_CHUNKS + lc) * SSD_Q, SSD_Q)

    @pl.when(s == 0)
    def _():
        state_ref[...] = jnp.zeros_like(state_ref)

    xe_ref[0:HALO, :] = jnp.where(first, 0.0, prev_ref[...])
    xe_ref[HALO:HALO + SSD_Q, :] = xbc_ref[...]
    xe_ref[HALO + SSD_Q:, :] = jnp.where(last, 0.0, next_ref[...])
    acc = cb_ref[...] + cw_ref[0:1, :] * xe_ref[HALO - 2:HALO - 2 + SSD_Q, :]
    for k in range(1, SSD_CONV):
        acc = acc + cw_ref[k:k + 1, :] * xe_ref[HALO - 2 + k:HALO - 2 + k + SSD_Q, :]
    xc = _silu(acc)
    xs = xc[:, :SSD_INNER]
    n_bc = SSD_GROUPS * SSD_STATE

    dt_c = _softplus(dtc_ref[...] + bias_r_ref[...])
    dt_r = _softplus(dtr_ref[...] + bias_c_ref[...])
    a_c = dt_c * -jnp.exp(alog_r_ref[...])
    a_r = dt_r * -jnp.exp(alog_c_ref[...])
    row = lax.broadcasted_iota(jnp.int32, (SSD_Q, SSD_Q), 0)
    col = lax.broadcasted_iota(jnp.int32, (SSD_Q, SSD_Q), 1)
    d = (row - col) * jnp.where(fwd, 1, -1)
    causal = d >= 0
    s_c = jnp.dot(causal.astype(F32), a_c, precision=HIGHEST, preferred_element_type=F32)
    s_r = jnp.dot(a_r, (d <= 0).astype(F32), precision=HIGHEST, preferred_element_type=F32)

    e16 = e16_ref[...]
    dt_e = jnp.dot(dt_c, e16, precision=HIGHEST, preferred_element_type=F32)
    s_e = jnp.dot(s_c, e16, precision=HIGHEST, preferred_element_type=F32)
    tot_e = jnp.where(fwd, s_e[SSD_Q - 1:SSD_Q, :], s_e[0:1, :])
    xdt = xs * dt_e
    xdt_end = (xdt * jnp.exp(tot_e - s_e)).astype(BF16)
    xdt16 = xdt.astype(BF16)
    exps_e = jnp.exp(s_e)
    dec_e = jnp.exp(tot_e)
    lane = lax.broadcasted_iota(jnp.int32, (SSD_Q, LANE), 1)

    hp = SSD_INNER // SSD_GROUPS
    heads_per_group = SSD_HEADS // SSD_GROUPS
    for g in range(SSD_GROUPS):
        bg = xc[:, SSD_INNER + g * SSD_STATE:SSD_INNER + (g + 1) * SSD_STATE]
        cg = xc[:, SSD_INNER + n_bc + g * SSD_STATE:SSD_INNER + n_bc + (g + 1) * SSD_STATE]
        bt16 = bg.T.astype(BF16)
        cg16 = cg.astype(BF16)
        cb = _dot(cg16, bt16)
        st = state_ref[:, g * hp:(g + 1) * hp]
        y_off = _dot(cg16, st.astype(BF16)) * exps_e[:, g * hp:(g + 1) * hp]
        state_ref[:, g * hp:(g + 1) * hp] = (dec_e[:, g * hp:(g + 1) * hp] * st
                                             + _dot(bt16, xdt_end[:, g * hp:(g + 1) * hp]))
        for j in range(heads_per_group // 2):
            h0 = g * heads_per_group + 2 * j
            m = []
            for h in (h0, h0 + 1):
                seg = s_c[:, h:h + 1] - s_r[h:h + 1, :]
                m.append(cb * jnp.exp(jnp.where(causal, seg, NEG)))
            m = jnp.concatenate(m, axis=0).astype(BF16)
            lo = h0 * SSD_HEADDIM
            yp = _dot(m, xdt16[:, lo:lo + LANE])
            y_ref[:, lo:lo + LANE] = (jnp.where(lane < SSD_HEADDIM, yp[:SSD_Q], yp[SSD_Q:])
                                      + y_off[:, lo - g * hp:lo - g * hp + LANE])

    @pl.when(jnp.logical_not(fwd))
    def _():
        ybwd_ref[pl.ds(voff, SSD_Q), :] = y_ref[...]

    @pl.when(fwd)
    def _():
        y = y_ref[...] + ybwd_ref[pl.ds(voff, SSD_Q), :] + dskip_ref[...] * xs
        y = y * _silu(z_ref[...])
        for g in range(SSD_GROUPS):
            yg = y[:, g * hp:(g + 1) * hp]
            o_ref[:, g * hp:(g + 1) * hp] = (_rms(yg) * gn_ref[:, g * hp:(g + 1) * hp]).astype(BF16)


def _ssd(xbc, z, dt_col, dt_row, cw, cb, bias_r, bias_c, alog_r, alog_c, dskip_e, gn, e16, *, l):
    n_halo = T_ALL // HALO
    per_blk = SSD_Q // HALO

    def blk(b, p, s):
        return _ssd_row_block(b, p, s)

    def out_blk(b, p, s):
        return _ssd_row_block(b, 1, jnp.where(p == 1, s, 0))

    return pl.pallas_call(
        _ssd_kernel,
        out_shape=jax.ShapeDtypeStruct((T_ALL, SSD_INNER), BF16),
        grid=(BATCH, 2, SSD_STEPS),
        in_specs=[
            pl.BlockSpec((SSD_Q, SSD_CONV_CH), lambda b, p, s: (blk(b, p, s), 0)),
            pl.BlockSpec((HALO, SSD_CONV_CH),
                         lambda b, p, s: (jnp.maximum(blk(b, p, s) * per_blk - 1, 0), 0)),
            pl.BlockSpec((HALO, SSD_CONV_CH),
                         lambda b, p, s: (jnp.minimum((blk(b, p, s) + 1) * per_blk, n_halo - 1), 0)),
            pl.BlockSpec((SSD_Q, SSD_INNER), lambda b, p, s: (out_blk(b, p, s), 0)),
            pl.BlockSpec((None, SSD_Q, SSD_HEADS), lambda b, p, s: (1 - p, blk(b, p, s), 0)),
            pl.BlockSpec((None, SSD_HEADS, SSD_Q), lambda b, p, s: (1 - p, 0, blk(b, p, s))),
            pl.BlockSpec((None, SSD_CONV, SSD_CONV_CH), lambda b, p, s: (l, 0, 0)),
            pl.BlockSpec((None, 1, SSD_CONV_CH), lambda b, p, s: (l, 0, 0)),
            pl.BlockSpec((None, None, 1, SSD_HEADS), lambda b, p, s: (l, 1 - p, 0, 0)),
            pl.BlockSpec((None, None, SSD_HEADS, 1), lambda b, p, s: (l, 1 - p, 0, 0)),
            pl.BlockSpec((None, None, 1, SSD_HEADS), lambda b, p, s: (l, 1 - p, 0, 0)),
            pl.BlockSpec((None, None, SSD_HEADS, 1), lambda b, p, s: (l, 1 - p, 0, 0)),
            pl.BlockSpec((None, 1, SSD_INNER), lambda b, p, s: (l, 0, 0)),
            pl.BlockSpec((None, 1, SSD_INNER), lambda b, p, s: (l, 0, 0)),
            pl.BlockSpec((SSD_HEADS, SSD_INNER), lambda b, p, s: (0, 0)),
        ],
        out_specs=pl.BlockSpec((SSD_Q, SSD_INNER), lambda b, p, s: (out_blk(b, p, s), 0)),
        scratch_shapes=[
            pltpu.VMEM((SSD_Q + 2 * HALO, SSD_CONV_CH), F32),
            pltpu.VMEM((SSD_STATE, SSD_INNER), F32),
            pltpu.VMEM((SSD_STEPS * SSD_Q, SSD_INNER), F32),
            pltpu.VMEM((SSD_Q, SSD_INNER), F32),
        ],
        compiler_params=_cparams(("parallel", "arbitrary", "arbitrary")),
        name="ssd",
    )(xbc, xbc, xbc, z, dt_col, dt_row, cw, cb, bias_r, bias_c, alog_r, alog_c, dskip_e, gn, e16)


def _mla_prep_kernel(cq_ref, ckv_ref, sm_ref, tab_ref, gq_ref, gkv_ref, wq_ref, wk_ref, wv_ref,
                     q_ref, k_ref, v_ref):
    cq_tab = tab_ref[:, 0:LANE]
    sq_tab = tab_ref[:, LANE:2 * LANE]
    ck_tab = tab_ref[:, 2 * LANE:3 * LANE]
    sk_tab = tab_ref[:, 3 * LANE:4 * LANE]
    nq = (_rms(cq_ref[...]) * gq_ref[...]).astype(BF16)
    q = _dot(nq, wq_ref[...])
    for h in range(MLA_HEADS):
        qh = q[:, h * LANE:(h + 1) * LANE]
        qh = qh * cq_tab + pltpu.roll(qh, LANE - MLA_ROPE, 1) * sq_tab
        q_ref[:, h * LANE:(h + 1) * LANE] = qh.astype(BF16)
    nkv = _rms(ckv_ref[...]) * gkv_ref[...]
    sm = sm_ref[...]
    kr = sm * ck_tab + pltpu.roll(sm, LANE - MLA_ROPE, 1) * sk_tab
    lhs = jnp.concatenate([nkv, kr], axis=1).astype(BF16)
    k_ref[...] = _dot(lhs, wk_ref[...]).astype(BF16)
    v_ref[...] = _dot(lhs[:, :MLA_KV_RANK], wv_ref[...]).astype(BF16)


def _mla_prep(cq, ckv, small, tab, gq, gkv, wq, wk, wv, *, l):
    tm = PREP_TM
    tiles_per_seq = SEQ // tm

    def tab_idx(i):
        return jnp.where(i < T_X // tm, i % tiles_per_seq, tiles_per_seq)

    return pl.pallas_call(
        _mla_prep_kernel,
        out_shape=[jax.ShapeDtypeStruct((T_ALL, MLA_HEADS * LANE), BF16),
                   jax.ShapeDtypeStruct((T_ALL, MLA_HEADS * LANE), BF16),
                   jax.ShapeDtypeStruct((T_ALL, MLA_HEADS * MLA_V), BF16)],
        grid=(T_ALL // tm,),
        in_specs=[
            pl.BlockSpec((tm, MLA_Q_RANK), lambda i: (i, 0)),
            pl.BlockSpec((tm, MLA_KV_RANK), lambda i: (i, 0)),
            pl.BlockSpec((tm, LANE), lambda i: (i, 0)),
            pl.BlockSpec((tm, 4 * LANE), lambda i: (tab_idx(i), 0)),
            pl.BlockSpec((None, 1, MLA_Q_RANK), lambda i: (l, 0, 0)),
            pl.BlockSpec((None, 1, MLA_KV_RANK), lambda i: (l, 0, 0)),
            pl.BlockSpec((None, MLA_Q_RANK, MLA_HEADS * LANE), lambda i: (l, 0, 0)),
            pl.BlockSpec((None, MLA_KV_RANK + LANE, MLA_HEADS * LANE), lambda i: (l, 0, 0)),
            pl.BlockSpec((None, MLA_KV_RANK, MLA_HEADS * MLA_V), lambda i: (l, 0, 0)),
        ],
        out_specs=[pl.BlockSpec((tm, MLA_HEADS * LANE), lambda i: (i, 0)),
                   pl.BlockSpec((tm, MLA_HEADS * LANE), lambda i: (i, 0)),
                   pl.BlockSpec((tm, MLA_HEADS * MLA_V), lambda i: (i, 0))],
        compiler_params=_cparams(("parallel",)),
        name="mla_prep",
    )(cq, ckv, small, tab, gq, gkv, wq, wk, wv)


def _mla_attn_kernel(*refs, n_kv):
    refs = refs[(1 if n_kv == 1 else 0):]
    q_ref = refs[0]
    k_refs = refs[1:1 + n_kv]
    v_refs = refs[1 + n_kv:1 + 2 * n_kv]
    o_ref = refs[1 + 2 * n_kv]
    outs = []
    for h in range(2):
        q = q_ref[:, h * LANE:(h + 1) * LANE]
        s = [_dot_t(q, k_ref[:, h * LANE:(h + 1) * LANE]) for k_ref in k_refs]
        m = s[0].max(axis=-1, keepdims=True)
        for si in s[1:]:
            m = jnp.maximum(m, si.max(axis=-1, keepdims=True))
        p = [jnp.exp(si - m) for si in s]
        den = p[0].sum(axis=-1, keepdims=True)
        for pi in p[1:]:
            den = den + pi.sum(axis=-1, keepdims=True)
        o = _dot(p[0].astype(BF16), v_refs[0][...])
        for pi, v_ref in zip(p[1:], v_refs[1:]):
            o = o + _dot(pi.astype(BF16), v_ref[...])
        outs.append(o / den)
    lane = lax.broadcasted_iota(jnp.int32, outs[0].shape, 1)
    o_ref[...] = jnp.where(lane < MLA_V, outs[0], outs[1])


def _mla_attn_latent(q, k, v):
    tq = ATT_TQ
    qt = SEQ // tq
    ctx_blk0 = T_X // CTX_LEN
    return pl.pallas_call(
        functools.partial(_mla_attn_kernel, n_kv=2),
        out_shape=jax.ShapeDtypeStruct((T_ALL, MLA_HEADS * MLA_V), F32),
        grid=(BATCH, MLA_HEADS // 2, qt),
        in_specs=[
            pl.BlockSpec((tq, 2 * LANE), lambda b, h, t: (b * qt + t, h)),
            pl.BlockSpec((SEQ, 2 * LANE), lambda b, h, t: (b, h)),
            pl.BlockSpec((CTX_LEN, 2 * LANE), lambda b, h, t: (ctx_blk0 + b, h)),
            pl.BlockSpec((SEQ, LANE), lambda b, h, t: (b, h)),
            pl.BlockSpec((CTX_LEN, LANE), lambda b, h, t: (ctx_blk0 + b, h)),
        ],
        out_specs=pl.BlockSpec((tq, LANE), lambda b, h, t: (b * qt + t, h)),
        compiler_params=_cparams(("parallel", "parallel", "arbitrary")),
        name="mla_attn_latent",
    )(q, k, k, v, v)


def _mla_attn_ctx(o_prev, q, k, v):
    ctx_blk0 = T_X // CTX_LEN
    return pl.pallas_call(
        functools.partial(_mla_attn_kernel, n_kv=1),
        out_shape=jax.ShapeDtypeStruct((T_ALL, MLA_HEADS * MLA_V), F32),
        grid=(BATCH, MLA_HEADS // 2),
        in_specs=[
            pl.BlockSpec(memory_space=pl.ANY),
            pl.BlockSpec((CTX_LEN, 2 * LANE), lambda b, h: (ctx_blk0 + b, h)),
            pl.BlockSpec((CTX_LEN, 2 * LANE), lambda b, h: (ctx_blk0 + b, h)),
            pl.BlockSpec((CTX_LEN, LANE), lambda b, h: (ctx_blk0 + b, h)),
        ],
        out_specs=pl.BlockSpec((CTX_LEN, LANE), lambda b, h: (ctx_blk0 + b, h)),
        input_output_aliases={0: 0},
        compiler_params=_cparams(("parallel", "parallel")),
        name="mla_attn_ctx",
    )(o_prev, q, k, v)


def _swa_core(q, keys, vals, masks, sinks, o_ref):
    w = q.shape[0]
    lane = lax.broadcasted_iota(jnp.int32, (w, LANE), 1)
    low = lane < SWA_HEADDIM
    rep = SWA_HEADS // SWA_KV_HEADS
    for g in range(SWA_KV_HEADS):
        parts = []
        for j in range(rep // 2):
            qp = q[:, (g * rep // 2 + j) * LANE:(g * rep // 2 + j + 1) * LANE]
            parts.append(jnp.where(low, qp, 0.0))
            parts.append(jnp.where(low, 0.0, qp))
        lhs = jnp.concatenate(parts, axis=0).astype(BF16)
        sink = jnp.concatenate([jnp.full((w, 1), sinks[g * rep + r], F32) for r in range(rep)], axis=0)
        s = []
        for kb, mask in zip(keys, masks):
            sb = _dot_t(lhs, kb[:, g * LANE:(g + 1) * LANE].astype(BF16))
            if mask is not None:
                sb = jnp.where(jnp.concatenate([mask] * rep, axis=0), sb, NEG)
            s.append(sb)
        m = sink
        for sb in s:
            m = jnp.maximum(m, sb.max(axis=-1, keepdims=True))
        den = jnp.exp(sink - m)
        o = None
        for sb, vb in zip(s, vals):
            p = jnp.exp(sb - m)
            den = den + p.sum(axis=-1, keepdims=True)
            pv = _dot(p.astype(BF16), vb[:, g * LANE:(g + 1) * LANE].astype(BF16))
            o = pv if o is None else o + pv
        o = o / den
        for j in range(rep // 2):
            o_ref[:, (g * rep // 2 + j) * LANE:(g * rep // 2 + j + 1) * LANE] = jnp.where(
                low, o[2 * j * w:(2 * j + 1) * w], o[(2 * j + 1) * w:(2 * j + 2) * w])


def _rope_rows(x_ref, tab_ref, n, reps):
    cos = jnp.concatenate([tab_ref[:, :LANE]] * reps, axis=1)
    sin = jnp.concatenate([tab_ref[:, LANE:]] * reps, axis=1)
    return x_ref[:, :n] * cos + x_ref[:, n:] * sin


def _swa_latent_kernel(sink_ref, q_ref, kp_ref, kc_ref, kn_ref, kx_ref, vp_ref, vc_ref, vn_ref,
                       vx_ref, tq_ref, tp_ref, tn_ref, o_ref, *, l):
    n = pl.program_id(1)
    scale = SWA_HEADDIM ** -0.5
    q = _rope_rows(q_ref, tq_ref, N_SWA_Q, N_SWA_Q // LANE) * scale
    keys = [kx_ref[:, :N_SWA_KV2],
            _rope_rows(kp_ref, tp_ref, N_SWA_KV2, N_SWA_KV2 // LANE),
            _rope_rows(kc_ref, tq_ref, N_SWA_KV2, N_SWA_KV2 // LANE),
            _rope_rows(kn_ref, tn_ref, N_SWA_KV2, N_SWA_KV2 // LANE)]
    vals = [vx_ref[...], vp_ref[...], vc_ref[...], vn_ref[...]]
    qi = lax.broadcasted_iota(jnp.int32, (SWA_BLOCK, SWA_BLOCK), 0)
    kj = lax.broadcasted_iota(jnp.int32, (SWA_BLOCK, SWA_BLOCK), 1)
    prev_ok = jnp.logical_and(kj - qi >= SWA_BLOCK - SWA_WINDOW, n > 0)
    next_ok = jnp.logical_and(kj - qi <= SWA_WINDOW - SWA_BLOCK, n < SEQ // SWA_BLOCK - 1)
    sinks = [sink_ref[l, h] for h in range(SWA_HEADS)]
    _swa_core(q, keys, vals, [None, prev_ok, None, next_ok], sinks, o_ref)


def _swa_ctx_kernel(sink_ref, carrier_ref, q_ref, kx_ref, vx_ref, o_ref, *, l):
    del carrier_ref
    q = q_ref[:, :N_SWA_Q] * (SWA_HEADDIM ** -0.5)
    sinks = [sink_ref[l, h] for h in range(SWA_HEADS)]
    _swa_core(q, [kx_ref[:, :N_SWA_KV2]], [vx_ref[...]], [None], sinks, o_ref)


def _swa_latent(sink, qs2, ks2, vs2, tab, *, l):
    nb = SEQ // SWA_BLOCK
    ctx_blk0 = T_X // CTX_LEN
    w = SWA_BLOCK

    def prev(b, n):
        return b * nb + jnp.maximum(n - 1, 0)

    def nxt(b, n):
        return b * nb + jnp.minimum(n + 1, nb - 1)

    return pl.pallas_call(
        functools.partial(_swa_latent_kernel, l=l),
        out_shape=jax.ShapeDtypeStruct((T_ALL, N_SWA_Q), F32),
        grid=(BATCH, nb),
        in_specs=[
            pl.BlockSpec(memory_space=pltpu.SMEM),
            pl.BlockSpec((w, 2 * N_SWA_Q), lambda b, n: (b * nb + n, 0)),
            pl.BlockSpec((w, 2 * N_SWA_KV2), lambda b, n: (prev(b, n), 0)),
            pl.BlockSpec((w, 2 * N_SWA_KV2), lambda b, n: (b * nb + n, 0)),
            pl.BlockSpec((w, 2 * N_SWA_KV2), lambda b, n: (nxt(b, n), 0)),
            pl.BlockSpec((CTX_LEN, 2 * N_SWA_KV2), lambda b, n: (ctx_blk0 + b, 0)),
            pl.BlockSpec((w, N_SWA_KV2), lambda b, n: (prev(b, n), 0)),
            pl.BlockSpec((w, N_SWA_KV2), lambda b, n: (b * nb + n, 0)),
            pl.BlockSpec((w, N_SWA_KV2), lambda b, n: (nxt(b, n), 0)),
            pl.BlockSpec((CTX_LEN, N_SWA_KV2), lambda b, n: (ctx_blk0 + b, 0)),
            pl.BlockSpec((w, 2 * LANE), lambda b, n: (n, 0)),
            pl.BlockSpec((w, 2 * LANE), lambda b, n: (jnp.maximum(n - 1, 0), 0)),
            pl.BlockSpec((w, 2 * LANE), lambda b, n: (jnp.minimum(n + 1, nb - 1), 0)),
        ],
        out_specs=pl.BlockSpec((w, N_SWA_Q), lambda b, n: (b * nb + n, 0)),
        compiler_params=_cparams(("parallel", "arbitrary")),
        name="swa_latent",
    )(sink, qs2, ks2, ks2, ks2, ks2, vs2, vs2, vs2, vs2, tab, tab, tab)


def _swa_ctx(sink, o_prev, qs2, ks2, vs2, *, l):
    w = SWA_BLOCK
    blk0 = T_X // w
    per_b = CTX_LEN // w
    ctx_blk0 = T_X // CTX_LEN
    return pl.pallas_call(
        functools.partial(_swa_ctx_kernel, l=l),
        out_shape=jax.ShapeDtypeStruct((T_ALL, N_SWA_Q), F32),
        grid=(BATCH, per_b),
        in_specs=[
            pl.BlockSpec(memory_space=pltpu.SMEM),
            pl.BlockSpec(memory_space=pl.ANY),
            pl.BlockSpec((w, 2 * N_SWA_Q), lambda b, n: (blk0 + b * per_b + n, 0)),
            pl.BlockSpec((CTX_LEN, 2 * N_SWA_KV2), lambda b, n: (ctx_blk0 + b, 0)),
            pl.BlockSpec((CTX_LEN, N_SWA_KV2), lambda b, n: (ctx_blk0 + b, 0)),
        ],
        out_specs=pl.BlockSpec((w, N_SWA_Q), lambda b, n: (blk0 + b * per_b + n, 0)),
        input_output_aliases={1: 0},
        compiler_params=_cparams(("parallel", "arbitrary")),
        name="swa_ctx",
    )(sink, o_prev, qs2, ks2, vs2)


def _outproj_kernel(h_ref, mod_ref, ssd_ref, mla_ref, swa_ref, gm_ref, gs_ref, w_ref, o_ref):
    n_mla = MLA_HEADS * MLA_V
    acc = _dot(ssd_ref[...], w_ref[0:SSD_INNER, :])
    mla = (_rms(mla_ref[...]) * gm_ref[...]).astype(BF16)
    acc += _dot(mla, w_ref[SSD_INNER:SSD_INNER + n_mla, :])
    swa = (_rms(swa_ref[...]) * gs_ref[...]).astype(BF16)
    acc += _dot(swa, w_ref[SSD_INNER + n_mla:, :])
    o_ref[...] = h_ref[...] + mod_ref[5:6, :] * acc


def _outproj(h, mod_l, ssd, mla, swa, gm, gs, w_out, *, l, rows):
    tm = OUT_TM
    n_mla = MLA_HEADS * MLA_V
    return pl.pallas_call(
        _outproj_kernel,
        out_shape=jax.ShapeDtypeStruct((rows, D_MODEL), F32),
        grid=(rows // tm,),
        in_specs=[
            pl.BlockSpec((tm, D_MODEL), lambda i: (i, 0)),
            pl.BlockSpec((None, N_MOD, D_MODEL), lambda i: (_group_of_tile(i, tm), 0, 0)),
            pl.BlockSpec((tm, SSD_INNER), lambda i: (i, 0)),
            pl.BlockSpec((tm, n_mla), lambda i: (i, 0)),
            pl.BlockSpec((tm, N_SWA_Q), lambda i: (i, 0)),
            pl.BlockSpec((None, 1, n_mla), lambda i: (l, 0, 0)),
            pl.BlockSpec((None, 1, N_SWA_Q), lambda i: (l, 0, 0)),
            pl.BlockSpec((None, D_MIX, D_MODEL), lambda i: (l, 0, 0)),
        ],
        out_specs=pl.BlockSpec((tm, D_MODEL), lambda i: (i, 0)),
        compiler_params=_cparams(("parallel",)),
        name="outproj",
    )(h, mod_l, ssd, mla, swa, gm, gs, w_out)


def _rot_src_sign(dim):
    q = dim // 4
    src = np.concatenate([np.arange(q, 2 * q), np.arange(0, q),
                          np.arange(3 * q, 4 * q), np.arange(2 * q, 3 * q)])
    sign = np.concatenate([-np.ones(q), np.ones(q), -np.ones(q), np.ones(q)])
    return src, sign


def _w_in_layout():
    offs = np.concatenate([[0], np.cumsum(IN_SIZES)])
    o_z, o_xbc, o_dt, o_cq, o_ckv, o_kr, o_qs, o_ks, o_vs = offs[:-1]
    idx, sgn = [], []
    ident = (np.arange(SWA_HEADDIM), np.ones(SWA_HEADDIM))

    def plain(start, n):
        idx.append(np.arange(start, start + n))
        sgn.append(np.ones(n))

    def heads(start, n_heads, dim, src_sign, copies):
        src, sign = src_sign
        for h in range(n_heads):
            for _ in range(copies):
                idx.append(start + h * dim + src)
                sgn.append(sign)

    plain(o_z, SSD_INNER)
    plain(o_xbc, SSD_CONV_CH)
    plain(o_cq, MLA_Q_RANK)
    plain(o_ckv, MLA_KV_RANK)
    plain(o_qs, N_SWA_Q)
    heads(o_qs, SWA_HEADS, SWA_HEADDIM, _rot_src_sign(SWA_HEADDIM), 1)
    heads(o_ks, SWA_KV_HEADS, SWA_HEADDIM, ident, 2)
    heads(o_ks, SWA_KV_HEADS, SWA_HEADDIM, _rot_src_sign(SWA_HEADDIM), 2)
    heads(o_vs, SWA_KV_HEADS, SWA_HEADDIM, ident, 2)
    plain(o_dt, 2 * SSD_HEADS)
    plain(o_kr, MLA_ROPE)
    heads(o_kr, 1, MLA_ROPE, _rot_src_sign(MLA_ROPE), 1)
    idx.append(np.zeros(MLA_ROPE, np.int64))
    sgn.append(np.zeros(MLA_ROPE))
    idx = np.concatenate(idx).astype(np.int32)
    sgn = np.concatenate(sgn).astype(np.float32)
    assert idx.shape[0] == W_IN_COLS
    return idx, sgn


def _w_uq_layout():
    hd = MLA_NOPE + MLA_ROPE
    src, sign = _rot_src_sign(MLA_ROPE)
    idx, sgn = [], []
    for h in range(MLA_HEADS):
        idx.append(h * hd + np.arange(hd))
        sgn.append(np.ones(hd))
        idx.append(h * hd + MLA_NOPE + src)
        sgn.append(sign)
    return np.concatenate(idx).astype(np.int32), np.concatenate(sgn).astype(np.float32)


def _w_ukv_layout():
    hd = MLA_NOPE + MLA_V
    kidx, ksgn, vidx = [], [], []
    for h in range(MLA_HEADS):
        kidx.append(h * hd + np.arange(MLA_NOPE))
        ksgn.append(np.ones(MLA_NOPE))
        kidx.append(np.zeros(LANE - MLA_NOPE, np.int64))
        ksgn.append(np.zeros(LANE - MLA_NOPE))
        vidx.append(h * hd + MLA_NOPE + np.arange(MLA_V))
    return (np.concatenate(kidx).astype(np.int32), np.concatenate(ksgn).astype(np.float32),
            np.concatenate(vidx).astype(np.int32))


def _k_rope_expand():
    e = np.zeros((LANE, MLA_HEADS * LANE), np.float32)
    for h in range(MLA_HEADS):
        for i in range(MLA_ROPE):
            e[MLA_ROPE + i, h * LANE + MLA_NOPE + i] = 1.0
    return e


def _head_expand():
    e = np.zeros((SSD_HEADS, SSD_INNER), np.float32)
    for h in range(SSD_HEADS):
        e[h, h * SSD_HEADDIM:(h + 1) * SSD_HEADDIM] = 1.0
    return e


def _rope_angles(dim):
    pos = np.arange(SEQ)
    quarter = dim // 4
    inv_freq = ROPE_THETA ** (-jnp.arange(quarter, dtype=F32) / quarter)
    ang_r = jnp.asarray(pos // GRID_W, F32)[:, None] * inv_freq[None, :]
    ang_c = jnp.asarray(pos % GRID_W, F32)[:, None] * inv_freq[None, :]
    ang = jnp.concatenate([ang_r, ang_r, ang_c, ang_c], axis=-1)
    return jnp.cos(ang), jnp.sin(ang)


def _mla_rope_table():
    cos, sin = _rope_angles(MLA_ROPE)
    scale = (MLA_NOPE + MLA_ROPE) ** -0.5
    n = SEQ + PREP_TM
    ident = jnp.ones((PREP_TM, MLA_ROPE), F32)
    zeros = jnp.zeros((PREP_TM, MLA_ROPE), F32)
    cos = jnp.concatenate([cos, ident], axis=0)
    sin = jnp.concatenate([sin, zeros], axis=0)
    z32 = jnp.zeros((n, MLA_ROPE), F32)
    cq = jnp.concatenate([jnp.ones((n, MLA_NOPE), F32), cos, z32], axis=1) * scale
    sq = jnp.concatenate([jnp.zeros((n, MLA_NOPE), F32), sin, z32], axis=1) * scale
    ck = jnp.concatenate([z32, cos, z32, z32], axis=1)
    sk = jnp.concatenate([z32, sin, z32, z32], axis=1)
    return jnp.concatenate([cq, sq, ck, sk], axis=1)


def _swa_rope_table():
    cos, sin = _rope_angles(SWA_HEADDIM)
    return jnp.concatenate([cos, cos, sin, sin], axis=1)


def kernel(x, c, ctx, c_ctx, w_mod, b_mod, norm_g, ffn_w_gate, ffn_w_up, ffn_w_down,
           w_in, w_out, ssd_conv_w, ssd_conv_b, ssd_dt_bias, ssd_a_log, ssd_d, ssd_norm,
           mla_q_norm, mla_w_uq, mla_kv_norm, mla_w_ukv, mla_out_norm, swa_sink,
           swa_out_norm, final_norm):
    idx, sgn = _w_in_layout()
    w_in_p = (w_in[:, :, idx] * sgn).astype(BF16)
    qidx, qsgn = _w_uq_layout()
    wq = (mla_w_uq[:, :, qidx] * qsgn).astype(BF16)
    kidx, ksgn, vidx = _w_ukv_layout()
    k_expand = jnp.broadcast_to(jnp.asarray(_k_rope_expand()), (DEPTH, LANE, MLA_HEADS * LANE))
    wk = jnp.concatenate([mla_w_ukv[:, :, kidx] * ksgn, k_expand], axis=1).astype(BF16)
    wv = mla_w_ukv[:, :, vidx].astype(BF16)
    wg = ffn_w_gate.astype(BF16)
    wu = ffn_w_up.astype(BF16)
    wd = ffn_w_down.astype(BF16)
    wo = w_out.astype(BF16)
    mla_tab = _mla_rope_table()
    swa_tab = _swa_rope_table()
    e16 = jnp.asarray(_head_expand())
    dskip_e = jnp.repeat(ssd_d, SSD_HEADDIM, axis=-1).reshape(DEPTH, 1, SSD_INNER)
    ssd_gn = ssd_norm.reshape(DEPTH, 1, SSD_INNER)
    conv_b = ssd_conv_b.reshape(DEPTH, 1, SSD_CONV_CH)
    bias_r = ssd_dt_bias.reshape(DEPTH, 2, 1, SSD_HEADS)
    bias_c = ssd_dt_bias.reshape(DEPTH, 2, SSD_HEADS, 1)
    alog_r = ssd_a_log.reshape(DEPTH, 2, 1, SSD_HEADS)
    alog_c = ssd_a_log.reshape(DEPTH, 2, SSD_HEADS, 1)
    gq = mla_q_norm.reshape(DEPTH, 1, MLA_Q_RANK)
    gkv = mla_kv_norm.reshape(DEPTH, 1, MLA_KV_RANK)
    gm = mla_out_norm.reshape(DEPTH, 1, -1)
    gs = swa_out_norm.reshape(DEPTH, 1, -1)
    fin = final_norm.reshape(1, D_MODEL)

    c_all = jnp.concatenate([c, c_ctx[None, :], jnp.zeros((GROUP_PAD - N_GROUPS, D_MODEL), F32)], axis=0)
    mod_all = _modulation(c_all, w_mod, b_mod).reshape(DEPTH, GROUP_PAD, N_MOD, D_MODEL)

    h = jnp.concatenate([x.reshape(T_X, D_MODEL), ctx.reshape(T_C, D_MODEL)], axis=0)

    for l in range(DEPTH):
        last = l == DEPTH - 1
        mod_l = mod_all[l]
        h = _ffn(h, mod_l, norm_g[l, 0].reshape(1, D_MODEL), wg, wu, wd, fin,
                 l=l, k=0, base=0, rows=T_ALL)
        z, xbc, cq, ckv, qs2, ks2, vs2, small = _inproj(h, mod_l, norm_g[l, 1].reshape(1, D_MODEL),
                                                       w_in_p, l=l)
        dt2 = small[:, :2 * SSD_HEADS].reshape(T_ALL, 2, SSD_HEADS)
        dt_col = jnp.transpose(dt2, (1, 0, 2))
        dt_row = jnp.transpose(dt2, (1, 2, 0))
        ssd_o = _ssd(xbc, z, dt_col, dt_row, ssd_conv_w, conv_b, bias_r, bias_c, alog_r, alog_c,
                     dskip_e, ssd_gn, e16, l=l)
        q_m, k_m, v_m = _mla_prep(cq, ckv, small, mla_tab, gq, gkv, wq, wk, wv, l=l)
        mla_o = _mla_attn_latent(q_m, k_m, v_m)
        swa_o = _swa_latent(swa_sink, qs2, ks2, vs2, swa_tab, l=l)
        if not last:
            mla_o = _mla_attn_ctx(mla_o, q_m, k_m, v_m)
            swa_o = _swa_ctx(swa_sink, swa_o, qs2, ks2, vs2, l=l)
        rows = T_X if last else T_ALL
        h = _outproj(h, mod_l, ssd_o, mla_o, swa_o, gm, gs, wo, l=l, rows=rows)
        h = _ffn(h, mod_l, norm_g[l, 2].reshape(1, D_MODEL), wg, wu, wd, fin,
                 l=l, k=1, base=6, rows=rows, final_norm=last)
    return h.reshape(BATCH, SEQ, D_MODEL)
```

```python
import functools
import math

import jax
import jax.numpy as jnp
import numpy as np
from jax import lax
from jax.experimental import pallas as pl
from jax.experimental.pallas import tpu as pltpu

F32 = jnp.float32
BF16 = jnp.bfloat16
HIGHEST = lax.Precision.HIGHEST

D_MODEL = 2048
BATCH = 4
SEQ = 4096
DEPTH = 4
GRID_W = 64
CTX_LEN = 256
EPS = 1e-6
ROPE_THETA = 10000.0
N_MOD = 9
D_FF = 5632

SSD_HEADS = 16
SSD_HEADDIM = 64
SSD_INNER = SSD_HEADS * SSD_HEADDIM
SSD_GROUPS = 2
SSD_STATE = 128
SSD_CONV = 5
SSD_CONV_CH = SSD_INNER + 2 * SSD_GROUPS * SSD_STATE

MLA_HEADS = 8
MLA_Q_RANK = 384
MLA_KV_RANK = 256
MLA_NOPE = 64
MLA_ROPE = 32
MLA_V = 64

SWA_HEADS = 8
SWA_KV_HEADS = 2
SWA_HEADDIM = 64
SWA_WINDOW = 128
SWA_BLOCK = 128

D_MIX = SSD_INNER + MLA_HEADS * MLA_V + SWA_HEADS * SWA_HEADDIM
IN_SIZES = (SSD_INNER, SSD_CONV_CH, 2 * SSD_HEADS, MLA_Q_RANK, MLA_KV_RANK, MLA_ROPE,
            SWA_HEADS * SWA_HEADDIM, SWA_KV_HEADS * SWA_HEADDIM, SWA_KV_HEADS * SWA_HEADDIM)

T_X = BATCH * SEQ
T_C = BATCH * CTX_LEN
T_ALL = T_X + T_C
N_GROUPS = BATCH + 1
GROUP_PAD = 8

LANE = 128
SUBLANE = 8
VMEM_LIMIT = 56 * 1024 * 1024
NEG = -1e30

FFN_TM = 512
FFN_TF = 512
PROJ_TM = 256
OUT_TM = 512
PREP_TM = 512
ATT_TQ = 512
ATT_TK = 1024
MOD_TN = 1024

SSD_Q = 128
SEQ_CHUNKS = SEQ // SSD_Q
CTX_CHUNKS = CTX_LEN // SSD_Q
SSD_STEPS = SEQ_CHUNKS + CTX_CHUNKS
HALO = SUBLANE
N_SWA_Q = SWA_HEADS * SWA_HEADDIM
N_SWA_KV2 = 2 * SWA_KV_HEADS * SWA_HEADDIM

SEG_Z = (0, 1024)
SEG_XBC = (1024, 2560)
SEG_CQ = (2560, 2944)
SEG_CKV = (2944, 3200)
SEG_QS = (3200, 4224)
SEG_KS = (4224, 4736)
SEG_VS = (4736, 4992)
SEG_SMALL = (4992, 5120)
W_IN_COLS = 5120
IN_SEGS = (SEG_Z, SEG_XBC, SEG_CQ, SEG_CKV, SEG_QS, SEG_KS, SEG_VS, SEG_SMALL)


def _cparams(sem):
    return pltpu.CompilerParams(dimension_semantics=sem, vmem_limit_bytes=VMEM_LIMIT)


def _group_of_tile(i, tm):
    return jnp.minimum((i * tm) // SEQ, BATCH)


def _rms(x):
    return x * lax.rsqrt(jnp.mean(x * x, axis=-1, keepdims=True) + EPS)


def _silu(x):
    return x * jax.nn.sigmoid(x)


def _softplus(x):
    return jnp.maximum(x, 0.0) + jnp.log1p(jnp.exp(-jnp.abs(x)))


def _dot(a, b):
    return jnp.dot(a, b, preferred_element_type=F32)


def _dot_t(a, b):
    return lax.dot_general(a, b, (((1,), (1,)), ((), ())), preferred_element_type=F32)


def _mod_kernel(c_ref, w_ref, b_ref, o_ref):
    sc = _silu(c_ref[...]).astype(BF16)
    o_ref[...] = _dot(sc, w_ref[...].astype(BF16)) + b_ref[...]


def _modulation(c_all, w_mod, b_mod):
    n = N_MOD * D_MODEL
    return pl.pallas_call(
        _mod_kernel,
        out_shape=jax.ShapeDtypeStruct((DEPTH, GROUP_PAD, n), F32),
        grid=(DEPTH, n // MOD_TN),
        in_specs=[
            pl.BlockSpec((GROUP_PAD, D_MODEL), lambda l, j: (0, 0)),
            pl.BlockSpec((None, D_MODEL, MOD_TN), lambda l, j: (l, 0, j)),
            pl.BlockSpec((None, 1, MOD_TN), lambda l, j: (l, 0, j)),
        ],
        out_specs=pl.BlockSpec((None, GROUP_PAD, MOD_TN), lambda l, j: (l, 0, j)),
        compiler_params=_cparams(("parallel", "parallel")),
        name="modulation",
    )(c_all, w_mod, b_mod.reshape(DEPTH, 1, n))


def _ffn_kernel(*refs, split, base, n_ff, n_lat, final_norm):
    x_ref, xc_ref = (refs[0], refs[1]) if split else (refs[0], None)
    mod_ref, g_ref, wg_ref, wu_ref, wd_ref, fin_ref, o_ref, u_ref = refs[(2 if split else 1):]
    j = pl.program_id(1)

    def residual():
        if xc_ref is None:
            return x_ref[...]
        return jnp.where(pl.program_id(0) < n_lat, x_ref[...], xc_ref[...])

    @pl.when(j == 0)
    def _():
        y = _rms(residual()) * g_ref[...]
        u = y * (1.0 + mod_ref[base + 1:base + 2, :]) + mod_ref[base:base + 1, :]
        u_ref[...] = u.astype(BF16)

    u = u_ref[...]
    gate = _dot(u, wg_ref[...])
    up = _dot(u, wu_ref[...])
    a = (_silu(gate) * up).astype(BF16)
    part = _dot(a, wd_ref[...])

    @pl.when(j == 0)
    def _():
        o_ref[...] = part

    @pl.when(j > 0)
    def _():
        o_ref[...] += part

    @pl.when(j == n_ff - 1)
    def _():
        h = residual() + (0.5 * mod_ref[base + 2:base + 3, :]) * o_ref[...]
        if final_norm:
            h = _rms(h) * fin_ref[...]
        o_ref[...] = h


def _ffn(h, h_ctx, mod_l, g, wg, wu, wd, fin, *, l, k, base, rows, final_norm=False):
    n_ff = D_FF // FFN_TF
    tm = FFN_TM
    n_lat = T_X // tm
    split = h_ctx is not None
    kern = functools.partial(_ffn_kernel, split=split, base=base, n_ff=n_ff, n_lat=n_lat,
                             final_norm=final_norm)
    if split:
        x_specs = [pl.BlockSpec((tm, D_MODEL), lambda i, j: (jnp.minimum(i, n_lat - 1), 0)),
                   pl.BlockSpec((tm, D_MODEL), lambda i, j: (jnp.maximum(i - n_lat, 0), 0))]
        x_args = (h, h_ctx)
    else:
        x_specs = [pl.BlockSpec((tm, D_MODEL), lambda i, j: (i, 0))]
        x_args = (h,)
    return pl.pallas_call(
        kern,
        out_shape=jax.ShapeDtypeStruct((rows, D_MODEL), F32),
        grid=(rows // tm, n_ff),
        in_specs=x_specs + [
            pl.BlockSpec((None, N_MOD, D_MODEL), lambda i, j: (_group_of_tile(i, tm), 0, 0)),
            pl.BlockSpec((1, D_MODEL), lambda i, j: (0, 0)),
            pl.BlockSpec((None, None, D_MODEL, FFN_TF), lambda i, j: (l, k, 0, j)),
            pl.BlockSpec((None, None, D_MODEL, FFN_TF), lambda i, j: (l, k, 0, j)),
            pl.BlockSpec((None, None, FFN_TF, D_MODEL), lambda i, j: (l, k, j, 0)),
            pl.BlockSpec((1, D_MODEL), lambda i, j: (0, 0)),
        ],
        out_specs=pl.BlockSpec((tm, D_MODEL), lambda i, j: (i, 0)),
        scratch_shapes=[pltpu.VMEM((tm, D_MODEL), BF16)],
        compiler_params=_cparams(("parallel", "arbitrary")),
        name="ffn",
    )(*x_args, mod_l, g, wg, wu, wd, fin)


def _inproj_kernel(x_ref, mod_ref, g_ref, w_ref, *o_refs):
    y = _rms(x_ref[...]) * g_ref[...]
    u = (y * (1.0 + mod_ref[4:5, :]) + mod_ref[3:4, :]).astype(BF16)
    for (a, b), o_ref in zip(IN_SEGS, o_refs):
        o_ref[...] = _dot(u, w_ref[:, a:b])


def _inproj(h, mod_l, g, w_in_p, *, l):
    tm = PROJ_TM
    return pl.pallas_call(
        _inproj_kernel,
        out_shape=[jax.ShapeDtypeStruct((T_ALL, b - a), F32) for a, b in IN_SEGS],
        grid=(T_ALL // tm,),
        in_specs=[
            pl.BlockSpec((tm, D_MODEL), lambda i: (i, 0)),
            pl.BlockSpec((None, N_MOD, D_MODEL), lambda i: (_group_of_tile(i, tm), 0, 0)),
            pl.BlockSpec((1, D_MODEL), lambda i: (0, 0)),
            pl.BlockSpec((None, D_MODEL, W_IN_COLS), lambda i: (l, 0, 0),
                         pipeline_mode=pl.Buffered(1)),
        ],
        out_specs=[pl.BlockSpec((tm, b - a), lambda i: (i, 0)) for a, b in IN_SEGS],
        compiler_params=_cparams(("parallel",)),
        name="inproj",
    )(h, mod_l, g, w_in_p)


def _ssd_chunk_ids(phase, s):
    fwd = phase == 1
    is_ctx = s < CTX_CHUNKS
    cc = jnp.where(fwd, s, CTX_CHUNKS - 1 - s)
    lc = jnp.where(fwd, s - CTX_CHUNKS, SSD_STEPS - 1 - s)
    return is_ctx, cc, lc


def _ssd_row_block(b, phase, s):
    is_ctx, cc, lc = _ssd_chunk_ids(phase, s)
    return jnp.where(is_ctx, T_X // SSD_Q + b * CTX_CHUNKS + cc, b * SEQ_CHUNKS + lc)


def _ssd_kernel(xbc_ref, prev_ref, next_ref, z_ref, dtc_ref, dtr_ref, cw_ref, cb_ref,
                bias_r_ref, bias_c_ref, alog_r_ref, alog_c_ref, dskip_ref, gn_ref, e16_ref,
                o_ref, xe_ref, state_ref, ybwd_ref, y_ref):
    phase = pl.program_id(1)
    s = pl.program_id(2)
    fwd = phase == 1
    is_ctx, cc, lc = _ssd_chunk_ids(phase, s)
    first = jnp.where(is_ctx, cc == 0, lc == 0)
    last = jnp.where(is_ctx, cc == CTX_CHUNKS - 1, lc == SEQ_CHUNKS - 1)
    voff = pl.multiple_of(jnp.where(is_ctx, cc, CTX_CHUNKS + lc) * SSD_Q, SSD_Q)

    @pl.when(s == 0)
    def _():
        state_ref[...] = jnp.zeros_like(state_ref)

    xe_ref[0:HALO, :] = jnp.where(first, 0.0, prev_ref[...])
    xe_ref[HALO:HALO + SSD_Q, :] = xbc_ref[...]
    xe_ref[HALO + SSD_Q:, :] = jnp.where(last, 0.0, next_ref[...])
    acc = cb_ref[...] + cw_ref[0:1, :] * xe_ref[HALO - 2:HALO - 2 + SSD_Q, :]
    for k in range(1, SSD_CONV):
        acc = acc + cw_ref[k:k + 1, :] * xe_ref[HALO - 2 + k:HALO - 2 + k + SSD_Q, :]
    xc = _silu(acc)
    xs = xc[:, :SSD_INNER]
    n_bc = SSD_GROUPS * SSD_STATE

    dt_c = _softplus(dtc_ref[...] + bias_r_ref[...])
    dt_r = _softplus(dtr_ref[...] + bias_c_ref[...])
    a_c = dt_c * -jnp.exp(alog_r_ref[...])
    a_r = dt_r * -jnp.exp(alog_c_ref[...])
    row = lax.broadcasted_iota(jnp.int32, (SSD_Q, SSD_Q), 0)
    col = lax.broadcasted_iota(jnp.int32, (SSD_Q, SSD_Q), 1)
    d = (row - col) * jnp.where(fwd, 1, -1)
    causal = d >= 0
    s_c = jnp.dot(causal.astype(F32), a_c, precision=HIGHEST, preferred_element_type=F32)
    s_r = jnp.dot(a_r, (d <= 0).astype(F32), precision=HIGHEST, preferred_element_type=F32)

    e16 = e16_ref[...]
    dt_e = jnp.dot(dt_c, e16, precision=HIGHEST, preferred_element_type=F32)
    s_e = jnp.dot(s_c, e16, precision=HIGHEST, preferred_element_type=F32)
    tot_e = jnp.where(fwd, s_e[SSD_Q - 1:SSD_Q, :], s_e[0:1, :])
    xdt = xs * dt_e
    xdt_end = (xdt * jnp.exp(tot_e - s_e)).astype(BF16)
    xdt16 = xdt.astype(BF16)
    exps_e = jnp.exp(s_e)
    dec_e = jnp.exp(tot_e)
    lane = lax.broadcasted_iota(jnp.int32, (SSD_Q, LANE), 1)

    hp = SSD_INNER // SSD_GROUPS
    heads_per_group = SSD_HEADS // SSD_GROUPS
    for g in range(SSD_GROUPS):
        bg = xc[:, SSD_INNER + g * SSD_STATE:SSD_INNER + (g + 1) * SSD_STATE]
        cg = xc[:, SSD_INNER + n_bc + g * SSD_STATE:SSD_INNER + n_bc + (g + 1) * SSD_STATE]
        bt16 = bg.T.astype(BF16)
        cg16 = cg.astype(BF16)
        cb = _dot(cg16, bt16)
        st = state_ref[:, g * hp:(g + 1) * hp]
        y_off = _dot(cg16, st.astype(BF16)) * exps_e[:, g * hp:(g + 1) * hp]
        state_ref[:, g * hp:(g + 1) * hp] = (dec_e[:, g * hp:(g + 1) * hp] * st
                                             + _dot(bt16, xdt_end[:, g * hp:(g + 1) * hp]))
        for j in range(heads_per_group // 2):
            h0 = g * heads_per_group + 2 * j
            m = []
            for h in (h0, h0 + 1):
                seg = s_c[:, h:h + 1] - s_r[h:h + 1, :]
                m.append(cb * jnp.exp(jnp.where(causal, seg, NEG)))
            m = jnp.concatenate(m, axis=0).astype(BF16)
            lo = h0 * SSD_HEADDIM
            yp = _dot(m, xdt16[:, lo:lo + LANE])
            y_ref[:, lo:lo + LANE] = (jnp.where(lane < SSD_HEADDIM, yp[:SSD_Q], yp[SSD_Q:])
                                      + y_off[:, lo - g * hp:lo - g * hp + LANE])

    @pl.when(jnp.logical_not(fwd))
    def _():
        ybwd_ref[pl.ds(voff, SSD_Q), :] = y_ref[...]

    @pl.when(fwd)
    def _():
        y = y_ref[...] + ybwd_ref[pl.ds(voff, SSD_Q), :] + dskip_ref[...] * xs
        y = y * _silu(z_ref[...])
        for g in range(SSD_GROUPS):
            yg = y[:, g * hp:(g + 1) * hp]
            o_ref[:, g * hp:(g + 1) * hp] = (_rms(yg) * gn_ref[:, g * hp:(g + 1) * hp]).astype(BF16)


def _ssd(xbc, z, dt_col, dt_row, cw, cb, bias_r, bias_c, alog_r, alog_c, dskip_e, gn, e16, *, l):
    n_halo = T_ALL // HALO
    per_blk = SSD_Q // HALO

    def blk(b, p, s):
        return _ssd_row_block(b, p, s)

    def out_blk(b, p, s):
        return _ssd_row_block(b, 1, jnp.where(p == 1, s, 0))

    return pl.pallas_call(
        _ssd_kernel,
        out_shape=jax.ShapeDtypeStruct((T_ALL, SSD_INNER), BF16),
        grid=(BATCH, 2, SSD_STEPS),
        in_specs=[
            pl.BlockSpec((SSD_Q, SSD_CONV_CH), lambda b, p, s: (blk(b, p, s), 0)),
            pl.BlockSpec((HALO, SSD_CONV_CH),
                         lambda b, p, s: (jnp.maximum(blk(b, p, s) * per_blk - 1, 0), 0)),
            pl.BlockSpec((HALO, SSD_CONV_CH),
                         lambda b, p, s: (jnp.minimum((blk(b, p, s) + 1) * per_blk, n_halo - 1), 0)),
            pl.BlockSpec((SSD_Q, SSD_INNER), lambda b, p, s: (out_blk(b, p, s), 0)),
            pl.BlockSpec((None, SSD_Q, SSD_HEADS), lambda b, p, s: (1 - p, blk(b, p, s), 0)),
            pl.BlockSpec((None, SSD_HEADS, SSD_Q), lambda b, p, s: (1 - p, 0, blk(b, p, s))),
            pl.BlockSpec((None, SSD_CONV, SSD_CONV_CH), lambda b, p, s: (l, 0, 0)),
            pl.BlockSpec((None, 1, SSD_CONV_CH), lambda b, p, s: (l, 0, 0)),
            pl.BlockSpec((None, None, 1, SSD_HEADS), lambda b, p, s: (l, 1 - p, 0, 0)),
            pl.BlockSpec((None, None, SSD_HEADS, 1), lambda b, p, s: (l, 1 - p, 0, 0)),
            pl.BlockSpec((None, None, 1, SSD_HEADS), lambda b, p, s: (l, 1 - p, 0, 0)),
            pl.BlockSpec((None, None, SSD_HEADS, 1), lambda b, p, s: (l, 1 - p, 0, 0)),
            pl.BlockSpec((None, 1, SSD_INNER), lambda b, p, s: (l, 0, 0)),
            pl.BlockSpec((None, 1, SSD_INNER), lambda b, p, s: (l, 0, 0)),
            pl.BlockSpec((SSD_HEADS, SSD_INNER), lambda b, p, s: (0, 0)),
        ],
        out_specs=pl.BlockSpec((SSD_Q, SSD_INNER), lambda b, p, s: (out_blk(b, p, s), 0)),
        scratch_shapes=[
            pltpu.VMEM((SSD_Q + 2 * HALO, SSD_CONV_CH), F32),
            pltpu.VMEM((SSD_STATE, SSD_INNER), F32),
            pltpu.VMEM((SSD_STEPS * SSD_Q, SSD_INNER), F32),
            pltpu.VMEM((SSD_Q, SSD_INNER), F32),
        ],
        compiler_params=_cparams(("parallel", "arbitrary", "arbitrary")),
        name="ssd",
    )(xbc, xbc, xbc, z, dt_col, dt_row, cw, cb, bias_r, bias_c, alog_r, alog_c, dskip_e, gn, e16)


def _mla_prep_kernel(cq_ref, ckv_ref, sm_ref, tab_ref, gq_ref, gkv_ref, wq_ref, wk_ref, wv_ref,
                     q_ref, k_ref, v_ref):
    cq_tab = tab_ref[:, 0:LANE]
    sq_tab = tab_ref[:, LANE:2 * LANE]
    ck_tab = tab_ref[:, 2 * LANE:3 * LANE]
    sk_tab = tab_ref[:, 3 * LANE:4 * LANE]
    nq = (_rms(cq_ref[...]) * gq_ref[...]).astype(BF16)
    q = _dot(nq, wq_ref[...])
    for h in range(MLA_HEADS):
        qh = q[:, h * LANE:(h + 1) * LANE]
        qh = qh * cq_tab + pltpu.roll(qh, LANE - MLA_ROPE, 1) * sq_tab
        q_ref[:, h * LANE:(h + 1) * LANE] = qh.astype(BF16)
    nkv = _rms(ckv_ref[...]) * gkv_ref[...]
    sm = sm_ref[...]
    kr = sm * ck_tab + pltpu.roll(sm, LANE - MLA_ROPE, 1) * sk_tab
    lhs = jnp.concatenate([nkv, kr], axis=1).astype(BF16)
    k_ref[...] = _dot(lhs, wk_ref[...]).T.astype(BF16)
    v = _dot(lhs[:, :MLA_KV_RANK], wv_ref[...])
    lane = lax.broadcasted_iota(jnp.int32, v.shape, 1)
    v_ref[...] = jnp.where(lane % LANE == MLA_V, 1.0, v).astype(BF16)


def _mla_prep(cq, ckv, small, tab, gq, gkv, wq, wk, wv, *, l):
    tm = PREP_TM
    tiles_per_seq = SEQ // tm

    def tab_idx(i):
        return jnp.where(i < T_X // tm, i % tiles_per_seq, tiles_per_seq)

    return pl.pallas_call(
        _mla_prep_kernel,
        out_shape=[jax.ShapeDtypeStruct((T_ALL, MLA_HEADS * LANE), BF16),
                   jax.ShapeDtypeStruct((MLA_HEADS * LANE, T_ALL), BF16),
                   jax.ShapeDtypeStruct((T_ALL, MLA_HEADS * LANE), BF16)],
        grid=(T_ALL // tm,),
        in_specs=[
            pl.BlockSpec((tm, MLA_Q_RANK), lambda i: (i, 0)),
            pl.BlockSpec((tm, MLA_KV_RANK), lambda i: (i, 0)),
            pl.BlockSpec((tm, LANE), lambda i: (i, 0)),
            pl.BlockSpec((tm, 4 * LANE), lambda i: (tab_idx(i), 0)),
            pl.BlockSpec((None, 1, MLA_Q_RANK), lambda i: (l, 0, 0)),
            pl.BlockSpec((None, 1, MLA_KV_RANK), lambda i: (l, 0, 0)),
            pl.BlockSpec((None, MLA_Q_RANK, MLA_HEADS * LANE), lambda i: (l, 0, 0)),
            pl.BlockSpec((None, MLA_KV_RANK + LANE, MLA_HEADS * LANE), lambda i: (l, 0, 0)),
            pl.BlockSpec((None, MLA_KV_RANK, MLA_HEADS * LANE), lambda i: (l, 0, 0)),
        ],
        out_specs=[pl.BlockSpec((tm, MLA_HEADS * LANE), lambda i: (i, 0)),
                   pl.BlockSpec((MLA_HEADS * LANE, tm), lambda i: (0, i)),
                   pl.BlockSpec((tm, MLA_HEADS * LANE), lambda i: (i, 0))],
        compiler_params=_cparams(("parallel",)),
        name="mla_prep",
    )(cq, ckv, small, tab, gq, gkv, wq, wk, wv)


def _mla_attn_kernel(*refs, n_kv):
    q_ref = refs[0]
    k_refs = refs[1:1 + n_kv]
    v_refs = refs[1 + n_kv:1 + 2 * n_kv]
    o_ref = refs[1 + 2 * n_kv]
    outs = []
    for h in range(2):
        hl = slice(h * LANE, (h + 1) * LANE)
        q = q_ref[:, hl]
        m, acc = None, None
        for k_ref, v_ref in zip(k_refs, v_refs):
            n_keys = k_ref.shape[1]
            for c0 in range(0, n_keys, ATT_TK):
                c1 = min(c0 + ATT_TK, n_keys)
                s = _dot(q, k_ref[hl, c0:c1])
                m_new = s.max(axis=-1, keepdims=True)
                if m is not None:
                    m_new = jnp.maximum(m, m_new)
                    acc = acc * jnp.exp2(m - m_new)
                pv = _dot(jnp.exp2(s - m_new).astype(BF16), v_ref[c0:c1, hl])
                acc = pv if acc is None else acc + pv
                m = m_new
        outs.append(acc / acc[:, MLA_V:MLA_V + 1])
    lane = lax.broadcasted_iota(jnp.int32, outs[0].shape, 1)
    o_ref[...] = jnp.where(lane < MLA_V, outs[0], pltpu.roll(outs[1], MLA_V, 1))


def _mla_attn_latent(q, k, v):
    tq = ATT_TQ
    qt = SEQ // tq
    ctx_blk0 = T_X // CTX_LEN
    return pl.pallas_call(
        functools.partial(_mla_attn_kernel, n_kv=2),
        out_shape=jax.ShapeDtypeStruct((T_X, MLA_HEADS * MLA_V), F32),
        grid=(BATCH, MLA_HEADS // 2, qt),
        in_specs=[
            pl.BlockSpec((tq, 2 * LANE), lambda b, h, t: (b * qt + t, h)),
            pl.BlockSpec((2 * LANE, SEQ), lambda b, h, t: (h, b)),
            pl.BlockSpec((2 * LANE, CTX_LEN), lambda b, h, t: (h, ctx_blk0 + b)),
            pl.BlockSpec((SEQ, 2 * LANE), lambda b, h, t: (b, h)),
            pl.BlockSpec((CTX_LEN, 2 * LANE), lambda b, h, t: (ctx_blk0 + b, h)),
        ],
        out_specs=pl.BlockSpec((tq, LANE), lambda b, h, t: (b * qt + t, h)),
        compiler_params=_cparams(("parallel", "parallel", "arbitrary")),
        name="mla_attn_latent",
    )(q, k, k, v, v)


def _mla_attn_ctx(q, k, v):
    ctx_blk0 = T_X // CTX_LEN
    return pl.pallas_call(
        functools.partial(_mla_attn_kernel, n_kv=1),
        out_shape=jax.ShapeDtypeStruct((T_C, MLA_HEADS * MLA_V), F32),
        grid=(BATCH, MLA_HEADS // 2),
        in_specs=[
            pl.BlockSpec((CTX_LEN, 2 * LANE), lambda b, h: (ctx_blk0 + b, h)),
            pl.BlockSpec((2 * LANE, CTX_LEN), lambda b, h: (h, ctx_blk0 + b)),
            pl.BlockSpec((CTX_LEN, 2 * LANE), lambda b, h: (ctx_blk0 + b, h)),
        ],
        out_specs=pl.BlockSpec((CTX_LEN, LANE), lambda b, h: (b, h)),
        compiler_params=_cparams(("parallel", "parallel")),
        name="mla_attn_ctx",
    )(q, k, v)


def _swa_core(q, keys, vals, masks, sinks, o_ref):
    w = q.shape[0]
    lane = lax.broadcasted_iota(jnp.int32, (w, LANE), 1)
    low = lane < SWA_HEADDIM
    rep = SWA_HEADS // SWA_KV_HEADS
    for g in range(SWA_KV_HEADS):
        parts = []
        for j in range(rep // 2):
            qp = q[:, (g * rep // 2 + j) * LANE:(g * rep // 2 + j + 1) * LANE]
            parts.append(jnp.where(low, qp, 0.0))
            parts.append(jnp.where(low, 0.0, qp))
        lhs = jnp.concatenate(parts, axis=0).astype(BF16)
        sink = jnp.concatenate([jnp.full((w, 1), sinks[g * rep + r], F32) for r in range(rep)], axis=0)
        s = []
        for kb, mask in zip(keys, masks):
            sb = _dot_t(lhs, kb[:, g * LANE:(g + 1) * LANE].astype(BF16))
            if mask is not None:
                sb = jnp.where(jnp.concatenate([mask] * rep, axis=0), sb, NEG)
            s.append(sb)
        m = sink
        for sb in s:
            m = jnp.maximum(m, sb.max(axis=-1, keepdims=True))
        den = jnp.exp(sink - m)
        o = None
        for sb, vb in zip(s, vals):
            p = jnp.exp(sb - m)
            den = den + p.sum(axis=-1, keepdims=True)
            pv = _dot(p.astype(BF16), vb[:, g * LANE:(g + 1) * LANE].astype(BF16))
            o = pv if o is None else o + pv
        o = o / den
        for j in range(rep // 2):
            o_ref[:, (g * rep // 2 + j) * LANE:(g * rep // 2 + j + 1) * LANE] = jnp.where(
                low, o[2 * j * w:(2 * j + 1) * w], o[(2 * j + 1) * w:(2 * j + 2) * w])


def _rope_rows(x_ref, tab_ref, n, reps):
    cos = jnp.concatenate([tab_ref[:, :LANE]] * reps, axis=1)
    sin = jnp.concatenate([tab_ref[:, LANE:]] * reps, axis=1)
    return x_ref[:, :n] * cos + x_ref[:, n:] * sin


def _swa_latent_kernel(sink_ref, q_ref, kp_ref, kc_ref, kn_ref, kx_ref, vp_ref, vc_ref, vn_ref,
                       vx_ref, tq_ref, tp_ref, tn_ref, o_ref, *, l):
    n = pl.program_id(1)
    scale = SWA_HEADDIM ** -0.5
    q = _rope_rows(q_ref, tq_ref, N_SWA_Q, N_SWA_Q // LANE) * scale
    keys = [kx_ref[:, :N_SWA_KV2],
            _rope_rows(kp_ref, tp_ref, N_SWA_KV2, N_SWA_KV2 // LANE),
            _rope_rows(kc_ref, tq_ref, N_SWA_KV2, N_SWA_KV2 // LANE),
            _rope_rows(kn_ref, tn_ref, N_SWA_KV2, N_SWA_KV2 // LANE)]
    vals = [vx_ref[...], vp_ref[...], vc_ref[...], vn_ref[...]]
    qi = lax.broadcasted_iota(jnp.int32, (SWA_BLOCK, SWA_BLOCK), 0)
    kj = lax.broadcasted_iota(jnp.int32, (SWA_BLOCK, SWA_BLOCK), 1)
    prev_ok = jnp.logical_and(kj - qi >= SWA_BLOCK - SWA_WINDOW, n > 0)
    next_ok = jnp.logical_and(kj - qi <= SWA_WINDOW - SWA_BLOCK, n < SEQ // SWA_BLOCK - 1)
    sinks = [sink_ref[l, h] for h in range(SWA_HEADS)]
    _swa_core(q, keys, vals, [None, prev_ok, None, next_ok], sinks, o_ref)


def _swa_ctx_kernel(sink_ref, q_ref, kx_ref, vx_ref, o_ref, *, l):
    q = q_ref[:, :N_SWA_Q] * (SWA_HEADDIM ** -0.5)
    sinks = [sink_ref[l, h] for h in range(SWA_HEADS)]
    _swa_core(q, [kx_ref[:, :N_SWA_KV2]], [vx_ref[...]], [None], sinks, o_ref)


def _swa_latent(sink, qs2, ks2, vs2, tab, *, l):
    nb = SEQ // SWA_BLOCK
    ctx_blk0 = T_X // CTX_LEN
    w = SWA_BLOCK

    def prev(b, n):
        return b * nb + jnp.maximum(n - 1, 0)

    def nxt(b, n):
        return b * nb + jnp.minimum(n + 1, nb - 1)

    return pl.pallas_call(
        functools.partial(_swa_latent_kernel, l=l),
        out_shape=jax.ShapeDtypeStruct((T_X, N_SWA_Q), F32),
        grid=(BATCH, nb),
        in_specs=[
            pl.BlockSpec(memory_space=pltpu.SMEM),
            pl.BlockSpec((w, 2 * N_SWA_Q), lambda b, n: (b * nb + n, 0)),
            pl.BlockSpec((w, 2 * N_SWA_KV2), lambda b, n: (prev(b, n), 0)),
            pl.BlockSpec((w, 2 * N_SWA_KV2), lambda b, n: (b * nb + n, 0)),
            pl.BlockSpec((w, 2 * N_SWA_KV2), lambda b, n: (nxt(b, n), 0)),
            pl.BlockSpec((CTX_LEN, 2 * N_SWA_KV2), lambda b, n: (ctx_blk0 + b, 0)),
            pl.BlockSpec((w, N_SWA_KV2), lambda b, n: (prev(b, n), 0)),
            pl.BlockSpec((w, N_SWA_KV2), lambda b, n: (b * nb + n, 0)),
            pl.BlockSpec((w, N_SWA_KV2), lambda b, n: (nxt(b, n), 0)),
            pl.BlockSpec((CTX_LEN, N_SWA_KV2), lambda b, n: (ctx_blk0 + b, 0)),
            pl.BlockSpec((w, 2 * LANE), lambda b, n: (n, 0)),
            pl.BlockSpec((w, 2 * LANE), lambda b, n: (jnp.maximum(n - 1, 0), 0)),
            pl.BlockSpec((w, 2 * LANE), lambda b, n: (jnp.minimum(n + 1, nb - 1), 0)),
        ],
        out_specs=pl.BlockSpec((w, N_SWA_Q), lambda b, n: (b * nb + n, 0)),
        compiler_params=_cparams(("parallel", "arbitrary")),
        name="swa_latent",
    )(sink, qs2, ks2, ks2, ks2, ks2, vs2, vs2, vs2, vs2, tab, tab, tab)


def _swa_ctx(sink, qs2, ks2, vs2, *, l):
    w = SWA_BLOCK
    blk0 = T_X // w
    per_b = CTX_LEN // w
    ctx_blk0 = T_X // CTX_LEN
    return pl.pallas_call(
        functools.partial(_swa_ctx_kernel, l=l),
        out_shape=jax.ShapeDtypeStruct((T_C, N_SWA_Q), F32),
        grid=(BATCH, per_b),
        in_specs=[
            pl.BlockSpec(memory_space=pltpu.SMEM),
            pl.BlockSpec((w, 2 * N_SWA_Q), lambda b, n: (blk0 + b * per_b + n, 0)),
            pl.BlockSpec((CTX_LEN, 2 * N_SWA_KV2), lambda b, n: (ctx_blk0 + b, 0)),
            pl.BlockSpec((CTX_LEN, N_SWA_KV2), lambda b, n: (ctx_blk0 + b, 0)),
        ],
        out_specs=pl.BlockSpec((w, N_SWA_Q), lambda b, n: (b * per_b + n, 0)),
        compiler_params=_cparams(("parallel", "arbitrary")),
        name="swa_ctx",
    )(sink, qs2, ks2, vs2)


def _outproj_kernel(h_ref, mod_ref, ssd_ref, mla_ref, mla_c_ref, swa_ref, swa_c_ref, gm_ref, gs_ref,
                    w_ref, o_ref, *, n_lat):
    n_mla = MLA_HEADS * MLA_V
    is_lat = pl.program_id(0) < n_lat
    acc = _dot(ssd_ref[...], w_ref[0:SSD_INNER, :])
    mla = jnp.where(is_lat, mla_ref[...], mla_c_ref[...])
    mla = (_rms(mla) * gm_ref[...]).astype(BF16)
    acc += _dot(mla, w_ref[SSD_INNER:SSD_INNER + n_mla, :])
    swa = jnp.where(is_lat, swa_ref[...], swa_c_ref[...])
    swa = (_rms(swa) * gs_ref[...]).astype(BF16)
    acc += _dot(swa, w_ref[SSD_INNER + n_mla:, :])
    o_ref[...] = h_ref[...] + mod_ref[5:6, :] * acc


def _outproj(h, mod_l, ssd, mla, mla_c, swa, swa_c, gm, gs, w_out, *, l, rows):
    tm = OUT_TM
    n_mla = MLA_HEADS * MLA_V
    n_lat = T_X // tm

    def lat(i):
        return (jnp.minimum(i, n_lat - 1), 0)

    def ctx(i):
        return (jnp.maximum(i - n_lat, 0), 0)

    return pl.pallas_call(
        functools.partial(_outproj_kernel, n_lat=n_lat),
        out_shape=jax.ShapeDtypeStruct((rows, D_MODEL), F32),
        grid=(rows // tm,),
        in_specs=[
            pl.BlockSpec((tm, D_MODEL), lambda i: (i, 0)),
            pl.BlockSpec((None, N_MOD, D_MODEL), lambda i: (_group_of_tile(i, tm), 0, 0)),
            pl.BlockSpec((tm, SSD_INNER), lambda i: (i, 0)),
            pl.BlockSpec((tm, n_mla), lat),
            pl.BlockSpec((tm, n_mla), ctx),
            pl.BlockSpec((tm, N_SWA_Q), lat),
            pl.BlockSpec((tm, N_SWA_Q), ctx),
            pl.BlockSpec((None, 1, n_mla), lambda i: (l, 0, 0)),
            pl.BlockSpec((None, 1, N_SWA_Q), lambda i: (l, 0, 0)),
            pl.BlockSpec((None, D_MIX, D_MODEL), lambda i: (l, 0, 0)),
        ],
        out_specs=pl.BlockSpec((tm, D_MODEL), lambda i: (i, 0)),
        compiler_params=_cparams(("parallel",)),
        name="outproj",
    )(h, mod_l, ssd, mla, mla_c, swa, swa_c, gm, gs, w_out)


def _rot_src_sign(dim):
    q = dim // 4
    src = np.concatenate([np.arange(q, 2 * q), np.arange(0, q),
                          np.arange(3 * q, 4 * q), np.arange(2 * q, 3 * q)])
    sign = np.concatenate([-np.ones(q), np.ones(q), -np.ones(q), np.ones(q)])
    return src, sign


def _permute_cols(w, idx, sgn):
    pieces, start, n = [], 0, len(idx)
    while start < n:
        end = start + 1
        if sgn[start] == 0:
            while end < n and sgn[end] == 0:
                end += 1
            pieces.append(jnp.zeros(w.shape[:-1] + (end - start,), w.dtype))
        else:
            while end < n and idx[end] == idx[end - 1] + 1 and sgn[end] == sgn[start]:
                end += 1
            piece = w[..., int(idx[start]):int(idx[end - 1]) + 1]
            pieces.append(piece if sgn[start] > 0 else -piece)
        start = end
    return jnp.concatenate(pieces, axis=-1)


def _w_in_layout():
    offs = np.concatenate([[0], np.cumsum(IN_SIZES)])
    o_z, o_xbc, o_dt, o_cq, o_ckv, o_kr, o_qs, o_ks, o_vs = offs[:-1]
    idx, sgn = [], []
    ident = (np.arange(SWA_HEADDIM), np.ones(SWA_HEADDIM))

    def plain(start, n):
        idx.append(np.arange(start, start + n))
        sgn.append(np.ones(n))

    def heads(start, n_heads, dim, src_sign, copies):
        src, sign = src_sign
        for h in range(n_heads):
            for _ in range(copies):
                idx.append(start + h * dim + src)
                sgn.append(sign)

    plain(o_z, SSD_INNER)
    plain(o_xbc, SSD_CONV_CH)
    plain(o_cq, MLA_Q_RANK)
    plain(o_ckv, MLA_KV_RANK)
    plain(o_qs, N_SWA_Q)
    heads(o_qs, SWA_HEADS, SWA_HEADDIM, _rot_src_sign(SWA_HEADDIM), 1)
    heads(o_ks, SWA_KV_HEADS, SWA_HEADDIM, ident, 2)
    heads(o_ks, SWA_KV_HEADS, SWA_HEADDIM, _rot_src_sign(SWA_HEADDIM), 2)
    heads(o_vs, SWA_KV_HEADS, SWA_HEADDIM, ident, 2)
    plain(o_dt, 2 * SSD_HEADS)
    plain(o_kr, MLA_ROPE)
    heads(o_kr, 1, MLA_ROPE, _rot_src_sign(MLA_ROPE), 1)
    idx.append(np.zeros(MLA_ROPE, np.int64))
    sgn.append(np.zeros(MLA_ROPE))
    idx = np.concatenate(idx).astype(np.int32)
    sgn = np.concatenate(sgn).astype(np.float32)
    assert idx.shape[0] == W_IN_COLS
    return idx, sgn


def _w_uq_layout():
    hd = MLA_NOPE + MLA_ROPE
    src, sign = _rot_src_sign(MLA_ROPE)
    idx, sgn = [], []
    for h in range(MLA_HEADS):
        idx.append(h * hd + np.arange(hd))
        sgn.append(np.ones(hd))
        idx.append(h * hd + MLA_NOPE + src)
        sgn.append(sign)
    return np.concatenate(idx).astype(np.int32), np.concatenate(sgn).astype(np.float32)


def _w_ukv_layout():
    hd = MLA_NOPE + MLA_V
    kidx, vidx, sgn = [], [], []
    for h in range(MLA_HEADS):
        kidx.append(h * hd + np.arange(MLA_NOPE))
        vidx.append(h * hd + MLA_NOPE + np.arange(MLA_V))
        sgn.append(np.ones(MLA_NOPE))
        kidx.append(np.zeros(LANE - MLA_NOPE, np.int64))
        vidx.append(np.zeros(LANE - MLA_V, np.int64))
        sgn.append(np.zeros(LANE - MLA_NOPE))
    return (np.concatenate(kidx).astype(np.int32), np.concatenate(vidx).astype(np.int32),
            np.concatenate(sgn).astype(np.float32))


def _k_rope_expand():
    e = np.zeros((LANE, MLA_HEADS * LANE), np.float32)
    for h in range(MLA_HEADS):
        for i in range(MLA_ROPE):
            e[MLA_ROPE + i, h * LANE + MLA_NOPE + i] = 1.0
    return e


def _head_expand():
    e = np.zeros((SSD_HEADS, SSD_INNER), np.float32)
    for h in range(SSD_HEADS):
        e[h, h * SSD_HEADDIM:(h + 1) * SSD_HEADDIM] = 1.0
    return e


def _rope_angles(dim):
    pos = np.arange(SEQ)
    quarter = dim // 4
    inv_freq = ROPE_THETA ** (-jnp.arange(quarter, dtype=F32) / quarter)
    ang_r = jnp.asarray(pos // GRID_W, F32)[:, None] * inv_freq[None, :]
    ang_c = jnp.asarray(pos % GRID_W, F32)[:, None] * inv_freq[None, :]
    ang = jnp.concatenate([ang_r, ang_r, ang_c, ang_c], axis=-1)
    return jnp.cos(ang), jnp.sin(ang)


def _mla_rope_table():
    cos, sin = _rope_angles(MLA_ROPE)
    scale = (MLA_NOPE + MLA_ROPE) ** -0.5 * math.log2(math.e)
    n = SEQ + PREP_TM
    ident = jnp.ones((PREP_TM, MLA_ROPE), F32)
    zeros = jnp.zeros((PREP_TM, MLA_ROPE), F32)
    cos = jnp.concatenate([cos, ident], axis=0)
    sin = jnp.concatenate([sin, zeros], axis=0)
    z32 = jnp.zeros((n, MLA_ROPE), F32)
    cq = jnp.concatenate([jnp.ones((n, MLA_NOPE), F32), cos, z32], axis=1) * scale
    sq = jnp.concatenate([jnp.zeros((n, MLA_NOPE), F32), sin, z32], axis=1) * scale
    ck = jnp.concatenate([z32, cos, z32, z32], axis=1)
    sk = jnp.concatenate([z32, sin, z32, z32], axis=1)
    return jnp.concatenate([cq, sq, ck, sk], axis=1)


def _swa_rope_table():
    cos, sin = _rope_angles(SWA_HEADDIM)
    return jnp.concatenate([cos, cos, sin, sin], axis=1)


def kernel(x, c, ctx, c_ctx, w_mod, b_mod, norm_g, ffn_w_gate, ffn_w_up, ffn_w_down,
           w_in, w_out, ssd_conv_w, ssd_conv_b, ssd_dt_bias, ssd_a_log, ssd_d, ssd_norm,
           mla_q_norm, mla_w_uq, mla_kv_norm, mla_w_ukv, mla_out_norm, swa_sink,
           swa_out_norm, final_norm):
    idx, sgn = _w_in_layout()
    w_in_p = _permute_cols(w_in, idx, sgn).astype(BF16)
    qidx, qsgn = _w_uq_layout()
    wq = _permute_cols(mla_w_uq, qidx, qsgn).astype(BF16)
    kidx, vidx, kvsgn = _w_ukv_layout()
    k_expand = jnp.broadcast_to(jnp.asarray(_k_rope_expand()), (DEPTH, LANE, MLA_HEADS * LANE))
    wk = jnp.concatenate([_permute_cols(mla_w_ukv, kidx, kvsgn), k_expand], axis=1).astype(BF16)
    wv = _permute_cols(mla_w_ukv, vidx, kvsgn).astype(BF16)
    wg = ffn_w_gate.astype(BF16)
    wu = ffn_w_up.astype(BF16)
    wd = ffn_w_down.astype(BF16)
    wo = w_out.astype(BF16)
    mla_tab = _mla_rope_table()
    swa_tab = _swa_rope_table()
    e16 = jnp.asarray(_head_expand())
    dskip_e = jnp.repeat(ssd_d, SSD_HEADDIM, axis=-1).reshape(DEPTH, 1, SSD_INNER)
    ssd_gn = ssd_norm.reshape(DEPTH, 1, SSD_INNER)
    conv_b = ssd_conv_b.reshape(DEPTH, 1, SSD_CONV_CH)
    bias_r = ssd_dt_bias.reshape(DEPTH, 2, 1, SSD_HEADS)
    bias_c = ssd_dt_bias.reshape(DEPTH, 2, SSD_HEADS, 1)
    alog_r = ssd_a_log.reshape(DEPTH, 2, 1, SSD_HEADS)
    alog_c = ssd_a_log.reshape(DEPTH, 2, SSD_HEADS, 1)
    gq = mla_q_norm.reshape(DEPTH, 1, MLA_Q_RANK)
    gkv = mla_kv_norm.reshape(DEPTH, 1, MLA_KV_RANK)
    gm = mla_out_norm.reshape(DEPTH, 1, -1)
    gs = swa_out_norm.reshape(DEPTH, 1, -1)
    fin = final_norm.reshape(1, D_MODEL)

    c_all = jnp.concatenate([c, c_ctx[None, :], jnp.zeros((GROUP_PAD - N_GROUPS, D_MODEL), F32)], axis=0)
    mod_all = _modulation(c_all, w_mod, b_mod).reshape(DEPTH, GROUP_PAD, N_MOD, D_MODEL)

    h = x.reshape(T_X, D_MODEL)
    h_ctx = ctx.reshape(T_C, D_MODEL)

    for l in range(DEPTH):
        last = l == DEPTH - 1
        mod_l = mod_all[l]
        h = _ffn(h, h_ctx, mod_l, norm_g[l, 0].reshape(1, D_MODEL), wg, wu, wd, fin,
                 l=l, k=0, base=0, rows=T_ALL)
        h_ctx = None
        z, xbc, cq, ckv, qs2, ks2, vs2, small = _inproj(h, mod_l, norm_g[l, 1].reshape(1, D_MODEL),
                                                       w_in_p, l=l)
        dt2 = small[:, :2 * SSD_HEADS].reshape(T_ALL, 2, SSD_HEADS)
        dt_col = jnp.transpose(dt2, (1, 0, 2))
        dt_row = jnp.transpose(dt2, (1, 2, 0))
        ssd_o = _ssd(xbc, z, dt_col, dt_row, ssd_conv_w, conv_b, bias_r, bias_c, alog_r, alog_c,
                     dskip_e, ssd_gn, e16, l=l)
        q_m, k_m, v_m = _mla_prep(cq, ckv, small, mla_tab, gq, gkv, wq, wk, wv, l=l)
        mla_x = _mla_attn_latent(q_m, k_m, v_m)
        swa_x = _swa_latent(swa_sink, qs2, ks2, vs2, swa_tab, l=l)
        if last:
            rows, mla_c, swa_c = T_X, mla_x, swa_x
        else:
            rows = T_ALL
            mla_c = _mla_attn_ctx(q_m, k_m, v_m)
            swa_c = _swa_ctx(swa_sink, qs2, ks2, vs2, l=l)
        h = _outproj(h, mod_l, ssd_o, mla_x, mla_c, swa_x, swa_c, gm, gs, wo, l=l, rows=rows)
        h = _ffn(h, None, mod_l, norm_g[l, 2].reshape(1, D_MODEL), wg, wu, wd, fin,
                 l=l, k=1, base=6, rows=rows, final_norm=last)
    return h.reshape(BATCH, SEQ, D_MODEL)
```

```python
import functools
import math

import jax
import jax.numpy as jnp
import numpy as np
from jax import lax
from jax.experimental import pallas as pl
from jax.experimental.pallas import tpu as pltpu

F32 = jnp.float32
BF16 = jnp.bfloat16
HIGHEST = lax.Precision.HIGHEST

D_MODEL = 2048
BATCH = 4
SEQ = 4096
DEPTH = 4
GRID_W = 64
CTX_LEN = 256
EPS = 1e-6
ROPE_THETA = 10000.0
N_MOD = 9
D_FF = 5632

SSD_HEADS = 16
SSD_HEADDIM = 64
SSD_INNER = SSD_HEADS * SSD_HEADDIM
SSD_GROUPS = 2
SSD_STATE = 128
SSD_CONV = 5
SSD_CONV_CH = SSD_INNER + 2 * SSD_GROUPS * SSD_STATE

MLA_HEADS = 8
MLA_Q_RANK = 384
MLA_KV_RANK = 256
MLA_NOPE = 64
MLA_ROPE = 32
MLA_V = 64

SWA_HEADS = 8
SWA_KV_HEADS = 2
SWA_HEADDIM = 64
SWA_WINDOW = 128
SWA_BLOCK = 128

D_MIX = SSD_INNER + MLA_HEADS * MLA_V + SWA_HEADS * SWA_HEADDIM
IN_SIZES = (SSD_INNER, SSD_CONV_CH, 2 * SSD_HEADS, MLA_Q_RANK, MLA_KV_RANK, MLA_ROPE,
            SWA_HEADS * SWA_HEADDIM, SWA_KV_HEADS * SWA_HEADDIM, SWA_KV_HEADS * SWA_HEADDIM)

T_X = BATCH * SEQ
T_C = BATCH * CTX_LEN
T_ALL = T_X + T_C
N_GROUPS = BATCH + 1
GROUP_PAD = 8

LANE = 128
SUBLANE = 8
VMEM_LIMIT = 56 * 1024 * 1024
NEG = -1e30
LOG2E = math.log2(math.e)

FFN_TM = 512
FFN_TF = 512
PROJ_TM = 256
OUT_TM = 512
PREP_TM = 512
ATT_TQ = 512
ATT_TK = 1024
MOD_TN = 1024

SSD_Q = 128
SEQ_CHUNKS = SEQ // SSD_Q
CTX_CHUNKS = CTX_LEN // SSD_Q
SSD_STEPS = SEQ_CHUNKS + CTX_CHUNKS
HALO = SUBLANE
N_SWA_Q = SWA_HEADS * SWA_HEADDIM
N_SWA_KV2 = 2 * SWA_KV_HEADS * SWA_HEADDIM
SWA_QSCALE = SWA_HEADDIM ** -0.5 * LOG2E

SEG_Z = (0, 1024)
SEG_XBC = (1024, 2560)
SEG_CQ = (2560, 2944)
SEG_CKV = (2944, 3200)
SEG_QS = (3200, 4224)
SEG_KS = (4224, 4736)
SEG_VS = (4736, 4992)
SEG_SMALL = (4992, 5120)
W_IN_COLS = 5120
IN_SEGS = (SEG_Z, SEG_XBC, SEG_CQ, SEG_CKV, SEG_QS, SEG_KS, SEG_VS, SEG_SMALL)


def _cparams(sem):
    return pltpu.CompilerParams(dimension_semantics=sem, vmem_limit_bytes=VMEM_LIMIT)


def _group_of_tile(i, tm):
    return jnp.minimum((i * tm) // SEQ, BATCH)


def _rms(x):
    return x * lax.rsqrt(jnp.mean(x * x, axis=-1, keepdims=True) + EPS)


def _silu(x):
    return x * jax.nn.sigmoid(x)


def _softplus(x):
    return jnp.maximum(x, 0.0) + jnp.log1p(jnp.exp(-jnp.abs(x)))


def _dot(a, b):
    return jnp.dot(a, b, preferred_element_type=F32)


def _dot_t(a, b):
    return lax.dot_general(a, b, (((1,), (1,)), ((), ())), preferred_element_type=F32)


def _mod_kernel(c_ref, w_ref, b_ref, o_ref):
    sc = _silu(c_ref[...]).astype(BF16)
    o_ref[...] = _dot(sc, w_ref[...].astype(BF16)) + b_ref[...]


def _modulation(c_all, w_mod, b_mod):
    n = N_MOD * D_MODEL
    return pl.pallas_call(
        _mod_kernel,
        out_shape=jax.ShapeDtypeStruct((DEPTH, GROUP_PAD, n), F32),
        grid=(DEPTH, n // MOD_TN),
        in_specs=[
            pl.BlockSpec((GROUP_PAD, D_MODEL), lambda l, j: (0, 0)),
            pl.BlockSpec((None, D_MODEL, MOD_TN), lambda l, j: (l, 0, j)),
            pl.BlockSpec((None, 1, MOD_TN), lambda l, j: (l, 0, j)),
        ],
        out_specs=pl.BlockSpec((None, GROUP_PAD, MOD_TN), lambda l, j: (l, 0, j)),
        compiler_params=_cparams(("parallel", "parallel")),
        name="modulation",
    )(c_all, w_mod, b_mod.reshape(DEPTH, 1, n))


def _ffn_kernel(*refs, split, base, n_ff, n_lat, final_norm):
    x_ref, xc_ref = (refs[0], refs[1]) if split else (refs[0], None)
    mod_ref, g_ref, wg_ref, wu_ref, wd_ref, fin_ref, o_ref, u_ref = refs[(2 if split else 1):]
    j = pl.program_id(1)

    def residual():
        if xc_ref is None:
            return x_ref[...]
        return jnp.where(pl.program_id(0) < n_lat, x_ref[...], xc_ref[...])

    @pl.when(j == 0)
    def _():
        y = _rms(residual()) * g_ref[...]
        u = y * (1.0 + mod_ref[base + 1:base + 2, :]) + mod_ref[base:base + 1, :]
        u_ref[...] = u.astype(BF16)
        o_ref[...] = jnp.zeros_like(o_ref)

    u = u_ref[...]
    gate = _dot(u, wg_ref[...])
    up = _dot(u, wu_ref[...])
    a = (_silu(gate) * up).astype(BF16)
    o_ref[...] += _dot(a, wd_ref[...])

    @pl.when(j == n_ff - 1)
    def _():
        h = residual() + (0.5 * mod_ref[base + 2:base + 3, :]) * o_ref[...]
        if final_norm:
            h = _rms(h) * fin_ref[...]
        o_ref[...] = h


def _ffn(h, h_ctx, mod_l, g, wg, wu, wd, fin, *, l, k, base, rows, final_norm=False):
    n_ff = D_FF // FFN_TF
    tm = FFN_TM
    n_lat = T_X // tm
    split = h_ctx is not None
    kern = functools.partial(_ffn_kernel, split=split, base=base, n_ff=n_ff, n_lat=n_lat,
                             final_norm=final_norm)
    if split:
        x_specs = [pl.BlockSpec((tm, D_MODEL), lambda i, j: (jnp.minimum(i, n_lat - 1), 0)),
                   pl.BlockSpec((tm, D_MODEL), lambda i, j: (jnp.maximum(i - n_lat, 0), 0))]
        x_args = (h, h_ctx)
    else:
        x_specs = [pl.BlockSpec((tm, D_MODEL), lambda i, j: (i, 0))]
        x_args = (h,)
    return pl.pallas_call(
        kern,
        out_shape=jax.ShapeDtypeStruct((rows, D_MODEL), F32),
        grid=(rows // tm, n_ff),
        in_specs=x_specs + [
            pl.BlockSpec((None, N_MOD, D_MODEL), lambda i, j: (_group_of_tile(i, tm), 0, 0)),
            pl.BlockSpec((1, D_MODEL), lambda i, j: (0, 0)),
            pl.BlockSpec((None, None, D_MODEL, FFN_TF), lambda i, j: (l, k, 0, j)),
            pl.BlockSpec((None, None, D_MODEL, FFN_TF), lambda i, j: (l, k, 0, j)),
            pl.BlockSpec((None, None, FFN_TF, D_MODEL), lambda i, j: (l, k, j, 0)),
            pl.BlockSpec((1, D_MODEL), lambda i, j: (0, 0)),
        ],
        out_specs=pl.BlockSpec((tm, D_MODEL), lambda i, j: (i, 0)),
        scratch_shapes=[pltpu.VMEM((tm, D_MODEL), BF16)],
        compiler_params=_cparams(("parallel", "arbitrary")),
        name="ffn",
    )(*x_args, mod_l, g, wg, wu, wd, fin)


def _inproj_kernel(x_ref, mod_ref, g_ref, w_ref, *o_refs):
    y = _rms(x_ref[...]) * g_ref[...]
    u = (y * (1.0 + mod_ref[4:5, :]) + mod_ref[3:4, :]).astype(BF16)
    for (a, b), o_ref in zip(IN_SEGS, o_refs):
        o_ref[...] = _dot(u, w_ref[:, a:b])


def _inproj(h, mod_l, g, w_in_p, *, l):
    tm = PROJ_TM
    return pl.pallas_call(
        _inproj_kernel,
        out_shape=[jax.ShapeDtypeStruct((T_ALL, b - a), F32) for a, b in IN_SEGS],
        grid=(T_ALL // tm,),
        in_specs=[
            pl.BlockSpec((tm, D_MODEL), lambda i: (i, 0)),
            pl.BlockSpec((None, N_MOD, D_MODEL), lambda i: (_group_of_tile(i, tm), 0, 0)),
            pl.BlockSpec((1, D_MODEL), lambda i: (0, 0)),
            pl.BlockSpec((None, D_MODEL, W_IN_COLS), lambda i: (l, 0, 0),
                         pipeline_mode=pl.Buffered(1)),
        ],
        out_specs=[pl.BlockSpec((tm, b - a), lambda i: (i, 0)) for a, b in IN_SEGS],
        compiler_params=_cparams(("parallel",)),
        name="inproj",
    )(h, mod_l, g, w_in_p)


def _ssd_chunk_ids(phase, s):
    fwd = phase == 1
    is_ctx = s < CTX_CHUNKS
    cc = jnp.where(fwd, s, CTX_CHUNKS - 1 - s)
    lc = jnp.where(fwd, s - CTX_CHUNKS, SSD_STEPS - 1 - s)
    return is_ctx, cc, lc


def _ssd_row_block(b, phase, s):
    is_ctx, cc, lc = _ssd_chunk_ids(phase, s)
    return jnp.where(is_ctx, T_X // SSD_Q + b * CTX_CHUNKS + cc, b * SEQ_CHUNKS + lc)


def _ssd_kernel(xbc_ref, prev_ref, next_ref, z_ref, dtc_ref, dtr_ref, cw_ref, cb_ref,
                bias_r_ref, bias_c_ref, alog_r_ref, alog_c_ref, dskip_ref, gn_ref, e16_ref,
                o_ref, xe_ref, state_ref, ybwd_ref, y_ref):
    phase = pl.program_id(1)
    s = pl.program_id(2)
    fwd = phase == 1
    is_ctx, cc, lc = _ssd_chunk_ids(phase, s)
    first = jnp.where(is_ctx, cc == 0, lc == 0)
    last = jnp.where(is_ctx, cc == CTX_CHUNKS - 1, lc == SEQ_CHUNKS - 1)
    voff = pl.multiple_of(jnp.where(is_ctx, cc, CTX_CHUNKS + lc) * SSD_Q, SSD_Q)

    @pl.when(s == 0)
    def _():
        state_ref[...] = jnp.zeros_like(state_ref)

    xe_ref[0:HALO, :] = jnp.where(first, 0.0, prev_ref[...])
    xe_ref[HALO:HALO + SSD_Q, :] = xbc_ref[...]
    xe_ref[HALO + SSD_Q:, :] = jnp.where(last, 0.0, next_ref[...])
    acc = cb_ref[...] + cw_ref[0:1, :] * xe_ref[HALO - 2:HALO - 2 + SSD_Q, :]
    for k in range(1, SSD_CONV):
        acc = acc + cw_ref[k:k + 1, :] * xe_ref[HALO - 2 + k:HALO - 2 + k + SSD_Q, :]
    xc = _silu(acc)
    xs = xc[:, :SSD_INNER]
    n_bc = SSD_GROUPS * SSD_STATE

    dt_c = _softplus(dtc_ref[...] + bias_r_ref[...])
    dt_r = _softplus(dtr_ref[...] + bias_c_ref[...])
    a_c = dt_c * -jnp.exp(alog_r_ref[...])
    a_r = dt_r * -jnp.exp(alog_c_ref[...])
    row = lax.broadcasted_iota(jnp.int32, (SSD_Q, SSD_Q), 0)
    col = lax.broadcasted_iota(jnp.int32, (SSD_Q, SSD_Q), 1)
    d = (row - col) * jnp.where(fwd, 1, -1)
    causal = d >= 0
    s_c = jnp.dot(causal.astype(F32), a_c, precision=HIGHEST, preferred_element_type=F32)
    s_r = jnp.dot(a_r, (d <= 0).astype(F32), precision=HIGHEST, preferred_element_type=F32)

    e16 = e16_ref[...]
    rem = jnp.concatenate([dt_c, s_c], axis=0)
    expanded = None
    for _ in range(3):
        term = rem.astype(BF16)
        rem = rem - term.astype(F32)
        part = _dot(term, e16)
        expanded = part if expanded is None else expanded + part
    dt_e = expanded[:SSD_Q]
    s_e = expanded[SSD_Q:]
    tot_e = jnp.where(fwd, s_e[SSD_Q - 1:SSD_Q, :], s_e[0:1, :])
    xdt = xs * dt_e
    xdt_end = (xdt * jnp.exp(tot_e - s_e)).astype(BF16)
    xdt16 = xdt.astype(BF16)
    exps_e = jnp.exp(s_e)
    dec_e = jnp.exp(tot_e)
    lane = lax.broadcasted_iota(jnp.int32, (SSD_Q, LANE), 1)

    hp = SSD_INNER // SSD_GROUPS
    heads_per_group = SSD_HEADS // SSD_GROUPS
    for g in range(SSD_GROUPS):
        bg = xc[:, SSD_INNER + g * SSD_STATE:SSD_INNER + (g + 1) * SSD_STATE]
        cg = xc[:, SSD_INNER + n_bc + g * SSD_STATE:SSD_INNER + n_bc + (g + 1) * SSD_STATE]
        bt16 = bg.T.astype(BF16)
        cg16 = cg.astype(BF16)
        cb = _dot(cg16, bt16)
        st = state_ref[:, g * hp:(g + 1) * hp]
        y_off = _dot(cg16, st.astype(BF16)) * exps_e[:, g * hp:(g + 1) * hp]
        state_ref[:, g * hp:(g + 1) * hp] = (dec_e[:, g * hp:(g + 1) * hp] * st
                                             + _dot(bt16, xdt_end[:, g * hp:(g + 1) * hp]))
        for j in range(heads_per_group // 2):
            h0 = g * heads_per_group + 2 * j
            m = []
            for h in (h0, h0 + 1):
                seg = s_c[:, h:h + 1] - s_r[h:h + 1, :]
                m.append(cb * jnp.exp(jnp.where(causal, seg, NEG)))
            m = jnp.concatenate(m, axis=0).astype(BF16)
            lo = h0 * SSD_HEADDIM
            yp = _dot(m, xdt16[:, lo:lo + LANE])
            y_ref[:, lo:lo + LANE] = (jnp.where(lane < SSD_HEADDIM, yp[:SSD_Q], yp[SSD_Q:])
                                      + y_off[:, lo - g * hp:lo - g * hp + LANE])

    @pl.when(jnp.logical_not(fwd))
    def _():
        ybwd_ref[pl.ds(voff, SSD_Q), :] = y_ref[...]

    @pl.when(fwd)
    def _():
        y = y_ref[...] + ybwd_ref[pl.ds(voff, SSD_Q), :] + dskip_ref[...] * xs
        y = y * _silu(z_ref[...])
        for g in range(SSD_GROUPS):
            yg = y[:, g * hp:(g + 1) * hp]
            o_ref[:, g * hp:(g + 1) * hp] = (_rms(yg) * gn_ref[:, g * hp:(g + 1) * hp]).astype(BF16)


def _ssd(xbc, z, dt_col, dt_row, cw, cb, bias_r, bias_c, alog_r, alog_c, dskip_e, gn, e16, *, l):
    n_halo = T_ALL // HALO
    per_blk = SSD_Q // HALO

    def blk(b, p, s):
        return _ssd_row_block(b, p, s)

    def out_blk(b, p, s):
        return _ssd_row_block(b, 1, jnp.where(p == 1, s, 0))

    return pl.pallas_call(
        _ssd_kernel,
        out_shape=jax.ShapeDtypeStruct((T_ALL, SSD_INNER), BF16),
        grid=(BATCH, 2, SSD_STEPS),
        in_specs=[
            pl.BlockSpec((SSD_Q, SSD_CONV_CH), lambda b, p, s: (blk(b, p, s), 0)),
            pl.BlockSpec((HALO, SSD_CONV_CH),
                         lambda b, p, s: (jnp.maximum(blk(b, p, s) * per_blk - 1, 0), 0)),
            pl.BlockSpec((HALO, SSD_CONV_CH),
                         lambda b, p, s: (jnp.minimum((blk(b, p, s) + 1) * per_blk, n_halo - 1), 0)),
            pl.BlockSpec((SSD_Q, SSD_INNER), lambda b, p, s: (out_blk(b, p, s), 0)),
            pl.BlockSpec((None, SSD_Q, SSD_HEADS), lambda b, p, s: (1 - p, blk(b, p, s), 0)),
            pl.BlockSpec((None, SSD_HEADS, SSD_Q), lambda b, p, s: (1 - p, 0, blk(b, p, s))),
            pl.BlockSpec((None, SSD_CONV, SSD_CONV_CH), lambda b, p, s: (l, 0, 0)),
            pl.BlockSpec((None, 1, SSD_CONV_CH), lambda b, p, s: (l, 0, 0)),
            pl.BlockSpec((None, None, 1, SSD_HEADS), lambda b, p, s: (l, 1 - p, 0, 0)),
            pl.BlockSpec((None, None, SSD_HEADS, 1), lambda b, p, s: (l, 1 - p, 0, 0)),
            pl.BlockSpec((None, None, 1, SSD_HEADS), lambda b, p, s: (l, 1 - p, 0, 0)),
            pl.BlockSpec((None, None, SSD_HEADS, 1), lambda b, p, s: (l, 1 - p, 0, 0)),
            pl.BlockSpec((None, 1, SSD_INNER), lambda b, p, s: (l, 0, 0)),
            pl.BlockSpec((None, 1, SSD_INNER), lambda b, p, s: (l, 0, 0)),
            pl.BlockSpec((SSD_HEADS, SSD_INNER), lambda b, p, s: (0, 0)),
        ],
        out_specs=pl.BlockSpec((SSD_Q, SSD_INNER), lambda b, p, s: (out_blk(b, p, s), 0)),
        scratch_shapes=[
            pltpu.VMEM((SSD_Q + 2 * HALO, SSD_CONV_CH), F32),
            pltpu.VMEM((SSD_STATE, SSD_INNER), F32),
            pltpu.VMEM((SSD_STEPS * SSD_Q, SSD_INNER), F32),
            pltpu.VMEM((SSD_Q, SSD_INNER), F32),
        ],
        compiler_params=_cparams(("parallel", "arbitrary", "arbitrary")),
        name="ssd",
    )(xbc, xbc, xbc, z, dt_col, dt_row, cw, cb, bias_r, bias_c, alog_r, alog_c, dskip_e, gn, e16)


def _mla_prep_kernel(cq_ref, ckv_ref, sm_ref, tab_ref, gq_ref, gkv_ref, wq_ref, wk_ref, wv_ref,
                     q_ref, k_ref, v_ref):
    cq_tab = tab_ref[:, 0:LANE]
    sq_tab = tab_ref[:, LANE:2 * LANE]
    ck_tab = tab_ref[:, 2 * LANE:3 * LANE]
    sk_tab = tab_ref[:, 3 * LANE:4 * LANE]
    nq = (_rms(cq_ref[...]) * gq_ref[...]).astype(BF16)
    q = _dot(nq, wq_ref[...])
    for h in range(MLA_HEADS):
        qh = q[:, h * LANE:(h + 1) * LANE]
        qh = qh * cq_tab + pltpu.roll(qh, LANE - MLA_ROPE, 1) * sq_tab
        q_ref[:, h * LANE:(h + 1) * LANE] = qh.astype(BF16)
    nkv = _rms(ckv_ref[...]) * gkv_ref[...]
    sm = sm_ref[...]
    kr = sm * ck_tab + pltpu.roll(sm, LANE - MLA_ROPE, 1) * sk_tab
    lhs = jnp.concatenate([nkv, kr], axis=1).astype(BF16)
    k_ref[...] = _dot(lhs, wk_ref[...]).T.astype(BF16)
    v = _dot(lhs[:, :MLA_KV_RANK], wv_ref[...])
    lane = lax.broadcasted_iota(jnp.int32, v.shape, 1)
    v_ref[...] = jnp.where(lane % LANE == MLA_V, 1.0, v).astype(BF16)


def _mla_prep(cq, ckv, small, tab, gq, gkv, wq, wk, wv, *, l):
    tm = PREP_TM
    tiles_per_seq = SEQ // tm

    def tab_idx(i):
        return jnp.where(i < T_X // tm, i % tiles_per_seq, tiles_per_seq)

    return pl.pallas_call(
        _mla_prep_kernel,
        out_shape=[jax.ShapeDtypeStruct((T_ALL, MLA_HEADS * LANE), BF16),
                   jax.ShapeDtypeStruct((MLA_HEADS * LANE, T_ALL), BF16),
                   jax.ShapeDtypeStruct((T_ALL, MLA_HEADS * LANE), BF16)],
        grid=(T_ALL // tm,),
        in_specs=[
            pl.BlockSpec((tm, MLA_Q_RANK), lambda i: (i, 0)),
            pl.BlockSpec((tm, MLA_KV_RANK), lambda i: (i, 0)),
            pl.BlockSpec((tm, LANE), lambda i: (i, 0)),
            pl.BlockSpec((tm, 4 * LANE), lambda i: (tab_idx(i), 0)),
            pl.BlockSpec((None, 1, MLA_Q_RANK), lambda i: (l, 0, 0)),
            pl.BlockSpec((None, 1, MLA_KV_RANK), lambda i: (l, 0, 0)),
            pl.BlockSpec((None, MLA_Q_RANK, MLA_HEADS * LANE), lambda i: (l, 0, 0)),
            pl.BlockSpec((None, MLA_KV_RANK + LANE, MLA_HEADS * LANE), lambda i: (l, 0, 0)),
            pl.BlockSpec((None, MLA_KV_RANK, MLA_HEADS * LANE), lambda i: (l, 0, 0)),
        ],
        out_specs=[pl.BlockSpec((tm, MLA_HEADS * LANE), lambda i: (i, 0)),
                   pl.BlockSpec((MLA_HEADS * LANE, tm), lambda i: (0, i)),
                   pl.BlockSpec((tm, MLA_HEADS * LANE), lambda i: (i, 0))],
        compiler_params=_cparams(("parallel",)),
        name="mla_prep",
    )(cq, ckv, small, tab, gq, gkv, wq, wk, wv)


def _mla_attn_kernel(*refs, n_kv):
    q_ref = refs[0]
    k_refs = refs[1:1 + n_kv]
    v_refs = refs[1 + n_kv:1 + 2 * n_kv]
    o_ref = refs[1 + 2 * n_kv]
    outs = []
    for h in range(2):
        hl = slice(h * LANE, (h + 1) * LANE)
        q = q_ref[:, hl]
        m, acc = None, None
        for k_ref, v_ref in zip(k_refs, v_refs):
            n_keys = k_ref.shape[1]
            for c0 in range(0, n_keys, ATT_TK):
                c1 = min(c0 + ATT_TK, n_keys)
                s = _dot(q, k_ref[hl, c0:c1])
                m_new = s.max(axis=-1, keepdims=True)
                if m is not None:
                    m_new = jnp.maximum(m, m_new)
                    acc = acc * jnp.exp2(m - m_new)
                pv = _dot(jnp.exp2(s - m_new).astype(BF16), v_ref[c0:c1, hl])
                acc = pv if acc is None else acc + pv
                m = m_new
        outs.append(acc / acc[:, MLA_V:MLA_V + 1])
    lane = lax.broadcasted_iota(jnp.int32, outs[0].shape, 1)
    o_ref[...] = jnp.where(lane < MLA_V, outs[0], pltpu.roll(outs[1], MLA_V, 1))


def _mla_attn_latent(q, k, v):
    tq = ATT_TQ
    qt = SEQ // tq
    ctx_blk0 = T_X // CTX_LEN
    return pl.pallas_call(
        functools.partial(_mla_attn_kernel, n_kv=2),
        out_shape=jax.ShapeDtypeStruct((T_X, MLA_HEADS * MLA_V), F32),
        grid=(BATCH, MLA_HEADS // 2, qt),
        in_specs=[
            pl.BlockSpec((tq, 2 * LANE), lambda b, h, t: (b * qt + t, h)),
            pl.BlockSpec((2 * LANE, SEQ), lambda b, h, t: (h, b)),
            pl.BlockSpec((2 * LANE, CTX_LEN), lambda b, h, t: (h, ctx_blk0 + b)),
            pl.BlockSpec((SEQ, 2 * LANE), lambda b, h, t: (b, h)),
            pl.BlockSpec((CTX_LEN, 2 * LANE), lambda b, h, t: (ctx_blk0 + b, h)),
        ],
        out_specs=pl.BlockSpec((tq, LANE), lambda b, h, t: (b * qt + t, h)),
        compiler_params=_cparams(("parallel", "parallel", "arbitrary")),
        name="mla_attn_latent",
    )(q, k, k, v, v)


def _mla_attn_ctx(q, k, v):
    ctx_blk0 = T_X // CTX_LEN
    return pl.pallas_call(
        functools.partial(_mla_attn_kernel, n_kv=1),
        out_shape=jax.ShapeDtypeStruct((T_C, MLA_HEADS * MLA_V), F32),
        grid=(BATCH, MLA_HEADS // 2),
        in_specs=[
            pl.BlockSpec((CTX_LEN, 2 * LANE), lambda b, h: (ctx_blk0 + b, h)),
            pl.BlockSpec((2 * LANE, CTX_LEN), lambda b, h: (h, ctx_blk0 + b)),
            pl.BlockSpec((CTX_LEN, 2 * LANE), lambda b, h: (ctx_blk0 + b, h)),
        ],
        out_specs=pl.BlockSpec((CTX_LEN, LANE), lambda b, h: (b, h)),
        compiler_params=_cparams(("parallel", "parallel")),
        name="mla_attn_ctx",
    )(q, k, v)


def _swa_core(q, keys, vals, masks, sinks, o_ref):
    w = q.shape[0]
    lane = lax.broadcasted_iota(jnp.int32, (w, LANE), 1)
    low = lane < SWA_HEADDIM
    rep = SWA_HEADS // SWA_KV_HEADS
    for g in range(SWA_KV_HEADS):
        parts = []
        for j in range(rep // 2):
            qp = q[:, (g * rep // 2 + j) * LANE:(g * rep // 2 + j + 1) * LANE]
            parts.append(jnp.where(low, qp, 0.0))
            parts.append(jnp.where(low, 0.0, qp))
        lhs = jnp.concatenate(parts, axis=0).astype(BF16)
        sink = jnp.concatenate([jnp.full((w, 1), sinks[g * rep + r] * LOG2E, F32) for r in range(rep)],
                               axis=0)
        s = []
        for kb, mask in zip(keys, masks):
            sb = _dot_t(lhs, kb[:, g * LANE:(g + 1) * LANE].astype(BF16))
            if mask is not None:
                sb = jnp.where(jnp.concatenate([mask] * rep, axis=0), sb, NEG)
            s.append(sb)
        m = sink
        for sb in s:
            m = jnp.maximum(m, sb.max(axis=-1, keepdims=True))
        den = jnp.exp2(sink - m)
        o = None
        for sb, vb in zip(s, vals):
            p = jnp.exp2(sb - m)
            den = den + p.sum(axis=-1, keepdims=True)
            pv = _dot(p.astype(BF16), vb[:, g * LANE:(g + 1) * LANE].astype(BF16))
            o = pv if o is None else o + pv
        o = o / den
        for j in range(rep // 2):
            o_ref[:, (g * rep // 2 + j) * LANE:(g * rep // 2 + j + 1) * LANE] = jnp.where(
                low, o[2 * j * w:(2 * j + 1) * w], o[(2 * j + 1) * w:(2 * j + 2) * w])


def _rope_rows(x_ref, tab_ref, n, reps):
    cos = jnp.concatenate([tab_ref[:, :LANE]] * reps, axis=1)
    sin = jnp.concatenate([tab_ref[:, LANE:]] * reps, axis=1)
    return x_ref[:, :n] * cos + x_ref[:, n:] * sin


def _swa_latent_kernel(sink_ref, q_ref, kp_ref, kc_ref, kn_ref, kx_ref, vp_ref, vc_ref, vn_ref,
                       vx_ref, tq_ref, tp_ref, tn_ref, o_ref, *, l):
    n = pl.program_id(1)
    q = _rope_rows(q_ref, tq_ref, N_SWA_Q, N_SWA_Q // LANE) * SWA_QSCALE
    keys = [kx_ref[:, :N_SWA_KV2],
            _rope_rows(kp_ref, tp_ref, N_SWA_KV2, N_SWA_KV2 // LANE),
            _rope_rows(kc_ref, tq_ref, N_SWA_KV2, N_SWA_KV2 // LANE),
            _rope_rows(kn_ref, tn_ref, N_SWA_KV2, N_SWA_KV2 // LANE)]
    vals = [vx_ref[...], vp_ref[...], vc_ref[...], vn_ref[...]]
    qi = lax.broadcasted_iota(jnp.int32, (SWA_BLOCK, SWA_BLOCK), 0)
    kj = lax.broadcasted_iota(jnp.int32, (SWA_BLOCK, SWA_BLOCK), 1)
    prev_ok = jnp.logical_and(kj - qi >= SWA_BLOCK - SWA_WINDOW, n > 0)
    next_ok = jnp.logical_and(kj - qi <= SWA_WINDOW - SWA_BLOCK, n < SEQ // SWA_BLOCK - 1)
    sinks = [sink_ref[l, h] for h in range(SWA_HEADS)]
    _swa_core(q, keys, vals, [None, prev_ok, None, next_ok], sinks, o_ref)


def _swa_ctx_kernel(sink_ref, q_ref, kx_ref, vx_ref, o_ref, *, l):
    q = q_ref[:, :N_SWA_Q] * SWA_QSCALE
    sinks = [sink_ref[l, h] for h in range(SWA_HEADS)]
    _swa_core(q, [kx_ref[:, :N_SWA_KV2]], [vx_ref[...]], [None], sinks, o_ref)


def _swa_latent(sink, qs2, ks2, vs2, tab, *, l):
    nb = SEQ // SWA_BLOCK
    ctx_blk0 = T_X // CTX_LEN
    w = SWA_BLOCK

    def prev(b, n):
        return b * nb + jnp.maximum(n - 1, 0)

    def nxt(b, n):
        return b * nb + jnp.minimum(n + 1, nb - 1)

    return pl.pallas_call(
        functools.partial(_swa_latent_kernel, l=l),
        out_shape=jax.ShapeDtypeStruct((T_X, N_SWA_Q), F32),
        grid=(BATCH, nb),
        in_specs=[
            pl.BlockSpec(memory_space=pltpu.SMEM),
            pl.BlockSpec((w, 2 * N_SWA_Q), lambda b, n: (b * nb + n, 0)),
            pl.BlockSpec((w, 2 * N_SWA_KV2), lambda b, n: (prev(b, n), 0)),
            pl.BlockSpec((w, 2 * N_SWA_KV2), lambda b, n: (b * nb + n, 0)),
            pl.BlockSpec((w, 2 * N_SWA_KV2), lambda b, n: (nxt(b, n), 0)),
            pl.BlockSpec((CTX_LEN, 2 * N_SWA_KV2), lambda b, n: (ctx_blk0 + b, 0)),
            pl.BlockSpec((w, N_SWA_KV2), lambda b, n: (prev(b, n), 0)),
            pl.BlockSpec((w, N_SWA_KV2), lambda b, n: (b * nb + n, 0)),
            pl.BlockSpec((w, N_SWA_KV2), lambda b, n: (nxt(b, n), 0)),
            pl.BlockSpec((CTX_LEN, N_SWA_KV2), lambda b, n: (ctx_blk0 + b, 0)),
            pl.BlockSpec((w, 2 * LANE), lambda b, n: (n, 0)),
            pl.BlockSpec((w, 2 * LANE), lambda b, n: (jnp.maximum(n - 1, 0), 0)),
            pl.BlockSpec((w, 2 * LANE), lambda b, n: (jnp.minimum(n + 1, nb - 1), 0)),
        ],
        out_specs=pl.BlockSpec((w, N_SWA_Q), lambda b, n: (b * nb + n, 0)),
        compiler_params=_cparams(("parallel", "arbitrary")),
        name="swa_latent",
    )(sink, qs2, ks2, ks2, ks2, ks2, vs2, vs2, vs2, vs2, tab, tab, tab)


def _swa_ctx(sink, qs2, ks2, vs2, *, l):
    w = SWA_BLOCK
    blk0 = T_X // w
    per_b = CTX_LEN // w
    ctx_blk0 = T_X // CTX_LEN
    return pl.pallas_call(
        functools.partial(_swa_ctx_kernel, l=l),
        out_shape=jax.ShapeDtypeStruct((T_C, N_SWA_Q), F32),
        grid=(BATCH, per_b),
        in_specs=[
            pl.BlockSpec(memory_space=pltpu.SMEM),
            pl.BlockSpec((w, 2 * N_SWA_Q), lambda b, n: (blk0 + b * per_b + n, 0)),
            pl.BlockSpec((CTX_LEN, 2 * N_SWA_KV2), lambda b, n: (ctx_blk0 + b, 0)),
            pl.BlockSpec((CTX_LEN, N_SWA_KV2), lambda b, n: (ctx_blk0 + b, 0)),
        ],
        out_specs=pl.BlockSpec((w, N_SWA_Q), lambda b, n: (b * per_b + n, 0)),
        compiler_params=_cparams(("parallel", "arbitrary")),
        name="swa_ctx",
    )(sink, qs2, ks2, vs2)


def _outproj_kernel(h_ref, mod_ref, ssd_ref, mla_ref, mla_c_ref, swa_ref, swa_c_ref, gm_ref, gs_ref,
                    w_ref, o_ref, *, n_lat):
    n_mla = MLA_HEADS * MLA_V
    is_lat = pl.program_id(0) < n_lat
    acc = _dot(ssd_ref[...], w_ref[0:SSD_INNER, :])
    mla = jnp.where(is_lat, mla_ref[...], mla_c_ref[...])
    mla = (_rms(mla) * gm_ref[...]).astype(BF16)
    acc += _dot(mla, w_ref[SSD_INNER:SSD_INNER + n_mla, :])
    swa = jnp.where(is_lat, swa_ref[...], swa_c_ref[...])
    swa = (_rms(swa) * gs_ref[...]).astype(BF16)
    acc += _dot(swa, w_ref[SSD_INNER + n_mla:, :])
    o_ref[...] = h_ref[...] + mod_ref[5:6, :] * acc


def _outproj(h, mod_l, ssd, mla, mla_c, swa, swa_c, gm, gs, w_out, *, l, rows):
    tm = OUT_TM
    n_mla = MLA_HEADS * MLA_V
    n_lat = T_X // tm

    def lat(i):
        return (jnp.minimum(i, n_lat - 1), 0)

    def ctx(i):
        return (jnp.maximum(i - n_lat, 0), 0)

    return pl.pallas_call(
        functools.partial(_outproj_kernel, n_lat=n_lat),
        out_shape=jax.ShapeDtypeStruct((rows, D_MODEL), F32),
        grid=(rows // tm,),
        in_specs=[
            pl.BlockSpec((tm, D_MODEL), lambda i: (i, 0)),
            pl.BlockSpec((None, N_MOD, D_MODEL), lambda i: (_group_of_tile(i, tm), 0, 0)),
            pl.BlockSpec((tm, SSD_INNER), lambda i: (i, 0)),
            pl.BlockSpec((tm, n_mla), lat),
            pl.BlockSpec((tm, n_mla), ctx),
            pl.BlockSpec((tm, N_SWA_Q), lat),
            pl.BlockSpec((tm, N_SWA_Q), ctx),
            pl.BlockSpec((None, 1, n_mla), lambda i: (l, 0, 0)),
            pl.BlockSpec((None, 1, N_SWA_Q), lambda i: (l, 0, 0)),
            pl.BlockSpec((None, D_MIX, D_MODEL), lambda i: (l, 0, 0)),
        ],
        out_specs=pl.BlockSpec((tm, D_MODEL), lambda i: (i, 0)),
        compiler_params=_cparams(("parallel",)),
        name="outproj",
    )(h, mod_l, ssd, mla, mla_c, swa, swa_c, gm, gs, w_out)


def _rot_src_sign(dim):
    q = dim // 4
    src = np.concatenate([np.arange(q, 2 * q), np.arange(0, q),
                          np.arange(3 * q, 4 * q), np.arange(2 * q, 3 * q)])
    sign = np.concatenate([-np.ones(q), np.ones(q), -np.ones(q), np.ones(q)])
    return src, sign


def _permute_cols(w, idx, sgn):
    pieces, pending, start, n = [], [], 0, len(idx)

    def flush():
        if pending:
            cols = np.concatenate(pending)
            pieces.append(w[..., idx[cols]] * sgn[cols])
            pending.clear()

    while start < n:
        end = start + 1
        while end < n and idx[end] == idx[end - 1] + 1 and sgn[end] == sgn[start] == 1:
            end += 1
        if end - start >= LANE:
            flush()
            pieces.append(w[..., int(idx[start]):int(idx[end - 1]) + 1])
        else:
            pending.append(np.arange(start, end))
        start = end
    flush()
    return jnp.concatenate(pieces, axis=-1)


def _w_in_layout():
    offs = np.concatenate([[0], np.cumsum(IN_SIZES)])
    o_z, o_xbc, o_dt, o_cq, o_ckv, o_kr, o_qs, o_ks, o_vs = offs[:-1]
    idx, sgn = [], []
    ident = (np.arange(SWA_HEADDIM), np.ones(SWA_HEADDIM))

    def plain(start, n):
        idx.append(np.arange(start, start + n))
        sgn.append(np.ones(n))

    def heads(start, n_heads, dim, src_sign, copies):
        src, sign = src_sign
        for h in range(n_heads):
            for _ in range(copies):
                idx.append(start + h * dim + src)
                sgn.append(sign)

    plain(o_z, SSD_INNER)
    plain(o_xbc, SSD_CONV_CH)
    plain(o_cq, MLA_Q_RANK)
    plain(o_ckv, MLA_KV_RANK)
    plain(o_qs, N_SWA_Q)
    heads(o_qs, SWA_HEADS, SWA_HEADDIM, _rot_src_sign(SWA_HEADDIM), 1)
    heads(o_ks, SWA_KV_HEADS, SWA_HEADDIM, ident, 2)
    heads(o_ks, SWA_KV_HEADS, SWA_HEADDIM, _rot_src_sign(SWA_HEADDIM), 2)
    heads(o_vs, SWA_KV_HEADS, SWA_HEADDIM, ident, 2)
    plain(o_dt, 2 * SSD_HEADS)
    plain(o_kr, MLA_ROPE)
    heads(o_kr, 1, MLA_ROPE, _rot_src_sign(MLA_ROPE), 1)
    idx.append(np.zeros(MLA_ROPE, np.int64))
    sgn.append(np.zeros(MLA_ROPE))
    idx = np.concatenate(idx).astype(np.int32)
    sgn = np.concatenate(sgn).astype(np.float32)
    assert idx.shape[0] == W_IN_COLS
    return idx, sgn


def _w_uq_layout():
    hd = MLA_NOPE + MLA_ROPE
    src, sign = _rot_src_sign(MLA_ROPE)
    idx, sgn = [], []
    for h in range(MLA_HEADS):
        idx.append(h * hd + np.arange(hd))
        sgn.append(np.ones(hd))
        idx.append(h * hd + MLA_NOPE + src)
        sgn.append(sign)
    return np.concatenate(idx).astype(np.int32), np.concatenate(sgn).astype(np.float32)


def _w_ukv_layout():
    hd = MLA_NOPE + MLA_V
    kidx, vidx, sgn = [], [], []
    for h in range(MLA_HEADS):
        kidx.append(h * hd + np.arange(MLA_NOPE))
        vidx.append(h * hd + MLA_NOPE + np.arange(MLA_V))
        sgn.append(np.ones(MLA_NOPE))
        kidx.append(np.zeros(LANE - MLA_NOPE, np.int64))
        vidx.append(np.zeros(LANE - MLA_V, np.int64))
        sgn.append(np.zeros(LANE - MLA_NOPE))
    return (np.concatenate(kidx).astype(np.int32), np.concatenate(vidx).astype(np.int32),
            np.concatenate(sgn).astype(np.float32))


def _k_rope_expand():
    e = np.zeros((LANE, MLA_HEADS * LANE), np.float32)
    for h in range(MLA_HEADS):
        for i in range(MLA_ROPE):
            e[MLA_ROPE + i, h * LANE + MLA_NOPE + i] = 1.0
    return e


def _head_expand():
    e = np.zeros((SSD_HEADS, SSD_INNER), np.float32)
    for h in range(SSD_HEADS):
        e[h, h * SSD_HEADDIM:(h + 1) * SSD_HEADDIM] = 1.0
    return e


def _rope_angles(dim):
    pos = np.arange(SEQ)
    quarter = dim // 4
    inv_freq = ROPE_THETA ** (-jnp.arange(quarter, dtype=F32) / quarter)
    ang_r = jnp.asarray(pos // GRID_W, F32)[:, None] * inv_freq[None, :]
    ang_c = jnp.asarray(pos % GRID_W, F32)[:, None] * inv_freq[None, :]
    ang = jnp.concatenate([ang_r, ang_r, ang_c, ang_c], axis=-1)
    return jnp.cos(ang), jnp.sin(ang)


def _mla_rope_table():
    cos, sin = _rope_angles(MLA_ROPE)
    scale = (MLA_NOPE + MLA_ROPE) ** -0.5 * LOG2E
    n = SEQ + PREP_TM
    ident = jnp.ones((PREP_TM, MLA_ROPE), F32)
    zeros = jnp.zeros((PREP_TM, MLA_ROPE), F32)
    cos = jnp.concatenate([cos, ident], axis=0)
    sin = jnp.concatenate([sin, zeros], axis=0)
    z32 = jnp.zeros((n, MLA_ROPE), F32)
    cq = jnp.concatenate([jnp.ones((n, MLA_NOPE), F32), cos, z32], axis=1) * scale
    sq = jnp.concatenate([jnp.zeros((n, MLA_NOPE), F32), sin, z32], axis=1) * scale
    ck = jnp.concatenate([z32, cos, z32, z32], axis=1)
    sk = jnp.concatenate([z32, sin, z32, z32], axis=1)
    return jnp.concatenate([cq, sq, ck, sk], axis=1)


def _swa_rope_table():
    cos, sin = _rope_angles(SWA_HEADDIM)
    return jnp.concatenate([cos, cos, sin, sin], axis=1)


def kernel(x, c, ctx, c_ctx, w_mod, b_mod, norm_g, ffn_w_gate, ffn_w_up, ffn_w_down,
           w_in, w_out, ssd_conv_w, ssd_conv_b, ssd_dt_bias, ssd_a_log, ssd_d, ssd_norm,
           mla_q_norm, mla_w_uq, mla_kv_norm, mla_w_ukv, mla_out_norm, swa_sink,
           swa_out_norm, final_norm):
    idx, sgn = _w_in_layout()
    w_in_p = _permute_cols(w_in, idx, sgn).astype(BF16)
    qidx, qsgn = _w_uq_layout()
    wq = _permute_cols(mla_w_uq, qidx, qsgn).astype(BF16)
    kidx, vidx, kvsgn = _w_ukv_layout()
    k_expand = jnp.broadcast_to(jnp.asarray(_k_rope_expand()), (DEPTH, LANE, MLA_HEADS * LANE))
    wk = jnp.concatenate([_permute_cols(mla_w_ukv, kidx, kvsgn), k_expand], axis=1).astype(BF16)
    wv = _permute_cols(mla_w_ukv, vidx, kvsgn).astype(BF16)
    wg = ffn_w_gate.astype(BF16)
    wu = ffn_w_up.astype(BF16)
    wd = ffn_w_down.astype(BF16)
    wo = w_out.astype(BF16)
    mla_tab = _mla_rope_table()
    swa_tab = _swa_rope_table()
    e16 = jnp.asarray(_head_expand(), BF16)
    dskip_e = jnp.repeat(ssd_d, SSD_HEADDIM, axis=-1).reshape(DEPTH, 1, SSD_INNER)
    ssd_gn = ssd_norm.reshape(DEPTH, 1, SSD_INNER)
    conv_b = ssd_conv_b.reshape(DEPTH, 1, SSD_CONV_CH)
    bias_r = ssd_dt_bias.reshape(DEPTH, 2, 1, SSD_HEADS)
    bias_c = ssd_dt_bias.reshape(DEPTH, 2, SSD_HEADS, 1)
    alog_r = ssd_a_log.reshape(DEPTH, 2, 1, SSD_HEADS)
    alog_c = ssd_a_log.reshape(DEPTH, 2, SSD_HEADS, 1)
    gq = mla_q_norm.reshape(DEPTH, 1, MLA_Q_RANK)
    gkv = mla_kv_norm.reshape(DEPTH, 1, MLA_KV_RANK)
    gm = mla_out_norm.reshape(DEPTH, 1, -1)
    gs = swa_out_norm.reshape(DEPTH, 1, -1)
    fin = final_norm.reshape(1, D_MODEL)

    c_all = jnp.concatenate([c, c_ctx[None, :], jnp.zeros((GROUP_PAD - N_GROUPS, D_MODEL), F32)], axis=0)
    mod_all = _modulation(c_all, w_mod, b_mod).reshape(DEPTH, GROUP_PAD, N_MOD, D_MODEL)

    h = x.reshape(T_X, D_MODEL)
    h_ctx = ctx.reshape(T_C, D_MODEL)

    for l in range(DEPTH):
        last = l == DEPTH - 1
        mod_l = mod_all[l]
        h = _ffn(h, h_ctx, mod_l, norm_g[l, 0].reshape(1, D_MODEL), wg, wu, wd, fin,
                 l=l, k=0, base=0, rows=T_ALL)
        h_ctx = None
        z, xbc, cq, ckv, qs2, ks2, vs2, small = _inproj(h, mod_l, norm_g[l, 1].reshape(1, D_MODEL),
                                                       w_in_p, l=l)
        dt2 = small[:, :2 * SSD_HEADS].reshape(T_ALL, 2, SSD_HEADS)
        dt_col = jnp.transpose(dt2, (1, 0, 2))
        dt_row = jnp.transpose(dt2, (1, 2, 0))
        ssd_o = _ssd(xbc, z, dt_col, dt_row, ssd_conv_w, conv_b, bias_r, bias_c, alog_r, alog_c,
                     dskip_e, ssd_gn, e16, l=l)
        q_m, k_m, v_m = _mla_prep(cq, ckv, small, mla_tab, gq, gkv, wq, wk, wv, l=l)
        mla_x = _mla_attn_latent(q_m, k_m, v_m)
        swa_x = _swa_latent(swa_sink, qs2, ks2, vs2, swa_tab, l=l)
        if last:
            rows, mla_c, swa_c = T_X, mla_x, swa_x
        else:
            rows = T_ALL
            mla_c = _mla_attn_ctx(q_m, k_m, v_m)
            swa_c = _swa_ctx(swa_sink, qs2, ks2, vs2, l=l)
        h = _outproj(h, mod_l, ssd_o, mla_x, mla_c, swa_x, swa_c, gm, gs, wo, l=l, rows=rows)
        h = _ffn(h, None, mod_l, norm_g[l, 2].reshape(1, D_MODEL), wg, wu, wd, fin,
                 l=l, k=1, base=6, rows=rows, final_norm=last)
    return h.reshape(BATCH, SEQ, D_MODEL)
```

```python
import functools
import math

import jax
import jax.numpy as jnp
import numpy as np
from jax import lax
from jax.experimental import pallas as pl
from jax.experimental.pallas import tpu as pltpu

F32 = jnp.float32
BF16 = jnp.bfloat16
HIGHEST = lax.Precision.HIGHEST

D_MODEL = 2048
BATCH = 4
SEQ = 4096
DEPTH = 4
GRID_W = 64
CTX_LEN = 256
EPS = 1e-6
ROPE_THETA = 10000.0
N_MOD = 9
D_FF = 5632

SSD_HEADS = 16
SSD_HEADDIM = 64
SSD_INNER = SSD_HEADS * SSD_HEADDIM
SSD_GROUPS = 2
SSD_STATE = 128
SSD_CONV = 5
SSD_CONV_CH = SSD_INNER + 2 * SSD_GROUPS * SSD_STATE

MLA_HEADS = 8
MLA_Q_RANK = 384
MLA_KV_RANK = 256
MLA_NOPE = 64
MLA_ROPE = 32
MLA_V = 64

SWA_HEADS = 8
SWA_KV_HEADS = 2
SWA_HEADDIM = 64
SWA_WINDOW = 128
SWA_BLOCK = 128

D_MIX = SSD_INNER + MLA_HEADS * MLA_V + SWA_HEADS * SWA_HEADDIM
IN_SIZES = (SSD_INNER, SSD_CONV_CH, 2 * SSD_HEADS, MLA_Q_RANK, MLA_KV_RANK, MLA_ROPE,
            SWA_HEADS * SWA_HEADDIM, SWA_KV_HEADS * SWA_HEADDIM, SWA_KV_HEADS * SWA_HEADDIM)

T_X = BATCH * SEQ
T_C = BATCH * CTX_LEN
T_ALL = T_X + T_C
N_GROUPS = BATCH + 1
GROUP_PAD = 8

LANE = 128
SUBLANE = 8
VMEM_LIMIT = 56 * 1024 * 1024
NEG = -1e30
LOG2E = math.log2(math.e)

FFN_TM = 512
FFN_TF = 512
PROJ_TM = 256
OUT_TM = 512
PREP_TM = 512
ATT_TQ = 512
ATT_TK = 2048
MOD_TN = 1024

SSD_Q = 128
SEQ_CHUNKS = SEQ // SSD_Q
CTX_CHUNKS = CTX_LEN // SSD_Q
SSD_STEPS = SEQ_CHUNKS + CTX_CHUNKS
HALO = SUBLANE
N_SWA_Q = SWA_HEADS * SWA_HEADDIM
N_SWA_KV2 = 2 * SWA_KV_HEADS * SWA_HEADDIM
SWA_QSCALE = SWA_HEADDIM ** -0.5 * LOG2E

SEG_Z = (0, 1024)
SEG_XBC = (1024, 2560)
SEG_CQ = (2560, 2944)
SEG_CKV = (2944, 3200)
SEG_QS = (3200, 4224)
SEG_KS = (4224, 4736)
SEG_VS = (4736, 4992)
SEG_SMALL = (4992, 5120)
W_IN_COLS = 5120
IN_SEGS = (SEG_Z, SEG_XBC, SEG_CQ, SEG_CKV, SEG_QS, SEG_KS, SEG_VS, SEG_SMALL)


def _cparams(sem):
    return pltpu.CompilerParams(dimension_semantics=sem, vmem_limit_bytes=VMEM_LIMIT)


def _group_of_tile(i, tm):
    return jnp.minimum((i * tm) // SEQ, BATCH)


def _rms(x):
    return x * lax.rsqrt(jnp.mean(x * x, axis=-1, keepdims=True) + EPS)


def _silu(x):
    return x * jax.nn.sigmoid(x)


def _softplus(x):
    return jnp.maximum(x, 0.0) + jnp.log1p(jnp.exp(-jnp.abs(x)))


def _dot(a, b):
    return jnp.dot(a, b, preferred_element_type=F32)


def _dot_t(a, b):
    return lax.dot_general(a, b, (((1,), (1,)), ((), ())), preferred_element_type=F32)


def _mod_kernel(c_ref, w_ref, b_ref, o_ref):
    sc = _silu(c_ref[...]).astype(BF16)
    o_ref[...] = _dot(sc, w_ref[...].astype(BF16)) + b_ref[...]


def _modulation(c_all, w_mod, b_mod):
    n = N_MOD * D_MODEL
    return pl.pallas_call(
        _mod_kernel,
        out_shape=jax.ShapeDtypeStruct((DEPTH, GROUP_PAD, n), F32),
        grid=(DEPTH, n // MOD_TN),
        in_specs=[
            pl.BlockSpec((GROUP_PAD, D_MODEL), lambda l, j: (0, 0)),
            pl.BlockSpec((None, D_MODEL, MOD_TN), lambda l, j: (l, 0, j)),
            pl.BlockSpec((None, 1, MOD_TN), lambda l, j: (l, 0, j)),
        ],
        out_specs=pl.BlockSpec((None, GROUP_PAD, MOD_TN), lambda l, j: (l, 0, j)),
        compiler_params=_cparams(("parallel", "parallel")),
        name="modulation",
    )(c_all, w_mod, b_mod.reshape(DEPTH, 1, n))


def _norm_modulate(x, g_ref, mod_ref, base):
    y = _rms(x) * g_ref[...]
    return (y * (1.0 + mod_ref[base + 1:base + 2, :]) + mod_ref[base:base + 1, :]).astype(BF16)


def _ffn_kernel(*refs, split, base, n_ff, n_lat, final_norm):
    x_ref, xc_ref = (refs[0], refs[1]) if split else (refs[0], None)
    mod_ref, g_ref, wg_ref, wu_ref, wd_ref, fin_ref, o_ref, u_ref = refs[(2 if split else 1):]
    j = pl.program_id(1)

    def residual():
        if xc_ref is None:
            return x_ref[...]
        return jnp.where(pl.program_id(0) < n_lat, x_ref[...], xc_ref[...])

    @pl.when(j == 0)
    def _():
        u_ref[...] = _norm_modulate(residual(), g_ref, mod_ref, base)
        o_ref[...] = jnp.zeros_like(o_ref)

    u = u_ref[...]
    gate = _dot(u, wg_ref[...])
    up = _dot(u, wu_ref[...])
    a = (_silu(gate) * up).astype(BF16)
    o_ref[...] += _dot(a, wd_ref[...])

    @pl.when(j == n_ff - 1)
    def _():
        h = residual() + (0.5 * mod_ref[base + 2:base + 3, :]) * o_ref[...]
        if final_norm:
            h = _rms(h) * fin_ref[...]
        o_ref[...] = h


def _ffn(h, h_ctx, mod_l, g, wg, wu, wd, fin, *, l, k, base, rows, final_norm=False):
    n_ff = D_FF // FFN_TF
    tm = FFN_TM
    n_lat = T_X // tm
    split = h_ctx is not None
    kern = functools.partial(_ffn_kernel, split=split, base=base, n_ff=n_ff, n_lat=n_lat,
                             final_norm=final_norm)
    if split:
        x_specs = [pl.BlockSpec((tm, D_MODEL), lambda i, j: (jnp.minimum(i, n_lat - 1), 0)),
                   pl.BlockSpec((tm, D_MODEL), lambda i, j: (jnp.maximum(i - n_lat, 0), 0))]
        x_args = (h, h_ctx)
    else:
        x_specs = [pl.BlockSpec((tm, D_MODEL), lambda i, j: (i, 0))]
        x_args = (h,)
    return pl.pallas_call(
        kern,
        out_shape=jax.ShapeDtypeStruct((rows, D_MODEL), F32),
        grid=(rows // tm, n_ff),
        in_specs=x_specs + [
            pl.BlockSpec((None, N_MOD, D_MODEL), lambda i, j: (_group_of_tile(i, tm), 0, 0)),
            pl.BlockSpec((1, D_MODEL), lambda i, j: (0, 0)),
            pl.BlockSpec((None, None, D_MODEL, FFN_TF), lambda i, j: (l, k, 0, j)),
            pl.BlockSpec((None, None, D_MODEL, FFN_TF), lambda i, j: (l, k, 0, j)),
            pl.BlockSpec((None, None, FFN_TF, D_MODEL), lambda i, j: (l, k, j, 0)),
            pl.BlockSpec((1, D_MODEL), lambda i, j: (0, 0)),
        ],
        out_specs=pl.BlockSpec((tm, D_MODEL), lambda i, j: (i, 0)),
        scratch_shapes=[pltpu.VMEM((tm, D_MODEL), BF16)],
        compiler_params=_cparams(("parallel", "arbitrary")),
        name="ffn",
    )(*x_args, mod_l, g, wg, wu, wd, fin)


def _inproj_kernel(x_ref, mod_ref, g_ref, w_ref, *o_refs):
    y = _rms(x_ref[...]) * g_ref[...]
    u = (y * (1.0 + mod_ref[4:5, :]) + mod_ref[3:4, :]).astype(BF16)
    for (a, b), o_ref in zip(IN_SEGS, o_refs):
        o_ref[...] = _dot(u, w_ref[:, a:b])


def _inproj(h, mod_l, g, w_in_p, *, l):
    tm = PROJ_TM
    return pl.pallas_call(
        _inproj_kernel,
        out_shape=[jax.ShapeDtypeStruct((T_ALL, b - a), F32) for a, b in IN_SEGS],
        grid=(T_ALL // tm,),
        in_specs=[
            pl.BlockSpec((tm, D_MODEL), lambda i: (i, 0)),
            pl.BlockSpec((None, N_MOD, D_MODEL), lambda i: (_group_of_tile(i, tm), 0, 0)),
            pl.BlockSpec((1, D_MODEL), lambda i: (0, 0)),
            pl.BlockSpec((None, D_MODEL, W_IN_COLS), lambda i: (l, 0, 0),
                         pipeline_mode=pl.Buffered(1)),
        ],
        out_specs=[pl.BlockSpec((tm, b - a), lambda i: (i, 0)) for a, b in IN_SEGS],
        compiler_params=_cparams(("parallel",)),
        name="inproj",
    )(h, mod_l, g, w_in_p)


def _ssd_chunk_ids(phase, s):
    fwd = phase == 1
    is_ctx = s < CTX_CHUNKS
    cc = jnp.where(fwd, s, CTX_CHUNKS - 1 - s)
    lc = jnp.where(fwd, s - CTX_CHUNKS, SSD_STEPS - 1 - s)
    return is_ctx, cc, lc


def _ssd_row_block(b, phase, s):
    is_ctx, cc, lc = _ssd_chunk_ids(phase, s)
    return jnp.where(is_ctx, T_X // SSD_Q + b * CTX_CHUNKS + cc, b * SEQ_CHUNKS + lc)


def _ssd_kernel(xbc_ref, prev_ref, next_ref, z_ref, dtc_ref, dtr_ref, cw_ref, cb_ref,
                bias_r_ref, bias_c_ref, alog_r_ref, alog_c_ref, dskip_ref, gn_ref, e16_ref,
                o_ref, xe_ref, state_ref, ybwd_ref, y_ref, xcs_ref):
    phase = pl.program_id(1)
    s = pl.program_id(2)
    fwd = phase == 1
    is_ctx, cc, lc = _ssd_chunk_ids(phase, s)
    first = jnp.where(is_ctx, cc == 0, lc == 0)
    last = jnp.where(is_ctx, cc == CTX_CHUNKS - 1, lc == SEQ_CHUNKS - 1)
    voff = pl.multiple_of(jnp.where(is_ctx, cc, CTX_CHUNKS + lc) * SSD_Q, SSD_Q)

    @pl.when(s == 0)
    def _():
        state_ref[...] = jnp.zeros_like(state_ref)

    @pl.when(jnp.logical_not(fwd))
    def _():
        xe_ref[0:HALO, :] = jnp.where(first, 0.0, prev_ref[...])
        xe_ref[HALO:HALO + SSD_Q, :] = xbc_ref[...]
        xe_ref[HALO + SSD_Q:, :] = jnp.where(last, 0.0, next_ref[...])
        acc = cb_ref[...] + cw_ref[0:1, :] * xe_ref[HALO - 2:HALO - 2 + SSD_Q, :]
        for k in range(1, SSD_CONV):
            acc = acc + cw_ref[k:k + 1, :] * xe_ref[HALO - 2 + k:HALO - 2 + k + SSD_Q, :]
        xcs_ref[pl.ds(voff, SSD_Q), :] = _silu(acc).astype(BF16)

    xc = xcs_ref[pl.ds(voff, SSD_Q), :]
    xs = xc[:, :SSD_INNER].astype(F32)
    n_bc = SSD_GROUPS * SSD_STATE

    dt_c = _softplus(dtc_ref[...] + bias_r_ref[...])
    dt_r = _softplus(dtr_ref[...] + bias_c_ref[...])
    a_c = dt_c * -jnp.exp(alog_r_ref[...])
    a_r = dt_r * -jnp.exp(alog_c_ref[...])
    row = lax.broadcasted_iota(jnp.int32, (SSD_Q, SSD_Q), 0)
    col = lax.broadcasted_iota(jnp.int32, (SSD_Q, SSD_Q), 1)
    d = (row - col) * jnp.where(fwd, 1, -1)
    causal = d >= 0
    s_c = jnp.dot(causal.astype(F32), a_c, precision=HIGHEST, preferred_element_type=F32)
    s_r = jnp.dot(a_r, (d <= 0).astype(F32), precision=HIGHEST, preferred_element_type=F32)

    e16 = e16_ref[...]
    rem = jnp.concatenate([dt_c, s_c], axis=0)
    expanded = None
    for _ in range(3):
        term = rem.astype(BF16)
        rem = rem - term.astype(F32)
        part = _dot(term, e16)
        expanded = part if expanded is None else expanded + part
    dt_e = expanded[:SSD_Q]
    s_e = expanded[SSD_Q:]
    tot_e = jnp.where(fwd, s_e[SSD_Q - 1:SSD_Q, :], s_e[0:1, :])
    xdt = xs * dt_e
    xdt_end = (xdt * jnp.exp(tot_e - s_e)).astype(BF16)
    xdt16 = xdt.astype(BF16)
    exps_e = jnp.exp(s_e)
    dec_e = jnp.exp(tot_e)
    lane = lax.broadcasted_iota(jnp.int32, (SSD_Q, LANE), 1)

    hp = SSD_INNER // SSD_GROUPS
    heads_per_group = SSD_HEADS // SSD_GROUPS
    for g in range(SSD_GROUPS):
        bg = xc[:, SSD_INNER + g * SSD_STATE:SSD_INNER + (g + 1) * SSD_STATE]
        cg = xc[:, SSD_INNER + n_bc + g * SSD_STATE:SSD_INNER + n_bc + (g + 1) * SSD_STATE]
        bt16 = bg.astype(F32).T.astype(BF16)
        cg16 = cg
        cb = _dot(cg16, bt16)
        st = state_ref[:, g * hp:(g + 1) * hp]
        y_off = _dot(cg16, st.astype(BF16)) * exps_e[:, g * hp:(g + 1) * hp]
        state_ref[:, g * hp:(g + 1) * hp] = (dec_e[:, g * hp:(g + 1) * hp] * st
                                             + _dot(bt16, xdt_end[:, g * hp:(g + 1) * hp]))
        for j in range(heads_per_group // 2):
            h0 = g * heads_per_group + 2 * j
            m = []
            for h in (h0, h0 + 1):
                seg = s_c[:, h:h + 1] - s_r[h:h + 1, :]
                m.append(cb * jnp.exp(jnp.where(causal, seg, NEG)))
            m = jnp.concatenate(m, axis=0).astype(BF16)
            lo = h0 * SSD_HEADDIM
            yp = _dot(m, xdt16[:, lo:lo + LANE])
            y_ref[:, lo:lo + LANE] = (jnp.where(lane < SSD_HEADDIM, yp[:SSD_Q], yp[SSD_Q:])
                                      + y_off[:, lo - g * hp:lo - g * hp + LANE])

    @pl.when(jnp.logical_not(fwd))
    def _():
        ybwd_ref[pl.ds(voff, SSD_Q), :] = y_ref[...]

    @pl.when(fwd)
    def _():
        y = y_ref[...] + ybwd_ref[pl.ds(voff, SSD_Q), :] + dskip_ref[...] * xs
        y = y * _silu(z_ref[...])
        for g in range(SSD_GROUPS):
            yg = y[:, g * hp:(g + 1) * hp]
            o_ref[:, g * hp:(g + 1) * hp] = (_rms(yg) * gn_ref[:, g * hp:(g + 1) * hp]).astype(BF16)


def _ssd(xbc, z, dt_col, dt_row, cw, cb, bias_r, bias_c, alog_r, alog_c, dskip_e, gn, e16, *, l):
    n_halo = T_ALL // HALO
    per_blk = SSD_Q // HALO

    def blk(b, p, s):
        return _ssd_row_block(b, p, s)

    def conv_blk(b, p, s):
        return _ssd_row_block(b, 0, jnp.where(p == 0, s, SSD_STEPS - 1))

    def out_blk(b, p, s):
        return _ssd_row_block(b, 1, jnp.where(p == 1, s, 0))

    return pl.pallas_call(
        _ssd_kernel,
        out_shape=jax.ShapeDtypeStruct((T_ALL, SSD_INNER), BF16),
        grid=(BATCH, 2, SSD_STEPS),
        in_specs=[
            pl.BlockSpec((SSD_Q, SSD_CONV_CH), lambda b, p, s: (conv_blk(b, p, s), 0)),
            pl.BlockSpec((HALO, SSD_CONV_CH),
                         lambda b, p, s: (jnp.maximum(conv_blk(b, p, s) * per_blk - 1, 0), 0)),
            pl.BlockSpec((HALO, SSD_CONV_CH),
                         lambda b, p, s: (jnp.minimum((conv_blk(b, p, s) + 1) * per_blk, n_halo - 1), 0)),
            pl.BlockSpec((SSD_Q, SSD_INNER), lambda b, p, s: (out_blk(b, p, s), 0)),
            pl.BlockSpec((None, SSD_Q, SSD_HEADS), lambda b, p, s: (1 - p, blk(b, p, s), 0)),
            pl.BlockSpec((None, SSD_HEADS, SSD_Q), lambda b, p, s: (1 - p, 0, blk(b, p, s))),
            pl.BlockSpec((None, SSD_CONV, SSD_CONV_CH), lambda b, p, s: (l, 0, 0)),
            pl.BlockSpec((None, 1, SSD_CONV_CH), lambda b, p, s: (l, 0, 0)),
            pl.BlockSpec((None, None, 1, SSD_HEADS), lambda b, p, s: (l, 1 - p, 0, 0)),
            pl.BlockSpec((None, None, SSD_HEADS, 1), lambda b, p, s: (l, 1 - p, 0, 0)),
            pl.BlockSpec((None, None, 1, SSD_HEADS), lambda b, p, s: (l, 1 - p, 0, 0)),
            pl.BlockSpec((None, None, SSD_HEADS, 1), lambda b, p, s: (l, 1 - p, 0, 0)),
            pl.BlockSpec((None, 1, SSD_INNER), lambda b, p, s: (l, 0, 0)),
            pl.BlockSpec((None, 1, SSD_INNER), lambda b, p, s: (l, 0, 0)),
            pl.BlockSpec((SSD_HEADS, SSD_INNER), lambda b, p, s: (0, 0)),
        ],
        out_specs=pl.BlockSpec((SSD_Q, SSD_INNER), lambda b, p, s: (out_blk(b, p, s), 0)),
        scratch_shapes=[
            pltpu.VMEM((SSD_Q + 2 * HALO, SSD_CONV_CH), F32),
            pltpu.VMEM((SSD_STATE, SSD_INNER), F32),
            pltpu.VMEM((SSD_STEPS * SSD_Q, SSD_INNER), F32),
            pltpu.VMEM((SSD_Q, SSD_INNER), F32),
            pltpu.VMEM((SSD_STEPS * SSD_Q, SSD_CONV_CH), BF16),
        ],
        compiler_params=_cparams(("parallel", "arbitrary", "arbitrary")),
        name="ssd",
    )(xbc, xbc, xbc, z, dt_col, dt_row, cw, cb, bias_r, bias_c, alog_r, alog_c, dskip_e, gn, e16)


def _mla_prep_kernel(cq_ref, ckv_ref, sm_ref, tab_ref, gq_ref, gkv_ref, wq_ref, wk_ref, wv_ref,
                     q_ref, k_ref, v_ref):
    cq_tab = tab_ref[:, 0:LANE]
    sq_tab = tab_ref[:, LANE:2 * LANE]
    ck_tab = tab_ref[:, 2 * LANE:3 * LANE]
    sk_tab = tab_ref[:, 3 * LANE:4 * LANE]
    nq = (_rms(cq_ref[...]) * gq_ref[...]).astype(BF16)
    q = _dot(nq, wq_ref[...])
    for h in range(MLA_HEADS):
        qh = q[:, h * LANE:(h + 1) * LANE]
        qh = qh * cq_tab + pltpu.roll(qh, LANE - MLA_ROPE, 1) * sq_tab
        q_ref[:, h * LANE:(h + 1) * LANE] = qh.astype(BF16)
    nkv = _rms(ckv_ref[...]) * gkv_ref[...]
    sm = sm_ref[...]
    kr = sm * ck_tab + pltpu.roll(sm, LANE - MLA_ROPE, 1) * sk_tab
    lhs = jnp.concatenate([nkv, kr], axis=1).astype(BF16)
    k_ref[...] = _dot(lhs, wk_ref[...]).T.astype(BF16)
    v = _dot(lhs[:, :MLA_KV_RANK], wv_ref[...])
    lane = lax.broadcasted_iota(jnp.int32, v.shape, 1)
    v_ref[...] = jnp.where(lane % LANE == MLA_V, 1.0, v).astype(BF16)


def _mla_prep(cq, ckv, small, tab, gq, gkv, wq, wk, wv, *, l):
    tm = PREP_TM
    tiles_per_seq = SEQ // tm

    def tab_idx(i):
        return jnp.where(i < T_X // tm, i % tiles_per_seq, tiles_per_seq)

    return pl.pallas_call(
        _mla_prep_kernel,
        out_shape=[jax.ShapeDtypeStruct((T_ALL, MLA_HEADS * LANE), BF16),
                   jax.ShapeDtypeStruct((MLA_HEADS * LANE, T_ALL), BF16),
                   jax.ShapeDtypeStruct((T_ALL, MLA_HEADS * LANE), BF16)],
        grid=(T_ALL // tm,),
        in_specs=[
            pl.BlockSpec((tm, MLA_Q_RANK), lambda i: (i, 0)),
            pl.BlockSpec((tm, MLA_KV_RANK), lambda i: (i, 0)),
            pl.BlockSpec((tm, LANE), lambda i: (i, 0)),
            pl.BlockSpec((tm, 4 * LANE), lambda i: (tab_idx(i), 0)),
            pl.BlockSpec((None, 1, MLA_Q_RANK), lambda i: (l, 0, 0)),
            pl.BlockSpec((None, 1, MLA_KV_RANK), lambda i: (l, 0, 0)),
            pl.BlockSpec((None, MLA_Q_RANK, MLA_HEADS * LANE), lambda i: (l, 0, 0)),
            pl.BlockSpec((None, MLA_KV_RANK + LANE, MLA_HEADS * LANE), lambda i: (l, 0, 0)),
            pl.BlockSpec((None, MLA_KV_RANK, MLA_HEADS * LANE), lambda i: (l, 0, 0)),
        ],
        out_specs=[pl.BlockSpec((tm, MLA_HEADS * LANE), lambda i: (i, 0)),
                   pl.BlockSpec((MLA_HEADS * LANE, tm), lambda i: (0, i)),
                   pl.BlockSpec((tm, MLA_HEADS * LANE), lambda i: (i, 0))],
        compiler_params=_cparams(("parallel",)),
        name="mla_prep",
    )(cq, ckv, small, tab, gq, gkv, wq, wk, wv)


def _mla_attn_kernel(*refs, n_kv):
    q_ref = refs[0]
    k_refs = refs[1:1 + n_kv]
    v_refs = refs[1 + n_kv:1 + 2 * n_kv]
    o_ref = refs[1 + 2 * n_kv]
    outs = []
    for h in range(2):
        hl = slice(h * LANE, (h + 1) * LANE)
        q = q_ref[:, hl]
        m, acc = None, None
        for k_ref, v_ref in zip(k_refs, v_refs):
            n_keys = k_ref.shape[1]
            for c0 in range(0, n_keys, ATT_TK):
                c1 = min(c0 + ATT_TK, n_keys)
                s = _dot(q, k_ref[hl, c0:c1])
                m_new = s.max(axis=-1, keepdims=True)
                if m is not None:
                    m_new = jnp.maximum(m, m_new)
                    acc = acc * jnp.exp2(m - m_new)
                pv = _dot(jnp.exp2(s - m_new).astype(BF16), v_ref[c0:c1, hl])
                acc = pv if acc is None else acc + pv
                m = m_new
        outs.append(acc / acc[:, MLA_V:MLA_V + 1])
    lane = lax.broadcasted_iota(jnp.int32, outs[0].shape, 1)
    o_ref[...] = jnp.where(lane < MLA_V, outs[0], pltpu.roll(outs[1], MLA_V, 1))


def _mla_attn_latent(q, k, v):
    tq = ATT_TQ
    qt = SEQ // tq
    ctx_blk0 = T_X // CTX_LEN
    return pl.pallas_call(
        functools.partial(_mla_attn_kernel, n_kv=2),
        out_shape=jax.ShapeDtypeStruct((T_X, MLA_HEADS * MLA_V), F32),
        grid=(BATCH, MLA_HEADS // 2, qt),
        in_specs=[
            pl.BlockSpec((tq, 2 * LANE), lambda b, h, t: (b * qt + t, h)),
            pl.BlockSpec((2 * LANE, SEQ), lambda b, h, t: (h, b)),
            pl.BlockSpec((2 * LANE, CTX_LEN), lambda b, h, t: (h, ctx_blk0 + b)),
            pl.BlockSpec((SEQ, 2 * LANE), lambda b, h, t: (b, h)),
            pl.BlockSpec((CTX_LEN, 2 * LANE), lambda b, h, t: (ctx_blk0 + b, h)),
        ],
        out_specs=pl.BlockSpec((tq, LANE), lambda b, h, t: (b * qt + t, h)),
        compiler_params=_cparams(("parallel", "parallel", "arbitrary")),
        name="mla_attn_latent",
    )(q, k, k, v, v)


def _mla_attn_ctx(q, k, v):
    ctx_blk0 = T_X // CTX_LEN
    return pl.pallas_call(
        functools.partial(_mla_attn_kernel, n_kv=1),
        out_shape=jax.ShapeDtypeStruct((T_C, MLA_HEADS * MLA_V), F32),
        grid=(BATCH, MLA_HEADS // 2),
        in_specs=[
            pl.BlockSpec((CTX_LEN, 2 * LANE), lambda b, h: (ctx_blk0 + b, h)),
            pl.BlockSpec((2 * LANE, CTX_LEN), lambda b, h: (h, ctx_blk0 + b)),
            pl.BlockSpec((CTX_LEN, 2 * LANE), lambda b, h: (ctx_blk0 + b, h)),
        ],
        out_specs=pl.BlockSpec((CTX_LEN, LANE), lambda b, h: (b, h)),
        compiler_params=_cparams(("parallel", "parallel")),
        name="mla_attn_ctx",
    )(q, k, v)


def _swa_core(q, keys, vals, masks, sinks, o_ref):
    w = q.shape[0]
    lane = lax.broadcasted_iota(jnp.int32, (w, LANE), 1)
    low = lane < SWA_HEADDIM
    rep = SWA_HEADS // SWA_KV_HEADS
    k_all = jnp.concatenate(keys, axis=0).astype(BF16)
    v_all = jnp.concatenate(vals, axis=0).astype(BF16)
    bias = None
    if any(mask is not None for mask in masks):
        bias = jnp.concatenate(
            [jnp.zeros((w, kb.shape[0]), F32) if mask is None else jnp.where(mask, 0.0, NEG)
             for kb, mask in zip(keys, masks)], axis=1)
        bias = jnp.concatenate([bias] * rep, axis=0)
    for g in range(SWA_KV_HEADS):
        parts = []
        for j in range(rep // 2):
            qp = q[:, (g * rep // 2 + j) * LANE:(g * rep // 2 + j + 1) * LANE]
            parts.append(jnp.where(low, qp, 0.0))
            parts.append(jnp.where(low, 0.0, qp))
        lhs = jnp.concatenate(parts, axis=0).astype(BF16)
        sink = jnp.concatenate([jnp.full((w, 1), sinks[g * rep + r] * LOG2E, F32) for r in range(rep)],
                               axis=0)
        s = _dot_t(lhs, k_all[:, g * LANE:(g + 1) * LANE])
        if bias is not None:
            s = s + bias
        m = jnp.maximum(sink, s.max(axis=-1, keepdims=True))
        p = jnp.exp2(s - m)
        den = jnp.exp2(sink - m) + p.sum(axis=-1, keepdims=True)
        o = _dot(p.astype(BF16), v_all[:, g * LANE:(g + 1) * LANE]) / den
        for j in range(rep // 2):
            o_ref[:, (g * rep // 2 + j) * LANE:(g * rep // 2 + j + 1) * LANE] = jnp.where(
                low, o[2 * j * w:(2 * j + 1) * w], o[(2 * j + 1) * w:(2 * j + 2) * w])


def _rope_rows(x_ref, tab_ref, n, reps):
    cos = jnp.concatenate([tab_ref[:, :LANE]] * reps, axis=1)
    sin = jnp.concatenate([tab_ref[:, LANE:]] * reps, axis=1)
    return x_ref[:, :n] * cos + x_ref[:, n:] * sin


def _swa_latent_kernel(sink_ref, q_ref, kp_ref, kc_ref, kn_ref, kx_ref, vp_ref, vc_ref, vn_ref,
                       vx_ref, tq_ref, tp_ref, tn_ref, o_ref, *, l):
    n = pl.program_id(1)
    q = _rope_rows(q_ref, tq_ref, N_SWA_Q, N_SWA_Q // LANE) * SWA_QSCALE
    keys = [kx_ref[:, :N_SWA_KV2],
            _rope_rows(kp_ref, tp_ref, N_SWA_KV2, N_SWA_KV2 // LANE),
            _rope_rows(kc_ref, tq_ref, N_SWA_KV2, N_SWA_KV2 // LANE),
            _rope_rows(kn_ref, tn_ref, N_SWA_KV2, N_SWA_KV2 // LANE)]
    vals = [vx_ref[...], vp_ref[...], vc_ref[...], vn_ref[...]]
    qi = lax.broadcasted_iota(jnp.int32, (SWA_BLOCK, SWA_BLOCK), 0)
    kj = lax.broadcasted_iota(jnp.int32, (SWA_BLOCK, SWA_BLOCK), 1)
    prev_ok = jnp.logical_and(kj - qi >= SWA_BLOCK - SWA_WINDOW, n > 0)
    next_ok = jnp.logical_and(kj - qi <= SWA_WINDOW - SWA_BLOCK, n < SEQ // SWA_BLOCK - 1)
    sinks = [sink_ref[l, h] for h in range(SWA_HEADS)]
    _swa_core(q, keys, vals, [None, prev_ok, None, next_ok], sinks, o_ref)


def _swa_ctx_kernel(sink_ref, q_ref, kx_ref, vx_ref, o_ref, *, l):
    q = q_ref[:, :N_SWA_Q] * SWA_QSCALE
    sinks = [sink_ref[l, h] for h in range(SWA_HEADS)]
    _swa_core(q, [kx_ref[:, :N_SWA_KV2]], [vx_ref[...]], [None], sinks, o_ref)


def _swa_latent(sink, qs2, ks2, vs2, tab, *, l):
    nb = SEQ // SWA_BLOCK
    ctx_blk0 = T_X // CTX_LEN
    w = SWA_BLOCK

    def prev(b, n):
        return b * nb + jnp.maximum(n - 1, 0)

    def nxt(b, n):
        return b * nb + jnp.minimum(n + 1, nb - 1)

    return pl.pallas_call(
        functools.partial(_swa_latent_kernel, l=l),
        out_shape=jax.ShapeDtypeStruct((T_X, N_SWA_Q), F32),
        grid=(BATCH, nb),
        in_specs=[
            pl.BlockSpec(memory_space=pltpu.SMEM),
            pl.BlockSpec((w, 2 * N_SWA_Q), lambda b, n: (b * nb + n, 0)),
            pl.BlockSpec((w, 2 * N_SWA_KV2), lambda b, n: (prev(b, n), 0)),
            pl.BlockSpec((w, 2 * N_SWA_KV2), lambda b, n: (b * nb + n, 0)),
            pl.BlockSpec((w, 2 * N_SWA_KV2), lambda b, n: (nxt(b, n), 0)),
            pl.BlockSpec((CTX_LEN, 2 * N_SWA_KV2), lambda b, n: (ctx_blk0 + b, 0)),
            pl.BlockSpec((w, N_SWA_KV2), lambda b, n: (prev(b, n), 0)),
            pl.BlockSpec((w, N_SWA_KV2), lambda b, n: (b * nb + n, 0)),
            pl.BlockSpec((w, N_SWA_KV2), lambda b, n: (nxt(b, n), 0)),
            pl.BlockSpec((CTX_LEN, N_SWA_KV2), lambda b, n: (ctx_blk0 + b, 0)),
            pl.BlockSpec((w, 2 * LANE), lambda b, n: (n, 0)),
            pl.BlockSpec((w, 2 * LANE), lambda b, n: (jnp.maximum(n - 1, 0), 0)),
            pl.BlockSpec((w, 2 * LANE), lambda b, n: (jnp.minimum(n + 1, nb - 1), 0)),
        ],
        out_specs=pl.BlockSpec((w, N_SWA_Q), lambda b, n: (b * nb + n, 0)),
        compiler_params=_cparams(("parallel", "arbitrary")),
        name="swa_latent",
    )(sink, qs2, ks2, ks2, ks2, ks2, vs2, vs2, vs2, vs2, tab, tab, tab)


def _swa_ctx(sink, qs2, ks2, vs2, *, l):
    w = SWA_BLOCK
    blk0 = T_X // w
    per_b = CTX_LEN // w
    ctx_blk0 = T_X // CTX_LEN
    return pl.pallas_call(
        functools.partial(_swa_ctx_kernel, l=l),
        out_shape=jax.ShapeDtypeStruct((T_C, N_SWA_Q), F32),
        grid=(BATCH, per_b),
        in_specs=[
            pl.BlockSpec(memory_space=pltpu.SMEM),
            pl.BlockSpec((w, 2 * N_SWA_Q), lambda b, n: (blk0 + b * per_b + n, 0)),
            pl.BlockSpec((CTX_LEN, 2 * N_SWA_KV2), lambda b, n: (ctx_blk0 + b, 0)),
            pl.BlockSpec((CTX_LEN, N_SWA_KV2), lambda b, n: (ctx_blk0 + b, 0)),
        ],
        out_specs=pl.BlockSpec((w, N_SWA_Q), lambda b, n: (b * per_b + n, 0)),
        compiler_params=_cparams(("parallel", "arbitrary")),
        name="swa_ctx",
    )(sink, qs2, ks2, vs2)


def _outproj_kernel(h_ref, mod_ref, ssd_ref, mla_ref, mla_c_ref, swa_ref, swa_c_ref, gm_ref, gs_ref,
                    w_ref, o_ref, *, n_lat):
    n_mla = MLA_HEADS * MLA_V
    is_lat = pl.program_id(0) < n_lat
    acc = _dot(ssd_ref[...], w_ref[0:SSD_INNER, :])
    mla = jnp.where(is_lat, mla_ref[...], mla_c_ref[...])
    mla = (_rms(mla) * gm_ref[...]).astype(BF16)
    acc += _dot(mla, w_ref[SSD_INNER:SSD_INNER + n_mla, :])
    swa = jnp.where(is_lat, swa_ref[...], swa_c_ref[...])
    swa = (_rms(swa) * gs_ref[...]).astype(BF16)
    acc += _dot(swa, w_ref[SSD_INNER + n_mla:, :])
    o_ref[...] = h_ref[...] + mod_ref[5:6, :] * acc


def _outproj(h, mod_l, ssd, mla, mla_c, swa, swa_c, gm, gs, w_out, *, l, rows):
    tm = OUT_TM
    n_mla = MLA_HEADS * MLA_V
    n_lat = T_X // tm

    def lat(i):
        return (jnp.minimum(i, n_lat - 1), 0)

    def ctx(i):
        return (jnp.maximum(i - n_lat, 0), 0)

    return pl.pallas_call(
        functools.partial(_outproj_kernel, n_lat=n_lat),
        out_shape=jax.ShapeDtypeStruct((rows, D_MODEL), F32),
        grid=(rows // tm,),
        in_specs=[
            pl.BlockSpec((tm, D_MODEL), lambda i: (i, 0)),
            pl.BlockSpec((None, N_MOD, D_MODEL), lambda i: (_group_of_tile(i, tm), 0, 0)),
            pl.BlockSpec((tm, SSD_INNER), lambda i: (i, 0)),
            pl.BlockSpec((tm, n_mla), lat),
            pl.BlockSpec((tm, n_mla), ctx),
            pl.BlockSpec((tm, N_SWA_Q), lat),
            pl.BlockSpec((tm, N_SWA_Q), ctx),
            pl.BlockSpec((None, 1, n_mla), lambda i: (l, 0, 0)),
            pl.BlockSpec((None, 1, N_SWA_Q), lambda i: (l, 0, 0)),
            pl.BlockSpec((None, D_MIX, D_MODEL), lambda i: (l, 0, 0)),
        ],
        out_specs=pl.BlockSpec((tm, D_MODEL), lambda i: (i, 0)),
        compiler_params=_cparams(("parallel",)),
        name="outproj",
    )(h, mod_l, ssd, mla, mla_c, swa, swa_c, gm, gs, w_out)


def _rot_src_sign(dim):
    q = dim // 4
    src = np.concatenate([np.arange(q, 2 * q), np.arange(0, q),
                          np.arange(3 * q, 4 * q), np.arange(2 * q, 3 * q)])
    sign = np.concatenate([-np.ones(q), np.ones(q), -np.ones(q), np.ones(q)])
    return src, sign


def _permute_cols(w, idx, sgn):
    pieces, pending, start, n = [], [], 0, len(idx)

    def flush():
        if pending:
            cols = np.concatenate(pending)
            pieces.append(w[..., idx[cols]] * sgn[cols])
            pending.clear()

    while start < n:
        end = start + 1
        while end < n and idx[end] == idx[end - 1] + 1 and sgn[end] == sgn[start] == 1:
            end += 1
        if end - start >= LANE:
            flush()
            pieces.append(w[..., int(idx[start]):int(idx[end - 1]) + 1])
        else:
            pending.append(np.arange(start, end))
        start = end
    flush()
    return jnp.concatenate(pieces, axis=-1)


def _w_in_layout():
    offs = np.concatenate([[0], np.cumsum(IN_SIZES)])
    o_z, o_xbc, o_dt, o_cq, o_ckv, o_kr, o_qs, o_ks, o_vs = offs[:-1]
    idx, sgn = [], []
    ident = (np.arange(SWA_HEADDIM), np.ones(SWA_HEADDIM))

    def plain(start, n):
        idx.append(np.arange(start, start + n))
        sgn.append(np.ones(n))

    def heads(start, n_heads, dim, src_sign, copies):
        src, sign = src_sign
        for h in range(n_heads):
            for _ in range(copies):
                idx.append(start + h * dim + src)
                sgn.append(sign)

    plain(o_z, SSD_INNER)
    plain(o_xbc, SSD_CONV_CH)
    plain(o_cq, MLA_Q_RANK)
    plain(o_ckv, MLA_KV_RANK)
    plain(o_qs, N_SWA_Q)
    heads(o_qs, SWA_HEADS, SWA_HEADDIM, _rot_src_sign(SWA_HEADDIM), 1)
    heads(o_ks, SWA_KV_HEADS, SWA_HEADDIM, ident, 2)
    heads(o_ks, SWA_KV_HEADS, SWA_HEADDIM, _rot_src_sign(SWA_HEADDIM), 2)
    heads(o_vs, SWA_KV_HEADS, SWA_HEADDIM, ident, 2)
    plain(o_dt, 2 * SSD_HEADS)
    plain(o_kr, MLA_ROPE)
    heads(o_kr, 1, MLA_ROPE, _rot_src_sign(MLA_ROPE), 1)
    idx.append(np.zeros(MLA_ROPE, np.int64))
    sgn.append(np.zeros(MLA_ROPE))
    idx = np.concatenate(idx).astype(np.int32)
    sgn = np.concatenate(sgn).astype(np.float32)
    assert idx.shape[0] == W_IN_COLS
    return idx, sgn


def _w_uq_layout():
    hd = MLA_NOPE + MLA_ROPE
    src, sign = _rot_src_sign(MLA_ROPE)
    idx, sgn = [], []
    for h in range(MLA_HEADS):
        idx.append(h * hd + np.arange(hd))
        sgn.append(np.ones(hd))
        idx.append(h * hd + MLA_NOPE + src)
        sgn.append(sign)
    return np.concatenate(idx).astype(np.int32), np.concatenate(sgn).astype(np.float32)


def _w_ukv_layout():
    hd = MLA_NOPE + MLA_V
    kidx, vidx, sgn = [], [], []
    for h in range(MLA_HEADS):
        kidx.append(h * hd + np.arange(MLA_NOPE))
        vidx.append(h * hd + MLA_NOPE + np.arange(MLA_V))
        sgn.append(np.ones(MLA_NOPE))
        kidx.append(np.zeros(LANE - MLA_NOPE, np.int64))
        vidx.append(np.zeros(LANE - MLA_V, np.int64))
        sgn.append(np.zeros(LANE - MLA_NOPE))
    return (np.concatenate(kidx).astype(np.int32), np.concatenate(vidx).astype(np.int32),
            np.concatenate(sgn).astype(np.float32))


def _k_rope_expand():
    e = np.zeros((LANE, MLA_HEADS * LANE), np.float32)
    for h in range(MLA_HEADS):
        for i in range(MLA_ROPE):
            e[MLA_ROPE + i, h * LANE + MLA_NOPE + i] = 1.0
    return e


def _head_expand():
    e = np.zeros((SSD_HEADS, SSD_INNER), np.float32)
    for h in range(SSD_HEADS):
        e[h, h * SSD_HEADDIM:(h + 1) * SSD_HEADDIM] = 1.0
    return e


def _rope_angles(dim):
    pos = np.arange(SEQ)
    quarter = dim // 4
    inv_freq = ROPE_THETA ** (-jnp.arange(quarter, dtype=F32) / quarter)
    ang_r = jnp.asarray(pos // GRID_W, F32)[:, None] * inv_freq[None, :]
    ang_c = jnp.asarray(pos % GRID_W, F32)[:, None] * inv_freq[None, :]
    ang = jnp.concatenate([ang_r, ang_r, ang_c, ang_c], axis=-1)
    return jnp.cos(ang), jnp.sin(ang)


def _mla_rope_table():
    cos, sin = _rope_angles(MLA_ROPE)
    scale = (MLA_NOPE + MLA_ROPE) ** -0.5 * LOG2E
    n = SEQ + PREP_TM
    ident = jnp.ones((PREP_TM, MLA_ROPE), F32)
    zeros = jnp.zeros((PREP_TM, MLA_ROPE), F32)
    cos = jnp.concatenate([cos, ident], axis=0)
    sin = jnp.concatenate([sin, zeros], axis=0)
    z32 = jnp.zeros((n, MLA_ROPE), F32)
    cq = jnp.concatenate([jnp.ones((n, MLA_NOPE), F32), cos, z32], axis=1) * scale
    sq = jnp.concatenate([jnp.zeros((n, MLA_NOPE), F32), sin, z32], axis=1) * scale
    ck = jnp.concatenate([z32, cos, z32, z32], axis=1)
    sk = jnp.concatenate([z32, sin, z32, z32], axis=1)
    return jnp.concatenate([cq, sq, ck, sk], axis=1)


def _swa_rope_table():
    cos, sin = _rope_angles(SWA_HEADDIM)
    return jnp.concatenate([cos, cos, sin, sin], axis=1)


def kernel(x, c, ctx, c_ctx, w_mod, b_mod, norm_g, ffn_w_gate, ffn_w_up, ffn_w_down,
           w_in, w_out, ssd_conv_w, ssd_conv_b, ssd_dt_bias, ssd_a_log, ssd_d, ssd_norm,
           mla_q_norm, mla_w_uq, mla_kv_norm, mla_w_ukv, mla_out_norm, swa_sink,
           swa_out_norm, final_norm):
    idx, sgn = _w_in_layout()
    w_in_p = _permute_cols(w_in, idx, sgn).astype(BF16)
    qidx, qsgn = _w_uq_layout()
    wq = _permute_cols(mla_w_uq, qidx, qsgn).astype(BF16)
    kidx, vidx, kvsgn = _w_ukv_layout()
    k_expand = jnp.broadcast_to(jnp.asarray(_k_rope_expand()), (DEPTH, LANE, MLA_HEADS * LANE))
    wk = jnp.concatenate([_permute_cols(mla_w_ukv, kidx, kvsgn), k_expand], axis=1).astype(BF16)
    wv = _permute_cols(mla_w_ukv, vidx, kvsgn).astype(BF16)
    wg = ffn_w_gate.astype(BF16)
    wu = ffn_w_up.astype(BF16)
    wd = ffn_w_down.astype(BF16)
    wo = w_out.astype(BF16)
    mla_tab = _mla_rope_table()
    swa_tab = _swa_rope_table()
    e16 = jnp.asarray(_head_expand(), BF16)
    dskip_e = jnp.repeat(ssd_d, SSD_HEADDIM, axis=-1).reshape(DEPTH, 1, SSD_INNER)
    ssd_gn = ssd_norm.reshape(DEPTH, 1, SSD_INNER)
    conv_b = ssd_conv_b.reshape(DEPTH, 1, SSD_CONV_CH)
    bias_r = ssd_dt_bias.reshape(DEPTH, 2, 1, SSD_HEADS)
    bias_c = ssd_dt_bias.reshape(DEPTH, 2, SSD_HEADS, 1)
    alog_r = ssd_a_log.reshape(DEPTH, 2, 1, SSD_HEADS)
    alog_c = ssd_a_log.reshape(DEPTH, 2, SSD_HEADS, 1)
    gq = mla_q_norm.reshape(DEPTH, 1, MLA_Q_RANK)
    gkv = mla_kv_norm.reshape(DEPTH, 1, MLA_KV_RANK)
    gm = mla_out_norm.reshape(DEPTH, 1, -1)
    gs = swa_out_norm.reshape(DEPTH, 1, -1)
    fin = final_norm.reshape(1, D_MODEL)

    c_all = jnp.concatenate([c, c_ctx[None, :], jnp.zeros((GROUP_PAD - N_GROUPS, D_MODEL), F32)], axis=0)
    mod_all = _modulation(c_all, w_mod, b_mod).reshape(DEPTH, GROUP_PAD, N_MOD, D_MODEL)

    h = x.reshape(T_X, D_MODEL)
    h_ctx = ctx.reshape(T_C, D_MODEL)

    for l in range(DEPTH):
        last = l == DEPTH - 1
        mod_l = mod_all[l]
        h = _ffn(h, h_ctx, mod_l, norm_g[l, 0].reshape(1, D_MODEL), wg, wu, wd, fin,
                 l=l, k=0, base=0, rows=T_ALL)
        h_ctx = None
        z, xbc, cq, ckv, qs2, ks2, vs2, small = _inproj(h, mod_l, norm_g[l, 1].reshape(1, D_MODEL),
                                                       w_in_p, l=l)
        dt2 = small[:, :2 * SSD_HEADS].reshape(T_ALL, 2, SSD_HEADS)
        dt_col = jnp.transpose(dt2, (1, 0, 2))
        dt_row = jnp.transpose(dt2, (1, 2, 0))
        ssd_o = _ssd(xbc, z, dt_col, dt_row, ssd_conv_w, conv_b, bias_r, bias_c, alog_r, alog_c,
                     dskip_e, ssd_gn, e16, l=l)
        q_m, k_m, v_m = _mla_prep(cq, ckv, small, mla_tab, gq, gkv, wq, wk, wv, l=l)
        mla_x = _mla_attn_latent(q_m, k_m, v_m)
        swa_x = _swa_latent(swa_sink, qs2, ks2, vs2, swa_tab, l=l)
        if last:
            rows, mla_c, swa_c = T_X, mla_x, swa_x
        else:
            rows = T_ALL
            mla_c = _mla_attn_ctx(q_m, k_m, v_m)
            swa_c = _swa_ctx(swa_sink, qs2, ks2, vs2, l=l)
        h = _outproj(h, mod_l, ssd_o, mla_x, mla_c, swa_x, swa_c, gm, gs, wo, l=l, rows=rows)
        h = _ffn(h, None, mod_l, norm_g[l, 2].reshape(1, D_MODEL), wg, wu, wd, fin,
                 l=l, k=1, base=6, rows=rows, final_norm=last)
    return h.reshape(BATCH, SEQ, D_MODEL)
```

```python
import functools
import math

import jax
import jax.numpy as jnp
import numpy as np
from jax import lax
from jax.experimental import pallas as pl
from jax.experimental.pallas import tpu as pltpu

F32 = jnp.float32
BF16 = jnp.bfloat16
HIGHEST = lax.Precision.HIGHEST

D_MODEL = 2048
BATCH = 4
SEQ = 4096
DEPTH = 4
GRID_W = 64
CTX_LEN = 256
EPS = 1e-6
ROPE_THETA = 10000.0
N_MOD = 9
D_FF = 5632

SSD_HEADS = 16
SSD_HEADDIM = 64
SSD_INNER = SSD_HEADS * SSD_HEADDIM
SSD_GROUPS = 2
SSD_STATE = 128
SSD_CONV = 5
SSD_CONV_CH = SSD_INNER + 2 * SSD_GROUPS * SSD_STATE

MLA_HEADS = 8
MLA_Q_RANK = 384
MLA_KV_RANK = 256
MLA_NOPE = 64
MLA_ROPE = 32
MLA_V = 64

SWA_HEADS = 8
SWA_KV_HEADS = 2
SWA_HEADDIM = 64
SWA_WINDOW = 128
SWA_BLOCK = 128

D_MIX = SSD_INNER + MLA_HEADS * MLA_V + SWA_HEADS * SWA_HEADDIM
IN_SIZES = (SSD_INNER, SSD_CONV_CH, 2 * SSD_HEADS, MLA_Q_RANK, MLA_KV_RANK, MLA_ROPE,
            SWA_HEADS * SWA_HEADDIM, SWA_KV_HEADS * SWA_HEADDIM, SWA_KV_HEADS * SWA_HEADDIM)

T_X = BATCH * SEQ
T_C = BATCH * CTX_LEN
T_ALL = T_X + T_C
N_GROUPS = BATCH + 1
GROUP_PAD = 8

LANE = 128
SUBLANE = 8
VMEM_LIMIT = 56 * 1024 * 1024
NEG = -1e30
LOG2E = math.log2(math.e)

FFN_TM = 512
FFN_TF = 512
PROJ_TM = 256
OUT_TM = 512
PREP_TM = 512
ATT_TQ = 512
ATT_TK = 2048
MOD_TN = 1024
CAST_ROWS = 512

SSD_Q = 128
SEQ_CHUNKS = SEQ // SSD_Q
CTX_CHUNKS = CTX_LEN // SSD_Q
SSD_STEPS = SEQ_CHUNKS + CTX_CHUNKS
HALO = SUBLANE
N_SWA_Q = SWA_HEADS * SWA_HEADDIM
N_SWA_KV2 = 2 * SWA_KV_HEADS * SWA_HEADDIM
SWA_QSCALE = SWA_HEADDIM ** -0.5 * LOG2E

SEG_Z = (0, 1024)
SEG_XBC = (1024, 2560)
SEG_CQ = (2560, 2944)
SEG_CKV = (2944, 3200)
SEG_QS = (3200, 4224)
SEG_KS = (4224, 4736)
SEG_VS = (4736, 4992)
SEG_SMALL = (4992, 5120)
W_IN_COLS = 5120
W_IN_MAIN = SEG_XBC[1]
IN_SEGS = (SEG_Z, SEG_XBC, SEG_CQ, SEG_CKV, SEG_QS, SEG_KS, SEG_VS, SEG_SMALL)


def _cparams(sem):
    return pltpu.CompilerParams(dimension_semantics=sem, vmem_limit_bytes=VMEM_LIMIT)


def _group_of_tile(i, tm):
    return jnp.minimum((i * tm) // SEQ, BATCH)


def _rms(x):
    return x * lax.rsqrt(jnp.mean(x * x, axis=-1, keepdims=True) + EPS)


def _silu(x):
    return x * jax.nn.sigmoid(x)


def _softplus(x):
    return jnp.maximum(x, 0.0) + jnp.log1p(jnp.exp(-jnp.abs(x)))


def _dot(a, b):
    return jnp.dot(a, b, preferred_element_type=F32)


def _dot_t(a, b):
    return lax.dot_general(a, b, (((1,), (1,)), ((), ())), preferred_element_type=F32)


def _mod_kernel(c_ref, w_ref, b_ref, o_ref):
    sc = _silu(c_ref[...]).astype(BF16)
    o_ref[...] = _dot(sc, w_ref[...].astype(BF16)) + b_ref[...]


def _modulation(c_all, w_mod, b_mod):
    n = N_MOD * D_MODEL
    return pl.pallas_call(
        _mod_kernel,
        out_shape=jax.ShapeDtypeStruct((DEPTH, GROUP_PAD, n), F32),
        grid=(DEPTH, n // MOD_TN),
        in_specs=[
            pl.BlockSpec((GROUP_PAD, D_MODEL), lambda l, j: (0, 0)),
            pl.BlockSpec((None, D_MODEL, MOD_TN), lambda l, j: (l, 0, j)),
            pl.BlockSpec((None, 1, MOD_TN), lambda l, j: (l, 0, j)),
        ],
        out_specs=pl.BlockSpec((None, GROUP_PAD, MOD_TN), lambda l, j: (l, 0, j)),
        compiler_params=_cparams(("parallel", "parallel")),
        name="modulation",
    )(c_all, w_mod, b_mod.reshape(DEPTH, 1, n))


def _norm_modulate(x, g_ref, mod_ref, base):
    y = _rms(x) * g_ref[...]
    return (y * (1.0 + mod_ref[base + 1:base + 2, :]) + mod_ref[base:base + 1, :]).astype(BF16)


def _ffn_kernel(*refs, split, base, n_ff, n_lat, final_norm):
    x_ref, xc_ref = (refs[0], refs[1]) if split else (refs[0], None)
    mod_ref, g_ref, wg_ref, wu_ref, wd_ref, fin_ref, o_ref, u_ref = refs[(2 if split else 1):]
    j = pl.program_id(1)

    def residual():
        if xc_ref is None:
            return x_ref[...]
        return jnp.where(pl.program_id(0) < n_lat, x_ref[...], xc_ref[...])

    @pl.when(j == 0)
    def _():
        u_ref[...] = _norm_modulate(residual(), g_ref, mod_ref, base)
        o_ref[...] = jnp.zeros_like(o_ref)

    u = u_ref[...]
    gate = _dot(u, wg_ref[...])
    up = _dot(u, wu_ref[...])
    a = (_silu(gate) * up).astype(BF16)
    o_ref[...] += _dot(a, wd_ref[...])

    @pl.when(j == n_ff - 1)
    def _():
        h = residual() + (0.5 * mod_ref[base + 2:base + 3, :]) * o_ref[...]
        if final_norm:
            h = _rms(h) * fin_ref[...]
        o_ref[...] = h


def _ffn(h, h_ctx, mod_l, g, wg, wu, wd, fin, *, l, k, base, rows, final_norm=False):
    n_ff = D_FF // FFN_TF
    tm = FFN_TM
    n_lat = T_X // tm
    split = h_ctx is not None
    kern = functools.partial(_ffn_kernel, split=split, base=base, n_ff=n_ff, n_lat=n_lat,
                             final_norm=final_norm)
    if split:
        x_specs = [pl.BlockSpec((tm, D_MODEL), lambda i, j: (jnp.minimum(i, n_lat - 1), 0)),
                   pl.BlockSpec((tm, D_MODEL), lambda i, j: (jnp.maximum(i - n_lat, 0), 0))]
        x_args = (h, h_ctx)
    else:
        x_specs = [pl.BlockSpec((tm, D_MODEL), lambda i, j: (i, 0))]
        x_args = (h,)
    return pl.pallas_call(
        kern,
        out_shape=jax.ShapeDtypeStruct((rows, D_MODEL), F32),
        grid=(rows // tm, n_ff),
        in_specs=x_specs + [
            pl.BlockSpec((None, N_MOD, D_MODEL), lambda i, j: (_group_of_tile(i, tm), 0, 0)),
            pl.BlockSpec((1, D_MODEL), lambda i, j: (0, 0)),
            pl.BlockSpec((None, None, D_MODEL, FFN_TF), lambda i, j: (l, k, 0, j)),
            pl.BlockSpec((None, None, D_MODEL, FFN_TF), lambda i, j: (l, k, 0, j)),
            pl.BlockSpec((None, None, FFN_TF, D_MODEL), lambda i, j: (l, k, j, 0)),
            pl.BlockSpec((1, D_MODEL), lambda i, j: (0, 0)),
        ],
        out_specs=pl.BlockSpec((tm, D_MODEL), lambda i, j: (i, 0)),
        scratch_shapes=[pltpu.VMEM((tm, D_MODEL), BF16)],
        compiler_params=_cparams(("parallel", "arbitrary")),
        name="ffn",
    )(*x_args, mod_l, g, wg, wu, wd, fin)


def _inproj_kernel(x_ref, mod_ref, g_ref, wm_ref, we_ref, *o_refs):
    u = _norm_modulate(x_ref[...], g_ref, mod_ref, 3)
    for (a, b), o_ref in zip(IN_SEGS, o_refs):
        if b <= W_IN_MAIN:
            o_ref[...] = _dot(u, wm_ref[:, a:b])
        else:
            o_ref[...] = _dot(u, we_ref[:, a - W_IN_MAIN:b - W_IN_MAIN])


def _inproj(h, mod_l, g, w_main, w_extra, *, l):
    tm = PROJ_TM
    return pl.pallas_call(
        _inproj_kernel,
        out_shape=[jax.ShapeDtypeStruct((T_ALL, b - a), F32) for a, b in IN_SEGS],
        grid=(T_ALL // tm,),
        in_specs=[
            pl.BlockSpec((tm, D_MODEL), lambda i: (i, 0)),
            pl.BlockSpec((None, N_MOD, D_MODEL), lambda i: (_group_of_tile(i, tm), 0, 0)),
            pl.BlockSpec((1, D_MODEL), lambda i: (0, 0)),
            pl.BlockSpec((None, D_MODEL, W_IN_MAIN), lambda i: (l, 0, 0),
                         pipeline_mode=pl.Buffered(1)),
            pl.BlockSpec((None, D_MODEL, W_IN_COLS - W_IN_MAIN), lambda i: (l, 0, 0),
                         pipeline_mode=pl.Buffered(1)),
        ],
        out_specs=[pl.BlockSpec((tm, b - a), lambda i: (i, 0)) for a, b in IN_SEGS],
        compiler_params=_cparams(("parallel",)),
        name="inproj",
    )(h, mod_l, g, w_main, w_extra)


def _ssd_chunk_ids(phase, s):
    fwd = phase == 1
    is_ctx = s < CTX_CHUNKS
    cc = jnp.where(fwd, s, CTX_CHUNKS - 1 - s)
    lc = jnp.where(fwd, s - CTX_CHUNKS, SSD_STEPS - 1 - s)
    return is_ctx, cc, lc


def _ssd_row_block(b, phase, s):
    is_ctx, cc, lc = _ssd_chunk_ids(phase, s)
    return jnp.where(is_ctx, T_X // SSD_Q + b * CTX_CHUNKS + cc, b * SEQ_CHUNKS + lc)


def _ssd_kernel(xbc_ref, prev_ref, next_ref, z_ref, dtc_ref, dtr_ref, cw_ref, cb_ref,
                bias_r_ref, bias_c_ref, alog_r_ref, alog_c_ref, dskip_ref, gn_ref, e16_ref,
                o_ref, xe_ref, state_ref, ybwd_ref, y_ref, xcs_ref):
    phase = pl.program_id(1)
    s = pl.program_id(2)
    fwd = phase == 1
    is_ctx, cc, lc = _ssd_chunk_ids(phase, s)
    first = jnp.where(is_ctx, cc == 0, lc == 0)
    last = jnp.where(is_ctx, cc == CTX_CHUNKS - 1, lc == SEQ_CHUNKS - 1)
    voff = pl.multiple_of(jnp.where(is_ctx, cc, CTX_CHUNKS + lc) * SSD_Q, SSD_Q)

    @pl.when(s == 0)
    def _():
        state_ref[...] = jnp.zeros_like(state_ref)

    @pl.when(jnp.logical_not(fwd))
    def _():
        xe_ref[0:HALO, :] = jnp.where(first, 0.0, prev_ref[...])
        xe_ref[HALO:HALO + SSD_Q, :] = xbc_ref[...]
        xe_ref[HALO + SSD_Q:, :] = jnp.where(last, 0.0, next_ref[...])
        acc = cb_ref[...] + cw_ref[0:1, :] * xe_ref[HALO - 2:HALO - 2 + SSD_Q, :]
        for k in range(1, SSD_CONV):
            acc = acc + cw_ref[k:k + 1, :] * xe_ref[HALO - 2 + k:HALO - 2 + k + SSD_Q, :]
        xcs_ref[pl.ds(voff, SSD_Q), :] = _silu(acc).astype(BF16)

    xc = xcs_ref[pl.ds(voff, SSD_Q), :]
    xs = xc[:, :SSD_INNER].astype(F32)
    n_bc = SSD_GROUPS * SSD_STATE

    dt_c = _softplus(dtc_ref[...] + bias_r_ref[...])
    dt_r = _softplus(dtr_ref[...] + bias_c_ref[...])
    a_c = dt_c * -jnp.exp(alog_r_ref[...])
    a_r = dt_r * -jnp.exp(alog_c_ref[...])
    row = lax.broadcasted_iota(jnp.int32, (SSD_Q, SSD_Q), 0)
    col = lax.broadcasted_iota(jnp.int32, (SSD_Q, SSD_Q), 1)
    d = (row - col) * jnp.where(fwd, 1, -1)
    causal = d >= 0
    s_c = jnp.dot(causal.astype(F32), a_c, precision=HIGHEST, preferred_element_type=F32)
    s_r = jnp.dot(a_r, (d <= 0).astype(F32), precision=HIGHEST, preferred_element_type=F32)

    e16 = e16_ref[...]
    rem = jnp.concatenate([dt_c, s_c], axis=0)
    expanded = None
    for _ in range(3):
        term = rem.astype(BF16)
        rem = rem - term.astype(F32)
        part = _dot(term, e16)
        expanded = part if expanded is None else expanded + part
    dt_e = expanded[:SSD_Q]
    s_e = expanded[SSD_Q:]
    tot_e = jnp.where(fwd, s_e[SSD_Q - 1:SSD_Q, :], s_e[0:1, :])
    xdt = xs * dt_e
    xdt_end = (xdt * jnp.exp(tot_e - s_e)).astype(BF16)
    xdt16 = xdt.astype(BF16)
    exps_e = jnp.exp(s_e)
    dec_e = jnp.exp(tot_e)
    lane = lax.broadcasted_iota(jnp.int32, (SSD_Q, LANE), 1)

    hp = SSD_INNER // SSD_GROUPS
    heads_per_group = SSD_HEADS // SSD_GROUPS
    for g in range(SSD_GROUPS):
        bg = xc[:, SSD_INNER + g * SSD_STATE:SSD_INNER + (g + 1) * SSD_STATE]
        cg = xc[:, SSD_INNER + n_bc + g * SSD_STATE:SSD_INNER + n_bc + (g + 1) * SSD_STATE]
        bt16 = bg.astype(F32).T.astype(BF16)
        cg16 = cg
        cb = _dot(cg16, bt16)
        st = state_ref[:, g * hp:(g + 1) * hp]
        y_off = _dot(cg16, st.astype(BF16)) * exps_e[:, g * hp:(g + 1) * hp]
        state_ref[:, g * hp:(g + 1) * hp] = (dec_e[:, g * hp:(g + 1) * hp] * st
                                             + _dot(bt16, xdt_end[:, g * hp:(g + 1) * hp]))
        for j in range(heads_per_group // 2):
            h0 = g * heads_per_group + 2 * j
            m = []
            for h in (h0, h0 + 1):
                seg = s_c[:, h:h + 1] - s_r[h:h + 1, :]
                m.append(cb * jnp.exp(jnp.where(causal, seg, NEG)))
            m = jnp.concatenate(m, axis=0).astype(BF16)
            lo = h0 * SSD_HEADDIM
            yp = _dot(m, xdt16[:, lo:lo + LANE])
            y_ref[:, lo:lo + LANE] = (jnp.where(lane < SSD_HEADDIM, yp[:SSD_Q], yp[SSD_Q:])
                                      + y_off[:, lo - g * hp:lo - g * hp + LANE])

    @pl.when(jnp.logical_not(fwd))
    def _():
        ybwd_ref[pl.ds(voff, SSD_Q), :] = y_ref[...]

    @pl.when(fwd)
    def _():
        y = y_ref[...] + ybwd_ref[pl.ds(voff, SSD_Q), :] + dskip_ref[...] * xs
        y = y * _silu(z_ref[...])
        for g in range(SSD_GROUPS):
            yg = y[:, g * hp:(g + 1) * hp]
            o_ref[:, g * hp:(g + 1) * hp] = (_rms(yg) * gn_ref[:, g * hp:(g + 1) * hp]).astype(BF16)


def _ssd(xbc, z, dt_col, dt_row, cw, cb, bias_r, bias_c, alog_r, alog_c, dskip_e, gn, e16, *, l):
    n_halo = T_ALL // HALO
    per_blk = SSD_Q // HALO

    def blk(b, p, s):
        return _ssd_row_block(b, p, s)

    def conv_blk(b, p, s):
        return _ssd_row_block(b, 0, jnp.where(p == 0, s, SSD_STEPS - 1))

    def out_blk(b, p, s):
        return _ssd_row_block(b, 1, jnp.where(p == 1, s, 0))

    return pl.pallas_call(
        _ssd_kernel,
        out_shape=jax.ShapeDtypeStruct((T_ALL, SSD_INNER), BF16),
        grid=(BATCH, 2, SSD_STEPS),
        in_specs=[
            pl.BlockSpec((SSD_Q, SSD_CONV_CH), lambda b, p, s: (conv_blk(b, p, s), 0)),
            pl.BlockSpec((HALO, SSD_CONV_CH),
                         lambda b, p, s: (jnp.maximum(conv_blk(b, p, s) * per_blk - 1, 0), 0)),
            pl.BlockSpec((HALO, SSD_CONV_CH),
                         lambda b, p, s: (jnp.minimum((conv_blk(b, p, s) + 1) * per_blk, n_halo - 1), 0)),
            pl.BlockSpec((SSD_Q, SSD_INNER), lambda b, p, s: (out_blk(b, p, s), 0)),
            pl.BlockSpec((None, SSD_Q, SSD_HEADS), lambda b, p, s: (1 - p, blk(b, p, s), 0)),
            pl.BlockSpec((None, SSD_HEADS, SSD_Q), lambda b, p, s: (1 - p, 0, blk(b, p, s))),
            pl.BlockSpec((None, SSD_CONV, SSD_CONV_CH), lambda b, p, s: (l, 0, 0)),
            pl.BlockSpec((None, 1, SSD_CONV_CH), lambda b, p, s: (l, 0, 0)),
            pl.BlockSpec((None, None, 1, SSD_HEADS), lambda b, p, s: (l, 1 - p, 0, 0)),
            pl.BlockSpec((None, None, SSD_HEADS, 1), lambda b, p, s: (l, 1 - p, 0, 0)),
            pl.BlockSpec((None, None, 1, SSD_HEADS), lambda b, p, s: (l, 1 - p, 0, 0)),
            pl.BlockSpec((None, None, SSD_HEADS, 1), lambda b, p, s: (l, 1 - p, 0, 0)),
            pl.BlockSpec((None, 1, SSD_INNER), lambda b, p, s: (l, 0, 0)),
            pl.BlockSpec((None, 1, SSD_INNER), lambda b, p, s: (l, 0, 0)),
            pl.BlockSpec((SSD_HEADS, SSD_INNER), lambda b, p, s: (0, 0)),
        ],
        out_specs=pl.BlockSpec((SSD_Q, SSD_INNER), lambda b, p, s: (out_blk(b, p, s), 0)),
        scratch_shapes=[
            pltpu.VMEM((SSD_Q + 2 * HALO, SSD_CONV_CH), F32),
            pltpu.VMEM((SSD_STATE, SSD_INNER), F32),
            pltpu.VMEM((SSD_STEPS * SSD_Q, SSD_INNER), F32),
            pltpu.VMEM((SSD_Q, SSD_INNER), F32),
            pltpu.VMEM((SSD_STEPS * SSD_Q, SSD_CONV_CH), BF16),
        ],
        compiler_params=_cparams(("parallel", "arbitrary", "arbitrary")),
        name="ssd",
    )(xbc, xbc, xbc, z, dt_col, dt_row, cw, cb, bias_r, bias_c, alog_r, alog_c, dskip_e, gn, e16)


def _mla_prep_kernel(cq_ref, ckv_ref, sm_ref, tab_ref, gq_ref, gkv_ref, wq_ref, wk_ref, wv_ref,
                     q_ref, k_ref, v_ref):
    cq_tab = tab_ref[:, 0:LANE]
    sq_tab = tab_ref[:, LANE:2 * LANE]
    ck_tab = tab_ref[:, 2 * LANE:3 * LANE]
    sk_tab = tab_ref[:, 3 * LANE:4 * LANE]
    nq = (_rms(cq_ref[...]) * gq_ref[...]).astype(BF16)
    q = _dot(nq, wq_ref[...])
    for h in range(MLA_HEADS):
        qh = q[:, h * LANE:(h + 1) * LANE]
        qh = qh * cq_tab + pltpu.roll(qh, LANE - MLA_ROPE, 1) * sq_tab
        q_ref[:, h * LANE:(h + 1) * LANE] = qh.astype(BF16)
    nkv = _rms(ckv_ref[...]) * gkv_ref[...]
    sm = sm_ref[...]
    kr = sm * ck_tab + pltpu.roll(sm, LANE - MLA_ROPE, 1) * sk_tab
    lhs = jnp.concatenate([nkv, kr], axis=1).astype(BF16)
    k_ref[...] = _dot(lhs, wk_ref[...]).T.astype(BF16)
    v = _dot(lhs[:, :MLA_KV_RANK], wv_ref[...])
    lane = lax.broadcasted_iota(jnp.int32, v.shape, 1)
    v_ref[...] = jnp.where(lane % LANE == MLA_V, 1.0, v).astype(BF16)


def _mla_prep(cq, ckv, small, tab, gq, gkv, wq, wk, wv, *, l):
    tm = PREP_TM
    tiles_per_seq = SEQ // tm

    def tab_idx(i):
        return jnp.where(i < T_X // tm, i % tiles_per_seq, tiles_per_seq)

    return pl.pallas_call(
        _mla_prep_kernel,
        out_shape=[jax.ShapeDtypeStruct((T_ALL, MLA_HEADS * LANE), BF16),
                   jax.ShapeDtypeStruct((MLA_HEADS * LANE, T_ALL), BF16),
                   jax.ShapeDtypeStruct((T_ALL, MLA_HEADS * LANE), BF16)],
        grid=(T_ALL // tm,),
        in_specs=[
            pl.BlockSpec((tm, MLA_Q_RANK), lambda i: (i, 0)),
            pl.BlockSpec((tm, MLA_KV_RANK), lambda i: (i, 0)),
            pl.BlockSpec((tm, LANE), lambda i: (i, 0)),
            pl.BlockSpec((tm, 4 * LANE), lambda i: (tab_idx(i), 0)),
            pl.BlockSpec((None, 1, MLA_Q_RANK), lambda i: (l, 0, 0)),
            pl.BlockSpec((None, 1, MLA_KV_RANK), lambda i: (l, 0, 0)),
            pl.BlockSpec((None, MLA_Q_RANK, MLA_HEADS * LANE), lambda i: (l, 0, 0)),
            pl.BlockSpec((None, MLA_KV_RANK + LANE, MLA_HEADS * LANE), lambda i: (l, 0, 0)),
            pl.BlockSpec((None, MLA_KV_RANK, MLA_HEADS * LANE), lambda i: (l, 0, 0)),
        ],
        out_specs=[pl.BlockSpec((tm, MLA_HEADS * LANE), lambda i: (i, 0)),
                   pl.BlockSpec((MLA_HEADS * LANE, tm), lambda i: (0, i)),
                   pl.BlockSpec((tm, MLA_HEADS * LANE), lambda i: (i, 0))],
        compiler_params=_cparams(("parallel",)),
        name="mla_prep",
    )(cq, ckv, small, tab, gq, gkv, wq, wk, wv)


def _mla_attn_kernel(*refs, n_kv):
    q_ref = refs[0]
    k_refs = refs[1:1 + n_kv]
    v_refs = refs[1 + n_kv:1 + 2 * n_kv]
    o_ref = refs[1 + 2 * n_kv]
    outs = []
    for h in range(2):
        hl = slice(h * LANE, (h + 1) * LANE)
        q = q_ref[:, hl]
        m, acc = None, None
        for k_ref, v_ref in zip(k_refs, v_refs):
            n_keys = k_ref.shape[1]
            for c0 in range(0, n_keys, ATT_TK):
                c1 = min(c0 + ATT_TK, n_keys)
                s = _dot(q, k_ref[hl, c0:c1])
                m_new = s.max(axis=-1, keepdims=True)
                if m is not None:
                    m_new = jnp.maximum(m, m_new)
                    acc = acc * jnp.exp2(m - m_new)
                pv = _dot(jnp.exp2(s - m_new).astype(BF16), v_ref[c0:c1, hl])
                acc = pv if acc is None else acc + pv
                m = m_new
        outs.append(acc / acc[:, MLA_V:MLA_V + 1])
    lane = lax.broadcasted_iota(jnp.int32, outs[0].shape, 1)
    o_ref[...] = jnp.where(lane < MLA_V, outs[0], pltpu.roll(outs[1], MLA_V, 1))


def _mla_attn_latent(q, k, v):
    tq = ATT_TQ
    qt = SEQ // tq
    ctx_blk0 = T_X // CTX_LEN
    return pl.pallas_call(
        functools.partial(_mla_attn_kernel, n_kv=2),
        out_shape=jax.ShapeDtypeStruct((T_X, MLA_HEADS * MLA_V), F32),
        grid=(BATCH, MLA_HEADS // 2, qt),
        in_specs=[
            pl.BlockSpec((tq, 2 * LANE), lambda b, h, t: (b * qt + t, h)),
            pl.BlockSpec((2 * LANE, SEQ), lambda b, h, t: (h, b)),
            pl.BlockSpec((2 * LANE, CTX_LEN), lambda b, h, t: (h, ctx_blk0 + b)),
            pl.BlockSpec((SEQ, 2 * LANE), lambda b, h, t: (b, h)),
            pl.BlockSpec((CTX_LEN, 2 * LANE), lambda b, h, t: (ctx_blk0 + b, h)),
        ],
        out_specs=pl.BlockSpec((tq, LANE), lambda b, h, t: (b * qt + t, h)),
        compiler_params=_cparams(("parallel", "parallel", "arbitrary")),
        name="mla_attn_latent",
    )(q, k, k, v, v)


def _mla_attn_ctx(q, k, v):
    ctx_blk0 = T_X // CTX_LEN
    return pl.pallas_call(
        functools.partial(_mla_attn_kernel, n_kv=1),
        out_shape=jax.ShapeDtypeStruct((T_C, MLA_HEADS * MLA_V), F32),
        grid=(BATCH, MLA_HEADS // 2),
        in_specs=[
            pl.BlockSpec((CTX_LEN, 2 * LANE), lambda b, h: (ctx_blk0 + b, h)),
            pl.BlockSpec((2 * LANE, CTX_LEN), lambda b, h: (h, ctx_blk0 + b)),
            pl.BlockSpec((CTX_LEN, 2 * LANE), lambda b, h: (ctx_blk0 + b, h)),
        ],
        out_specs=pl.BlockSpec((CTX_LEN, LANE), lambda b, h: (b, h)),
        compiler_params=_cparams(("parallel", "parallel")),
        name="mla_attn_ctx",
    )(q, k, v)


def _swa_core(q, keys, vals, masks, sinks, o_ref):
    w = q.shape[0]
    lane = lax.broadcasted_iota(jnp.int32, (w, LANE), 1)
    low = lane < SWA_HEADDIM
    rep = SWA_HEADS // SWA_KV_HEADS
    k_all = jnp.concatenate(keys, axis=0).astype(BF16)
    v_all = jnp.concatenate(vals, axis=0).astype(BF16)
    bias = None
    if any(mask is not None for mask in masks):
        bias = jnp.concatenate(
            [jnp.zeros((w, kb.shape[0]), F32) if mask is None else jnp.where(mask, 0.0, NEG)
             for kb, mask in zip(keys, masks)], axis=1)
        bias = jnp.concatenate([bias] * rep, axis=0)
    for g in range(SWA_KV_HEADS):
        parts = []
        for j in range(rep // 2):
            qp = q[:, (g * rep // 2 + j) * LANE:(g * rep // 2 + j + 1) * LANE]
            parts.append(jnp.where(low, qp, 0.0))
            parts.append(jnp.where(low, 0.0, qp))
        lhs = jnp.concatenate(parts, axis=0).astype(BF16)
        sink = jnp.concatenate([jnp.full((w, 1), sinks[g * rep + r] * LOG2E, F32) for r in range(rep)],
                               axis=0)
        s = _dot_t(lhs, k_all[:, g * LANE:(g + 1) * LANE])
        if bias is not None:
            s = s + bias
        m = jnp.maximum(sink, s.max(axis=-1, keepdims=True))
        p = jnp.exp2(s - m)
        den = jnp.exp2(sink - m) + p.sum(axis=-1, keepdims=True)
        o = _dot(p.astype(BF16), v_all[:, g * LANE:(g + 1) * LANE]) / den
        for j in range(rep // 2):
            o_ref[:, (g * rep // 2 + j) * LANE:(g * rep // 2 + j + 1) * LANE] = jnp.where(
                low, o[2 * j * w:(2 * j + 1) * w], o[(2 * j + 1) * w:(2 * j + 2) * w])


def _rope_rows(x_ref, tab_ref, n, reps):
    cos = jnp.concatenate([tab_ref[:, :LANE]] * reps, axis=1)
    sin = jnp.concatenate([tab_ref[:, LANE:]] * reps, axis=1)
    return x_ref[:, :n] * cos + x_ref[:, n:] * sin


def _swa_latent_kernel(sink_ref, q_ref, kp_ref, kc_ref, kn_ref, kx_ref, vp_ref, vc_ref, vn_ref,
                       vx_ref, tq_ref, tp_ref, tn_ref, o_ref, *, l):
    n = pl.program_id(1)
    q = _rope_rows(q_ref, tq_ref, N_SWA_Q, N_SWA_Q // LANE) * SWA_QSCALE
    keys = [kx_ref[:, :N_SWA_KV2],
            _rope_rows(kp_ref, tp_ref, N_SWA_KV2, N_SWA_KV2 // LANE),
            _rope_rows(kc_ref, tq_ref, N_SWA_KV2, N_SWA_KV2 // LANE),
            _rope_rows(kn_ref, tn_ref, N_SWA_KV2, N_SWA_KV2 // LANE)]
    vals = [vx_ref[...], vp_ref[...], vc_ref[...], vn_ref[...]]
    qi = lax.broadcasted_iota(jnp.int32, (SWA_BLOCK, SWA_BLOCK), 0)
    kj = lax.broadcasted_iota(jnp.int32, (SWA_BLOCK, SWA_BLOCK), 1)
    prev_ok = jnp.logical_and(kj - qi >= SWA_BLOCK - SWA_WINDOW, n > 0)
    next_ok = jnp.logical_and(kj - qi <= SWA_WINDOW - SWA_BLOCK, n < SEQ // SWA_BLOCK - 1)
    sinks = [sink_ref[l, h] for h in range(SWA_HEADS)]
    _swa_core(q, keys, vals, [None, prev_ok, None, next_ok], sinks, o_ref)


def _swa_ctx_kernel(sink_ref, q_ref, kx_ref, vx_ref, o_ref, *, l):
    q = q_ref[:, :N_SWA_Q] * SWA_QSCALE
    sinks = [sink_ref[l, h] for h in range(SWA_HEADS)]
    _swa_core(q, [kx_ref[:, :N_SWA_KV2]], [vx_ref[...]], [None], sinks, o_ref)


def _swa_latent(sink, qs2, ks2, vs2, tab, *, l):
    nb = SEQ // SWA_BLOCK
    ctx_blk0 = T_X // CTX_LEN
    w = SWA_BLOCK

    def prev(b, n):
        return b * nb + jnp.maximum(n - 1, 0)

    def nxt(b, n):
        return b * nb + jnp.minimum(n + 1, nb - 1)

    return pl.pallas_call(
        functools.partial(_swa_latent_kernel, l=l),
        out_shape=jax.ShapeDtypeStruct((T_X, N_SWA_Q), F32),
        grid=(BATCH, nb),
        in_specs=[
            pl.BlockSpec(memory_space=pltpu.SMEM),
            pl.BlockSpec((w, 2 * N_SWA_Q), lambda b, n: (b * nb + n, 0)),
            pl.BlockSpec((w, 2 * N_SWA_KV2), lambda b, n: (prev(b, n), 0)),
            pl.BlockSpec((w, 2 * N_SWA_KV2), lambda b, n: (b * nb + n, 0)),
            pl.BlockSpec((w, 2 * N_SWA_KV2), lambda b, n: (nxt(b, n), 0)),
            pl.BlockSpec((CTX_LEN, 2 * N_SWA_KV2), lambda b, n: (ctx_blk0 + b, 0)),
            pl.BlockSpec((w, N_SWA_KV2), lambda b, n: (prev(b, n), 0)),
            pl.BlockSpec((w, N_SWA_KV2), lambda b, n: (b * nb + n, 0)),
            pl.BlockSpec((w, N_SWA_KV2), lambda b, n: (nxt(b, n), 0)),
            pl.BlockSpec((CTX_LEN, N_SWA_KV2), lambda b, n: (ctx_blk0 + b, 0)),
            pl.BlockSpec((w, 2 * LANE), lambda b, n: (n, 0)),
            pl.BlockSpec((w, 2 * LANE), lambda b, n: (jnp.maximum(n - 1, 0), 0)),
            pl.BlockSpec((w, 2 * LANE), lambda b, n: (jnp.minimum(n + 1, nb - 1), 0)),
        ],
        out_specs=pl.BlockSpec((w, N_SWA_Q), lambda b, n: (b * nb + n, 0)),
        compiler_params=_cparams(("parallel", "arbitrary")),
        name="swa_latent",
    )(sink, qs2, ks2, ks2, ks2, ks2, vs2, vs2, vs2, vs2, tab, tab, tab)


def _swa_ctx(sink, qs2, ks2, vs2, *, l):
    w = SWA_BLOCK
    blk0 = T_X // w
    per_b = CTX_LEN // w
    ctx_blk0 = T_X // CTX_LEN
    return pl.pallas_call(
        functools.partial(_swa_ctx_kernel, l=l),
        out_shape=jax.ShapeDtypeStruct((T_C, N_SWA_Q), F32),
        grid=(BATCH, per_b),
        in_specs=[
            pl.BlockSpec(memory_space=pltpu.SMEM),
            pl.BlockSpec((w, 2 * N_SWA_Q), lambda b, n: (blk0 + b * per_b + n, 0)),
            pl.BlockSpec((CTX_LEN, 2 * N_SWA_KV2), lambda b, n: (ctx_blk0 + b, 0)),
            pl.BlockSpec((CTX_LEN, N_SWA_KV2), lambda b, n: (ctx_blk0 + b, 0)),
        ],
        out_specs=pl.BlockSpec((w, N_SWA_Q), lambda b, n: (b * per_b + n, 0)),
        compiler_params=_cparams(("parallel", "arbitrary")),
        name="swa_ctx",
    )(sink, qs2, ks2, vs2)


def _outproj_kernel(h_ref, mod_ref, ssd_ref, mla_ref, mla_c_ref, swa_ref, swa_c_ref, gm_ref, gs_ref,
                    w_ref, o_ref, *, n_lat):
    n_mla = MLA_HEADS * MLA_V
    is_lat = pl.program_id(0) < n_lat
    acc = _dot(ssd_ref[...], w_ref[0:SSD_INNER, :])
    mla = jnp.where(is_lat, mla_ref[...], mla_c_ref[...])
    mla = (_rms(mla) * gm_ref[...]).astype(BF16)
    acc += _dot(mla, w_ref[SSD_INNER:SSD_INNER + n_mla, :])
    swa = jnp.where(is_lat, swa_ref[...], swa_c_ref[...])
    swa = (_rms(swa) * gs_ref[...]).astype(BF16)
    acc += _dot(swa, w_ref[SSD_INNER + n_mla:, :])
    o_ref[...] = h_ref[...] + mod_ref[5:6, :] * acc


def _outproj(h, mod_l, ssd, mla, mla_c, swa, swa_c, gm, gs, w_out, *, l, rows):
    tm = OUT_TM
    n_mla = MLA_HEADS * MLA_V
    n_lat = T_X // tm

    def lat(i):
        return (jnp.minimum(i, n_lat - 1), 0)

    def ctx(i):
        return (jnp.maximum(i - n_lat, 0), 0)

    return pl.pallas_call(
        functools.partial(_outproj_kernel, n_lat=n_lat),
        out_shape=jax.ShapeDtypeStruct((rows, D_MODEL), F32),
        grid=(rows // tm,),
        in_specs=[
            pl.BlockSpec((tm, D_MODEL), lambda i: (i, 0)),
            pl.BlockSpec((None, N_MOD, D_MODEL), lambda i: (_group_of_tile(i, tm), 0, 0)),
            pl.BlockSpec((tm, SSD_INNER), lambda i: (i, 0)),
            pl.BlockSpec((tm, n_mla), lat),
            pl.BlockSpec((tm, n_mla), ctx),
            pl.BlockSpec((tm, N_SWA_Q), lat),
            pl.BlockSpec((tm, N_SWA_Q), ctx),
            pl.BlockSpec((None, 1, n_mla), lambda i: (l, 0, 0)),
            pl.BlockSpec((None, 1, N_SWA_Q), lambda i: (l, 0, 0)),
            pl.BlockSpec((None, D_MIX, D_MODEL), lambda i: (l, 0, 0)),
        ],
        out_specs=pl.BlockSpec((tm, D_MODEL), lambda i: (i, 0)),
        compiler_params=_cparams(("parallel",)),
        name="outproj",
    )(h, mod_l, ssd, mla, mla_c, swa, swa_c, gm, gs, w_out)


def _cast_kernel(x_ref, o_ref):
    o_ref[...] = x_ref[...].astype(BF16)


def _to_bf16(w):
    r, c = w.shape[-2:]
    lead = int(np.prod(w.shape[:-2]))
    out = pl.pallas_call(
        _cast_kernel,
        out_shape=jax.ShapeDtypeStruct((lead, r, c), BF16),
        grid=(lead, r // CAST_ROWS),
        in_specs=[pl.BlockSpec((None, CAST_ROWS, c), lambda a, t: (a, t, 0))],
        out_specs=pl.BlockSpec((None, CAST_ROWS, c), lambda a, t: (a, t, 0)),
        compiler_params=_cparams(("parallel", "parallel")),
        name="to_bf16",
    )(w.reshape(lead, r, c))
    return out.reshape(w.shape)


def _rot_cols(w, dim):
    lead, n = w.shape[:-1], w.shape[-1]
    w5 = w.reshape(lead + (n // dim, 2, 2, dim // 4))
    w5 = lax.rev(w5, (w5.ndim - 2,)) * jnp.asarray([-1.0, 1.0], w.dtype)[:, None]
    return w5.reshape(lead + (n,))


def _dup_heads(w, dim):
    lead, n = w.shape[:-1], w.shape[-1]
    w3 = w.reshape(lead + (n // dim, 1, dim))
    return jnp.broadcast_to(w3, lead + (n // dim, 2, dim)).reshape(lead + (2 * n,))


def _w_in_relayout(w_in):
    offs = np.concatenate([[0], np.cumsum(IN_SIZES)])
    o_z, o_xbc, o_dt, o_cq, o_ckv, o_kr, o_qs, o_ks, o_vs, o_end = (int(o) - W_IN_MAIN for o in offs)
    tail = w_in[..., W_IN_MAIN:]
    qs = tail[..., o_qs:o_ks]
    ks = tail[..., o_ks:o_vs]
    vs = tail[..., o_vs:o_end]
    kr = tail[..., o_kr:o_qs]
    pieces = [tail[..., o_cq:o_kr],
              qs, _rot_cols(qs, SWA_HEADDIM),
              _dup_heads(ks, SWA_HEADDIM), _dup_heads(_rot_cols(ks, SWA_HEADDIM), SWA_HEADDIM),
              _dup_heads(vs, SWA_HEADDIM),
              tail[..., o_dt:o_cq], kr, _rot_cols(kr, MLA_ROPE),
              jnp.zeros(w_in.shape[:-1] + (MLA_ROPE,), w_in.dtype)]
    extra = jnp.concatenate(pieces, axis=-1)
    assert o_dt == 0 and extra.shape[-1] == W_IN_COLS - W_IN_MAIN
    return w_in[..., :W_IN_MAIN].astype(BF16), extra.astype(BF16)


def _mla_w_relayout(w_uq, w_ukv):
    lead = w_uq.shape[:-1]
    q = w_uq.reshape(lead + (MLA_HEADS, MLA_NOPE + MLA_ROPE))
    rope = q[..., MLA_NOPE:]
    wq = jnp.concatenate([q, _rot_cols(rope, MLA_ROPE)], axis=-1).reshape(lead + (MLA_HEADS * LANE,))
    lead = w_ukv.shape[:-1]
    kv = w_ukv.reshape(lead + (MLA_HEADS, MLA_NOPE + MLA_V))
    pad = jnp.zeros(lead + (MLA_HEADS, LANE - MLA_NOPE), w_ukv.dtype)
    wk = jnp.concatenate([kv[..., :MLA_NOPE], pad], axis=-1).reshape(lead + (MLA_HEADS * LANE,))
    wv = jnp.concatenate([kv[..., MLA_NOPE:], pad], axis=-1).reshape(lead + (MLA_HEADS * LANE,))
    return wq.astype(BF16), wk, wv.astype(BF16)


def _k_rope_expand():
    e = np.zeros((LANE, MLA_HEADS * LANE), np.float32)
    for h in range(MLA_HEADS):
        for i in range(MLA_ROPE):
            e[MLA_ROPE + i, h * LANE + MLA_NOPE + i] = 1.0
    return e


def _head_expand():
    e = np.zeros((SSD_HEADS, SSD_INNER), np.float32)
    for h in range(SSD_HEADS):
        e[h, h * SSD_HEADDIM:(h + 1) * SSD_HEADDIM] = 1.0
    return e


def _rope_angles(dim):
    pos = np.arange(SEQ)
    quarter = dim // 4
    inv_freq = ROPE_THETA ** (-jnp.arange(quarter, dtype=F32) / quarter)
    ang_r = jnp.asarray(pos // GRID_W, F32)[:, None] * inv_freq[None, :]
    ang_c = jnp.asarray(pos % GRID_W, F32)[:, None] * inv_freq[None, :]
    ang = jnp.concatenate([ang_r, ang_r, ang_c, ang_c], axis=-1)
    return jnp.cos(ang), jnp.sin(ang)


def _mla_rope_table():
    cos, sin = _rope_angles(MLA_ROPE)
    scale = (MLA_NOPE + MLA_ROPE) ** -0.5 * LOG2E
    n = SEQ + PREP_TM
    ident = jnp.ones((PREP_TM, MLA_ROPE), F32)
    zeros = jnp.zeros((PREP_TM, MLA_ROPE), F32)
    cos = jnp.concatenate([cos, ident], axis=0)
    sin = jnp.concatenate([sin, zeros], axis=0)
    z32 = jnp.zeros((n, MLA_ROPE), F32)
    cq = jnp.concatenate([jnp.ones((n, MLA_NOPE), F32), cos, z32], axis=1) * scale
    sq = jnp.concatenate([jnp.zeros((n, MLA_NOPE), F32), sin, z32], axis=1) * scale
    ck = jnp.concatenate([z32, cos, z32, z32], axis=1)
    sk = jnp.concatenate([z32, sin, z32, z32], axis=1)
    return jnp.concatenate([cq, sq, ck, sk], axis=1)


def _swa_rope_table():
    cos, sin = _rope_angles(SWA_HEADDIM)
    return jnp.concatenate([cos, cos, sin, sin], axis=1)


def kernel(x, c, ctx, c_ctx, w_mod, b_mod, norm_g, ffn_w_gate, ffn_w_up, ffn_w_down,
           w_in, w_out, ssd_conv_w, ssd_conv_b, ssd_dt_bias, ssd_a_log, ssd_d, ssd_norm,
           mla_q_norm, mla_w_uq, mla_kv_norm, mla_w_ukv, mla_out_norm, swa_sink,
           swa_out_norm, final_norm):
    w_main, w_extra = _w_in_relayout(w_in)
    wq, wk, wv = _mla_w_relayout(mla_w_uq, mla_w_ukv)
    k_expand = jnp.broadcast_to(jnp.asarray(_k_rope_expand()), (DEPTH, LANE, MLA_HEADS * LANE))
    wk = jnp.concatenate([wk, k_expand], axis=1).astype(BF16)
    wg = _to_bf16(ffn_w_gate)
    wu = _to_bf16(ffn_w_up)
    wd = _to_bf16(ffn_w_down)
    wo = _to_bf16(w_out)
    mla_tab = _mla_rope_table()
    swa_tab = _swa_rope_table()
    e16 = jnp.asarray(_head_expand(), BF16)
    dskip_e = jnp.repeat(ssd_d, SSD_HEADDIM, axis=-1).reshape(DEPTH, 1, SSD_INNER)
    ssd_gn = ssd_norm.reshape(DEPTH, 1, SSD_INNER)
    conv_b = ssd_conv_b.reshape(DEPTH, 1, SSD_CONV_CH)
    bias_r = ssd_dt_bias.reshape(DEPTH, 2, 1, SSD_HEADS)
    bias_c = ssd_dt_bias.reshape(DEPTH, 2, SSD_HEADS, 1)
    alog_r = ssd_a_log.reshape(DEPTH, 2, 1, SSD_HEADS)
    alog_c = ssd_a_log.reshape(DEPTH, 2, SSD_HEADS, 1)
    gq = mla_q_norm.reshape(DEPTH, 1, MLA_Q_RANK)
    gkv = mla_kv_norm.reshape(DEPTH, 1, MLA_KV_RANK)
    gm = mla_out_norm.reshape(DEPTH, 1, -1)
    gs = swa_out_norm.reshape(DEPTH, 1, -1)
    fin = final_norm.reshape(1, D_MODEL)

    c_all = jnp.concatenate([c, c_ctx[None, :], jnp.zeros((GROUP_PAD - N_GROUPS, D_MODEL), F32)], axis=0)
    mod_all = _modulation(c_all, w_mod, b_mod).reshape(DEPTH, GROUP_PAD, N_MOD, D_MODEL)

    h = x.reshape(T_X, D_MODEL)
    h_ctx = ctx.reshape(T_C, D_MODEL)

    for l in range(DEPTH):
        last = l == DEPTH - 1
        mod_l = mod_all[l]
        h = _ffn(h, h_ctx, mod_l, norm_g[l, 0].reshape(1, D_MODEL), wg, wu, wd, fin,
                 l=l, k=0, base=0, rows=T_ALL)
        h_ctx = None
        z, xbc, cq, ckv, qs2, ks2, vs2, small = _inproj(h, mod_l, norm_g[l, 1].reshape(1, D_MODEL),
                                                       w_main, w_extra, l=l)
        dt2 = small[:, :2 * SSD_HEADS].reshape(T_ALL, 2, SSD_HEADS)
        dt_col = jnp.transpose(dt2, (1, 0, 2))
        dt_row = jnp.transpose(dt2, (1, 2, 0))
        ssd_o = _ssd(xbc, z, dt_col, dt_row, ssd_conv_w, conv_b, bias_r, bias_c, alog_r, alog_c,
                     dskip_e, ssd_gn, e16, l=l)
        q_m, k_m, v_m = _mla_prep(cq, ckv, small, mla_tab, gq, gkv, wq, wk, wv, l=l)
        mla_x = _mla_attn_latent(q_m, k_m, v_m)
        swa_x = _swa_latent(swa_sink, qs2, ks2, vs2, swa_tab, l=l)
        if last:
            rows, mla_c, swa_c = T_X, mla_x, swa_x
        else:
            rows = T_ALL
            mla_c = _mla_attn_ctx(q_m, k_m, v_m)
            swa_c = _swa_ctx(swa_sink, qs2, ks2, vs2, l=l)
        h = _outproj(h, mod_l, ssd_o, mla_x, mla_c, swa_x, swa_c, gm, gs, wo, l=l, rows=rows)
        h = _ffn(h, None, mod_l, norm_g[l, 2].reshape(1, D_MODEL), wg, wu, wd, fin,
                 l=l, k=1, base=6, rows=rows, final_norm=last)
    return h.reshape(BATCH, SEQ, D_MODEL)
```

```python
import functools
import math

import jax
import jax.numpy as jnp
import numpy as np
from jax import lax
from jax.experimental import pallas as pl
from jax.experimental.pallas import tpu as pltpu

F32 = jnp.float32
BF16 = jnp.bfloat16
HIGHEST = lax.Precision.HIGHEST

D_MODEL = 2048
BATCH = 4
SEQ = 4096
DEPTH = 4
GRID_W = 64
CTX_LEN = 256
EPS = 1e-6
ROPE_THETA = 10000.0
N_MOD = 9
D_FF = 5632

SSD_HEADS = 16
SSD_HEADDIM = 64
SSD_INNER = SSD_HEADS * SSD_HEADDIM
SSD_GROUPS = 2
SSD_STATE = 128
SSD_CONV = 5
SSD_CONV_CH = SSD_INNER + 2 * SSD_GROUPS * SSD_STATE

MLA_HEADS = 8
MLA_Q_RANK = 384
MLA_KV_RANK = 256
MLA_NOPE = 64
MLA_ROPE = 32
MLA_V = 64

SWA_HEADS = 8
SWA_KV_HEADS = 2
SWA_HEADDIM = 64
SWA_WINDOW = 128
SWA_BLOCK = 128

D_MIX = SSD_INNER + MLA_HEADS * MLA_V + SWA_HEADS * SWA_HEADDIM
IN_SIZES = (SSD_INNER, SSD_CONV_CH, 2 * SSD_HEADS, MLA_Q_RANK, MLA_KV_RANK, MLA_ROPE,
            SWA_HEADS * SWA_HEADDIM, SWA_KV_HEADS * SWA_HEADDIM, SWA_KV_HEADS * SWA_HEADDIM)

T_X = BATCH * SEQ
T_C = BATCH * CTX_LEN
T_ALL = T_X + T_C
N_GROUPS = BATCH + 1
GROUP_PAD = 8

LANE = 128
SUBLANE = 8
VMEM_LIMIT = 56 * 1024 * 1024
NEG = -1e30
LOG2E = math.log2(math.e)

FFN_TM = 512
FFN_TF = 512
PROJ_TM = 512
OUT_TM = 512
PREP_TM = 512
ATT_TQ = 1024
ATT_TK = 2048
MOD_TN = 1024
CAST_ROWS = 512

SSD_Q = 128
SEQ_CHUNKS = SEQ // SSD_Q
CTX_CHUNKS = CTX_LEN // SSD_Q
SSD_STEPS = SEQ_CHUNKS + CTX_CHUNKS
HALO = 2 * SUBLANE
N_SWA_Q = SWA_HEADS * SWA_HEADDIM
N_SWA_KV2 = 2 * SWA_KV_HEADS * SWA_HEADDIM
SWA_QSCALE = SWA_HEADDIM ** -0.5 * LOG2E

SEG_Z = (0, 1024)
SEG_XBC = (1024, 2560)
SEG_CQ = (2560, 2944)
SEG_CKV = (2944, 3200)
SEG_QS = (3200, 4224)
SEG_KS = (4224, 4736)
SEG_VS = (4736, 4992)
SEG_SMALL = (4992, 5120)
W_IN_COLS = 5120
W_IN_MAIN = SEG_XBC[1]
IN_SEGS = (SEG_Z, SEG_XBC, SEG_CQ, SEG_CKV, SEG_QS, SEG_KS, SEG_VS, SEG_SMALL)


def _cparams(sem):
    return pltpu.CompilerParams(dimension_semantics=sem, vmem_limit_bytes=VMEM_LIMIT)


def _group_of_tile(i, tm):
    return jnp.minimum((i * tm) // SEQ, BATCH)


def _rms(x):
    return x * lax.rsqrt(jnp.mean(x * x, axis=-1, keepdims=True) + EPS)


def _silu(x):
    return x * jax.nn.sigmoid(x)


def _softplus(x):
    return jnp.maximum(x, 0.0) + jnp.log1p(jnp.exp(-jnp.abs(x)))


def _dot(a, b):
    return jnp.dot(a, b, preferred_element_type=F32)


def _dot_t(a, b):
    return lax.dot_general(a, b, (((1,), (1,)), ((), ())), preferred_element_type=F32)


def _mod_kernel(c_ref, w_ref, b_ref, o_ref):
    sc = _silu(c_ref[...]).astype(BF16)
    o_ref[...] = _dot(sc, w_ref[...].astype(BF16)) + b_ref[...]


def _modulation(c_all, w_mod, b_mod):
    n = N_MOD * D_MODEL
    return pl.pallas_call(
        _mod_kernel,
        out_shape=jax.ShapeDtypeStruct((DEPTH, GROUP_PAD, n), F32),
        grid=(DEPTH, n // MOD_TN),
        in_specs=[
            pl.BlockSpec((GROUP_PAD, D_MODEL), lambda l, j: (0, 0)),
            pl.BlockSpec((None, D_MODEL, MOD_TN), lambda l, j: (l, 0, j)),
            pl.BlockSpec((None, 1, MOD_TN), lambda l, j: (l, 0, j)),
        ],
        out_specs=pl.BlockSpec((None, GROUP_PAD, MOD_TN), lambda l, j: (l, 0, j)),
        compiler_params=_cparams(("parallel", "parallel")),
        name="modulation",
    )(c_all, w_mod, b_mod.reshape(DEPTH, 1, n))


def _norm_modulate(x, g_ref, mod_ref, base):
    y = _rms(x) * g_ref[...]
    return (y * (1.0 + mod_ref[base + 1:base + 2, :]) + mod_ref[base:base + 1, :]).astype(BF16)


def _ffn_kernel(*refs, split, base, n_ff, n_lat, final_norm):
    x_ref, xc_ref = (refs[0], refs[1]) if split else (refs[0], None)
    mod_ref, g_ref, wg_ref, wu_ref, wd_ref, fin_ref, o_ref, u_ref = refs[(2 if split else 1):]
    j = pl.program_id(1)

    def residual():
        if xc_ref is None:
            return x_ref[...]
        return jnp.where(pl.program_id(0) < n_lat, x_ref[...], xc_ref[...])

    @pl.when(j == 0)
    def _():
        u_ref[...] = _norm_modulate(residual(), g_ref, mod_ref, base)
        o_ref[...] = jnp.zeros_like(o_ref)

    u = u_ref[...]
    gate = _dot(u, wg_ref[...])
    up = _dot(u, wu_ref[...])
    a = (_silu(gate) * up).astype(BF16)
    o_ref[...] += _dot(a, wd_ref[...])

    @pl.when(j == n_ff - 1)
    def _():
        h = residual() + (0.5 * mod_ref[base + 2:base + 3, :]) * o_ref[...]
        if final_norm:
            h = _rms(h) * fin_ref[...]
        o_ref[...] = h


def _ffn(h, h_ctx, mod_l, g, wg, wu, wd, fin, *, l, k, base, rows, final_norm=False):
    n_ff = D_FF // FFN_TF
    tm = FFN_TM
    n_lat = T_X // tm
    split = h_ctx is not None
    kern = functools.partial(_ffn_kernel, split=split, base=base, n_ff=n_ff, n_lat=n_lat,
                             final_norm=final_norm)
    if split:
        x_specs = [pl.BlockSpec((tm, D_MODEL), lambda i, j: (jnp.minimum(i, n_lat - 1), 0)),
                   pl.BlockSpec((tm, D_MODEL), lambda i, j: (jnp.maximum(i - n_lat, 0), 0))]
        x_args = (h, h_ctx)
    else:
        x_specs = [pl.BlockSpec((tm, D_MODEL), lambda i, j: (i, 0))]
        x_args = (h,)
    return pl.pallas_call(
        kern,
        out_shape=jax.ShapeDtypeStruct((rows, D_MODEL), F32),
        grid=(rows // tm, n_ff),
        in_specs=x_specs + [
            pl.BlockSpec((None, N_MOD, D_MODEL), lambda i, j: (_group_of_tile(i, tm), 0, 0)),
            pl.BlockSpec((1, D_MODEL), lambda i, j: (0, 0)),
            pl.BlockSpec((None, None, D_MODEL, FFN_TF), lambda i, j: (l, k, 0, j)),
            pl.BlockSpec((None, None, D_MODEL, FFN_TF), lambda i, j: (l, k, 0, j)),
            pl.BlockSpec((None, None, FFN_TF, D_MODEL), lambda i, j: (l, k, j, 0)),
            pl.BlockSpec((1, D_MODEL), lambda i, j: (0, 0)),
        ],
        out_specs=pl.BlockSpec((tm, D_MODEL), lambda i, j: (i, 0)),
        scratch_shapes=[pltpu.VMEM((tm, D_MODEL), BF16)],
        compiler_params=_cparams(("parallel", "arbitrary")),
        name="ffn",
    )(*x_args, mod_l, g, wg, wu, wd, fin)


def _inproj_kernel(x_ref, mod_ref, g_ref, wm_ref, we_ref, *o_refs):
    u = _norm_modulate(x_ref[...], g_ref, mod_ref, 3)
    for (a, b), o_ref in zip(IN_SEGS, o_refs):
        if b <= W_IN_MAIN:
            o_ref[...] = _dot(u, wm_ref[:, a:b]).astype(o_ref.dtype)
        else:
            o_ref[...] = _dot(u, we_ref[:, a - W_IN_MAIN:b - W_IN_MAIN]).astype(o_ref.dtype)


def _inproj(h, mod_l, g, w_main, w_extra, *, l):
    tm = PROJ_TM
    return pl.pallas_call(
        _inproj_kernel,
        out_shape=[jax.ShapeDtypeStruct((T_ALL, b - a), F32 if (a, b) == SEG_SMALL else BF16)
                   for a, b in IN_SEGS],
        grid=(T_ALL // tm,),
        in_specs=[
            pl.BlockSpec((tm, D_MODEL), lambda i: (i, 0)),
            pl.BlockSpec((None, N_MOD, D_MODEL), lambda i: (_group_of_tile(i, tm), 0, 0)),
            pl.BlockSpec((1, D_MODEL), lambda i: (0, 0)),
            pl.BlockSpec((None, D_MODEL, W_IN_MAIN), lambda i: (l, 0, 0),
                         pipeline_mode=pl.Buffered(1)),
            pl.BlockSpec((None, D_MODEL, W_IN_COLS - W_IN_MAIN), lambda i: (l, 0, 0),
                         pipeline_mode=pl.Buffered(1)),
        ],
        out_specs=[pl.BlockSpec((tm, b - a), lambda i: (i, 0)) for a, b in IN_SEGS],
        compiler_params=_cparams(("parallel",)),
        name="inproj",
    )(h, mod_l, g, w_main, w_extra)


def _ssd_chunk_ids(phase, s):
    fwd = phase == 1
    is_ctx = s < CTX_CHUNKS
    cc = jnp.where(fwd, s, CTX_CHUNKS - 1 - s)
    lc = jnp.where(fwd, s - CTX_CHUNKS, SSD_STEPS - 1 - s)
    return is_ctx, cc, lc


def _ssd_row_block(b, phase, s):
    is_ctx, cc, lc = _ssd_chunk_ids(phase, s)
    return jnp.where(is_ctx, T_X // SSD_Q + b * CTX_CHUNKS + cc, b * SEQ_CHUNKS + lc)


def _ssd_kernel(xbc_ref, prev_ref, next_ref, z_ref, dtc_ref, dtr_ref, cw_ref, cb_ref,
                bias_r_ref, bias_c_ref, alog_r_ref, alog_c_ref, dskip_ref, gn_ref, e16_ref,
                o_ref, xe_ref, state_ref, ybwd_ref, y_ref, xcs_ref):
    phase = pl.program_id(1)
    s = pl.program_id(2)
    fwd = phase == 1
    is_ctx, cc, lc = _ssd_chunk_ids(phase, s)
    first = jnp.where(is_ctx, cc == 0, lc == 0)
    last = jnp.where(is_ctx, cc == CTX_CHUNKS - 1, lc == SEQ_CHUNKS - 1)
    voff = pl.multiple_of(jnp.where(is_ctx, cc, CTX_CHUNKS + lc) * SSD_Q, SSD_Q)

    @pl.when(s == 0)
    def _():
        state_ref[...] = jnp.zeros_like(state_ref)

    @pl.when(jnp.logical_not(fwd))
    def _():
        xe_ref[0:HALO, :] = jnp.where(first, 0.0, prev_ref[...].astype(F32))
        xe_ref[HALO:HALO + SSD_Q, :] = xbc_ref[...].astype(F32)
        xe_ref[HALO + SSD_Q:, :] = jnp.where(last, 0.0, next_ref[...].astype(F32))
        acc = cb_ref[...] + cw_ref[0:1, :] * xe_ref[HALO - 2:HALO - 2 + SSD_Q, :]
        for k in range(1, SSD_CONV):
            acc = acc + cw_ref[k:k + 1, :] * xe_ref[HALO - 2 + k:HALO - 2 + k + SSD_Q, :]
        xcs_ref[pl.ds(voff, SSD_Q), :] = _silu(acc).astype(BF16)

    xc = xcs_ref[pl.ds(voff, SSD_Q), :]
    xs = xc[:, :SSD_INNER].astype(F32)
    n_bc = SSD_GROUPS * SSD_STATE

    dt_c = _softplus(dtc_ref[...] + bias_r_ref[...])
    dt_r = _softplus(dtr_ref[...] + bias_c_ref[...])
    a_c = dt_c * -jnp.exp(alog_r_ref[...])
    a_r = dt_r * -jnp.exp(alog_c_ref[...])
    row = lax.broadcasted_iota(jnp.int32, (SSD_Q, SSD_Q), 0)
    col = lax.broadcasted_iota(jnp.int32, (SSD_Q, SSD_Q), 1)
    d = (row - col) * jnp.where(fwd, 1, -1)
    causal = d >= 0
    s_c = jnp.dot(causal.astype(F32), a_c, precision=HIGHEST, preferred_element_type=F32)
    s_r = jnp.dot(a_r, (d <= 0).astype(F32), precision=HIGHEST, preferred_element_type=F32)

    e16 = e16_ref[...]
    rem = jnp.concatenate([dt_c, s_c], axis=0)
    expanded = None
    for _ in range(3):
        term = rem.astype(BF16)
        rem = rem - term.astype(F32)
        part = _dot(term, e16)
        expanded = part if expanded is None else expanded + part
    dt_e = expanded[:SSD_Q]
    s_e = expanded[SSD_Q:]
    tot_e = jnp.where(fwd, s_e[SSD_Q - 1:SSD_Q, :], s_e[0:1, :])
    xdt = xs * dt_e
    xdt_end = (xdt * jnp.exp(tot_e - s_e)).astype(BF16)
    xdt16 = xdt.astype(BF16)
    exps_e = jnp.exp(s_e)
    dec_e = jnp.exp(tot_e)
    lane = lax.broadcasted_iota(jnp.int32, (SSD_Q, LANE), 1)

    hp = SSD_INNER // SSD_GROUPS
    heads_per_group = SSD_HEADS // SSD_GROUPS
    for g in range(SSD_GROUPS):
        bg = xc[:, SSD_INNER + g * SSD_STATE:SSD_INNER + (g + 1) * SSD_STATE]
        cg = xc[:, SSD_INNER + n_bc + g * SSD_STATE:SSD_INNER + n_bc + (g + 1) * SSD_STATE]
        bt16 = bg.astype(F32).T.astype(BF16)
        cg16 = cg
        cb = _dot(cg16, bt16)
        st = state_ref[:, g * hp:(g + 1) * hp]
        y_off = _dot(cg16, st.astype(BF16)) * exps_e[:, g * hp:(g + 1) * hp]
        state_ref[:, g * hp:(g + 1) * hp] = (dec_e[:, g * hp:(g + 1) * hp] * st
                                             + _dot(bt16, xdt_end[:, g * hp:(g + 1) * hp]))
        for j in range(heads_per_group // 2):
            h0 = g * heads_per_group + 2 * j
            m = []
            for h in (h0, h0 + 1):
                seg = s_c[:, h:h + 1] - s_r[h:h + 1, :]
                m.append(cb * jnp.exp(jnp.where(causal, seg, NEG)))
            m = jnp.concatenate(m, axis=0).astype(BF16)
            lo = h0 * SSD_HEADDIM
            yp = _dot(m, xdt16[:, lo:lo + LANE])
            y_ref[:, lo:lo + LANE] = (jnp.where(lane < SSD_HEADDIM, yp[:SSD_Q], yp[SSD_Q:])
                                      + y_off[:, lo - g * hp:lo - g * hp + LANE])

    @pl.when(jnp.logical_not(fwd))
    def _():
        ybwd_ref[pl.ds(voff, SSD_Q), :] = y_ref[...]

    @pl.when(fwd)
    def _():
        y = y_ref[...] + ybwd_ref[pl.ds(voff, SSD_Q), :] + dskip_ref[...] * xs
        y = y * _silu(z_ref[...].astype(F32))
        for g in range(SSD_GROUPS):
            yg = y[:, g * hp:(g + 1) * hp]
            o_ref[:, g * hp:(g + 1) * hp] = (_rms(yg) * gn_ref[:, g * hp:(g + 1) * hp]).astype(BF16)


def _ssd(xbc, z, dt_col, dt_row, cw, cb, bias_r, bias_c, alog_r, alog_c, dskip_e, gn, e16, *, l):
    n_halo = T_ALL // HALO
    per_blk = SSD_Q // HALO

    def blk(b, p, s):
        return _ssd_row_block(b, p, s)

    def conv_blk(b, p, s):
        return _ssd_row_block(b, 0, jnp.where(p == 0, s, SSD_STEPS - 1))

    def out_blk(b, p, s):
        return _ssd_row_block(b, 1, jnp.where(p == 1, s, 0))

    return pl.pallas_call(
        _ssd_kernel,
        out_shape=jax.ShapeDtypeStruct((T_ALL, SSD_INNER), BF16),
        grid=(BATCH, 2, SSD_STEPS),
        in_specs=[
            pl.BlockSpec((SSD_Q, SSD_CONV_CH), lambda b, p, s: (conv_blk(b, p, s), 0)),
            pl.BlockSpec((HALO, SSD_CONV_CH),
                         lambda b, p, s: (jnp.maximum(conv_blk(b, p, s) * per_blk - 1, 0), 0)),
            pl.BlockSpec((HALO, SSD_CONV_CH),
                         lambda b, p, s: (jnp.minimum((conv_blk(b, p, s) + 1) * per_blk, n_halo - 1), 0)),
            pl.BlockSpec((SSD_Q, SSD_INNER), lambda b, p, s: (out_blk(b, p, s), 0)),
            pl.BlockSpec((None, SSD_Q, SSD_HEADS), lambda b, p, s: (1 - p, blk(b, p, s), 0)),
            pl.BlockSpec((None, SSD_HEADS, SSD_Q), lambda b, p, s: (1 - p, 0, blk(b, p, s))),
            pl.BlockSpec((None, SSD_CONV, SSD_CONV_CH), lambda b, p, s: (l, 0, 0)),
            pl.BlockSpec((None, 1, SSD_CONV_CH), lambda b, p, s: (l, 0, 0)),
            pl.BlockSpec((None, None, 1, SSD_HEADS), lambda b, p, s: (l, 1 - p, 0, 0)),
            pl.BlockSpec((None, None, SSD_HEADS, 1), lambda b, p, s: (l, 1 - p, 0, 0)),
            pl.BlockSpec((None, None, 1, SSD_HEADS), lambda b, p, s: (l, 1 - p, 0, 0)),
            pl.BlockSpec((None, None, SSD_HEADS, 1), lambda b, p, s: (l, 1 - p, 0, 0)),
            pl.BlockSpec((None, 1, SSD_INNER), lambda b, p, s: (l, 0, 0)),
            pl.BlockSpec((None, 1, SSD_INNER), lambda b, p, s: (l, 0, 0)),
            pl.BlockSpec((SSD_HEADS, SSD_INNER), lambda b, p, s: (0, 0)),
        ],
        out_specs=pl.BlockSpec((SSD_Q, SSD_INNER), lambda b, p, s: (out_blk(b, p, s), 0)),
        scratch_shapes=[
            pltpu.VMEM((SSD_Q + 2 * HALO, SSD_CONV_CH), F32),
            pltpu.VMEM((SSD_STATE, SSD_INNER), F32),
            pltpu.VMEM((SSD_STEPS * SSD_Q, SSD_INNER), F32),
            pltpu.VMEM((SSD_Q, SSD_INNER), F32),
            pltpu.VMEM((SSD_STEPS * SSD_Q, SSD_CONV_CH), BF16),
        ],
        compiler_params=_cparams(("parallel", "arbitrary", "arbitrary")),
        name="ssd",
    )(xbc, xbc, xbc, z, dt_col, dt_row, cw, cb, bias_r, bias_c, alog_r, alog_c, dskip_e, gn, e16)


def _mla_prep_kernel(cq_ref, ckv_ref, sm_ref, tab_ref, gq_ref, gkv_ref, wq_ref, wk_ref, wv_ref,
                     q_ref, k_ref, v_ref):
    cq_tab = tab_ref[:, 0:LANE]
    sq_tab = tab_ref[:, LANE:2 * LANE]
    ck_tab = tab_ref[:, 2 * LANE:3 * LANE]
    sk_tab = tab_ref[:, 3 * LANE:4 * LANE]
    nq = (_rms(cq_ref[...].astype(F32)) * gq_ref[...]).astype(BF16)
    q = _dot(nq, wq_ref[...])
    for h in range(MLA_HEADS):
        qh = q[:, h * LANE:(h + 1) * LANE]
        qh = qh * cq_tab + pltpu.roll(qh, LANE - MLA_ROPE, 1) * sq_tab
        q_ref[:, h * LANE:(h + 1) * LANE] = qh.astype(BF16)
    nkv = _rms(ckv_ref[...].astype(F32)) * gkv_ref[...]
    sm = sm_ref[...]
    kr = sm * ck_tab + pltpu.roll(sm, LANE - MLA_ROPE, 1) * sk_tab
    lhs = jnp.concatenate([nkv, kr], axis=1).astype(BF16)
    k_ref[...] = _dot(lhs, wk_ref[...]).T.astype(BF16)
    v = _dot(lhs[:, :MLA_KV_RANK], wv_ref[...])
    lane = lax.broadcasted_iota(jnp.int32, v.shape, 1)
    v_ref[...] = jnp.where(lane % LANE == MLA_V, 1.0, v).astype(BF16)


def _mla_prep(cq, ckv, small, tab, gq, gkv, wq, wk, wv, *, l):
    tm = PREP_TM
    tiles_per_seq = SEQ // tm

    def tab_idx(i):
        return jnp.where(i < T_X // tm, i % tiles_per_seq, tiles_per_seq)

    return pl.pallas_call(
        _mla_prep_kernel,
        out_shape=[jax.ShapeDtypeStruct((T_ALL, MLA_HEADS * LANE), BF16),
                   jax.ShapeDtypeStruct((MLA_HEADS * LANE, T_ALL), BF16),
                   jax.ShapeDtypeStruct((T_ALL, MLA_HEADS * LANE), BF16)],
        grid=(T_ALL // tm,),
        in_specs=[
            pl.BlockSpec((tm, MLA_Q_RANK), lambda i: (i, 0)),
            pl.BlockSpec((tm, MLA_KV_RANK), lambda i: (i, 0)),
            pl.BlockSpec((tm, LANE), lambda i: (i, 0)),
            pl.BlockSpec((tm, 4 * LANE), lambda i: (tab_idx(i), 0)),
            pl.BlockSpec((None, 1, MLA_Q_RANK), lambda i: (l, 0, 0)),
            pl.BlockSpec((None, 1, MLA_KV_RANK), lambda i: (l, 0, 0)),
            pl.BlockSpec((None, MLA_Q_RANK, MLA_HEADS * LANE), lambda i: (l, 0, 0)),
            pl.BlockSpec((None, MLA_KV_RANK + LANE, MLA_HEADS * LANE), lambda i: (l, 0, 0)),
            pl.BlockSpec((None, MLA_KV_RANK, MLA_HEADS * LANE), lambda i: (l, 0, 0)),
        ],
        out_specs=[pl.BlockSpec((tm, MLA_HEADS * LANE), lambda i: (i, 0)),
                   pl.BlockSpec((MLA_HEADS * LANE, tm), lambda i: (0, i)),
                   pl.BlockSpec((tm, MLA_HEADS * LANE), lambda i: (i, 0))],
        compiler_params=_cparams(("parallel",)),
        name="mla_prep",
    )(cq, ckv, small, tab, gq, gkv, wq, wk, wv)


def _mla_attn_kernel(*refs, n_kv):
    q_ref = refs[0]
    k_refs = refs[1:1 + n_kv]
    v_refs = refs[1 + n_kv:1 + 2 * n_kv]
    o_ref = refs[1 + 2 * n_kv]
    outs = []
    for h in range(2):
        hl = slice(h * LANE, (h + 1) * LANE)
        q = q_ref[:, hl]
        m, acc = None, None
        for k_ref, v_ref in zip(k_refs, v_refs):
            n_keys = k_ref.shape[1]
            for c0 in range(0, n_keys, ATT_TK):
                c1 = min(c0 + ATT_TK, n_keys)
                s = _dot(q, k_ref[hl, c0:c1])
                m_new = s.max(axis=-1, keepdims=True)
                if m is not None:
                    m_new = jnp.maximum(m, m_new)
                    acc = acc * jnp.exp2(m - m_new)
                pv = _dot(jnp.exp2(s - m_new).astype(BF16), v_ref[c0:c1, hl])
                acc = pv if acc is None else acc + pv
                m = m_new
        outs.append(acc / acc[:, MLA_V:MLA_V + 1])
    lane = lax.broadcasted_iota(jnp.int32, outs[0].shape, 1)
    o_ref[...] = jnp.where(lane < MLA_V, outs[0], pltpu.roll(outs[1], MLA_V, 1))


def _mla_attn_latent(q, k, v):
    tq = ATT_TQ
    qt = SEQ // tq
    ctx_blk0 = T_X // CTX_LEN
    return pl.pallas_call(
        functools.partial(_mla_attn_kernel, n_kv=2),
        out_shape=jax.ShapeDtypeStruct((T_X, MLA_HEADS * MLA_V), F32),
        grid=(BATCH, MLA_HEADS // 2, qt),
        in_specs=[
            pl.BlockSpec((tq, 2 * LANE), lambda b, h, t: (b * qt + t, h)),
            pl.BlockSpec((2 * LANE, SEQ), lambda b, h, t: (h, b)),
            pl.BlockSpec((2 * LANE, CTX_LEN), lambda b, h, t: (h, ctx_blk0 + b)),
            pl.BlockSpec((SEQ, 2 * LANE), lambda b, h, t: (b, h)),
            pl.BlockSpec((CTX_LEN, 2 * LANE), lambda b, h, t: (ctx_blk0 + b, h)),
        ],
        out_specs=pl.BlockSpec((tq, LANE), lambda b, h, t: (b * qt + t, h)),
        compiler_params=_cparams(("parallel", "parallel", "arbitrary")),
        name="mla_attn_latent",
    )(q, k, k, v, v)


def _mla_attn_ctx(q, k, v):
    ctx_blk0 = T_X // CTX_LEN
    return pl.pallas_call(
        functools.partial(_mla_attn_kernel, n_kv=1),
        out_shape=jax.ShapeDtypeStruct((T_C, MLA_HEADS * MLA_V), F32),
        grid=(BATCH, MLA_HEADS // 2),
        in_specs=[
            pl.BlockSpec((CTX_LEN, 2 * LANE), lambda b, h: (ctx_blk0 + b, h)),
            pl.BlockSpec((2 * LANE, CTX_LEN), lambda b, h: (h, ctx_blk0 + b)),
            pl.BlockSpec((CTX_LEN, 2 * LANE), lambda b, h: (ctx_blk0 + b, h)),
        ],
        out_specs=pl.BlockSpec((CTX_LEN, LANE), lambda b, h: (b, h)),
        compiler_params=_cparams(("parallel", "parallel")),
        name="mla_attn_ctx",
    )(q, k, v)


def _swa_core(q, keys, vals, masks, sinks, o_ref):
    w = q.shape[0]
    lane = lax.broadcasted_iota(jnp.int32, (w, LANE), 1)
    low = lane < SWA_HEADDIM
    rep = SWA_HEADS // SWA_KV_HEADS
    k_all = jnp.concatenate(keys, axis=0).astype(BF16)
    v_all = jnp.concatenate(vals, axis=0).astype(BF16)
    bias = None
    if any(mask is not None for mask in masks):
        bias = jnp.concatenate(
            [jnp.zeros((w, kb.shape[0]), F32) if mask is None else jnp.where(mask, 0.0, NEG)
             for kb, mask in zip(keys, masks)], axis=1)
        bias = jnp.concatenate([bias] * rep, axis=0)
    for g in range(SWA_KV_HEADS):
        parts = []
        for j in range(rep // 2):
            qp = q[:, (g * rep // 2 + j) * LANE:(g * rep // 2 + j + 1) * LANE]
            parts.append(jnp.where(low, qp, 0.0))
            parts.append(jnp.where(low, 0.0, qp))
        lhs = jnp.concatenate(parts, axis=0).astype(BF16)
        sink = jnp.concatenate([jnp.full((w, 1), sinks[g * rep + r] * LOG2E, F32) for r in range(rep)],
                               axis=0)
        s = _dot_t(lhs, k_all[:, g * LANE:(g + 1) * LANE])
        if bias is not None:
            s = s + bias
        m = jnp.maximum(sink, s.max(axis=-1, keepdims=True))
        p = jnp.exp2(s - m)
        den = jnp.exp2(sink - m) + p.sum(axis=-1, keepdims=True)
        o = _dot(p.astype(BF16), v_all[:, g * LANE:(g + 1) * LANE]) / den
        for j in range(rep // 2):
            o_ref[:, (g * rep // 2 + j) * LANE:(g * rep // 2 + j + 1) * LANE] = jnp.where(
                low, o[2 * j * w:(2 * j + 1) * w], o[(2 * j + 1) * w:(2 * j + 2) * w])


def _rope_rows(x_ref, tab_ref, n, reps):
    cos = jnp.concatenate([tab_ref[:, :LANE]] * reps, axis=1)
    sin = jnp.concatenate([tab_ref[:, LANE:]] * reps, axis=1)
    return x_ref[:, :n].astype(F32) * cos + x_ref[:, n:].astype(F32) * sin


def _swa_latent_kernel(sink_ref, q_ref, kp_ref, kc_ref, kn_ref, kx_ref, vp_ref, vc_ref, vn_ref,
                       vx_ref, tq_ref, tp_ref, tn_ref, o_ref, *, l):
    n = pl.program_id(1)
    q = _rope_rows(q_ref, tq_ref, N_SWA_Q, N_SWA_Q // LANE) * SWA_QSCALE
    keys = [kx_ref[:, :N_SWA_KV2].astype(F32),
            _rope_rows(kp_ref, tp_ref, N_SWA_KV2, N_SWA_KV2 // LANE),
            _rope_rows(kc_ref, tq_ref, N_SWA_KV2, N_SWA_KV2 // LANE),
            _rope_rows(kn_ref, tn_ref, N_SWA_KV2, N_SWA_KV2 // LANE)]
    vals = [vx_ref[...], vp_ref[...], vc_ref[...], vn_ref[...]]
    qi = lax.broadcasted_iota(jnp.int32, (SWA_BLOCK, SWA_BLOCK), 0)
    kj = lax.broadcasted_iota(jnp.int32, (SWA_BLOCK, SWA_BLOCK), 1)
    prev_ok = jnp.logical_and(kj - qi >= SWA_BLOCK - SWA_WINDOW, n > 0)
    next_ok = jnp.logical_and(kj - qi <= SWA_WINDOW - SWA_BLOCK, n < SEQ // SWA_BLOCK - 1)
    sinks = [sink_ref[l, h] for h in range(SWA_HEADS)]
    _swa_core(q, keys, vals, [None, prev_ok, None, next_ok], sinks, o_ref)


def _swa_ctx_kernel(sink_ref, q_ref, kx_ref, vx_ref, o_ref, *, l):
    q = q_ref[:, :N_SWA_Q].astype(F32) * SWA_QSCALE
    sinks = [sink_ref[l, h] for h in range(SWA_HEADS)]
    _swa_core(q, [kx_ref[:, :N_SWA_KV2]], [vx_ref[...]], [None], sinks, o_ref)


def _swa_latent(sink, qs2, ks2, vs2, tab, *, l):
    nb = SEQ // SWA_BLOCK
    ctx_blk0 = T_X // CTX_LEN
    w = SWA_BLOCK

    def prev(b, n):
        return b * nb + jnp.maximum(n - 1, 0)

    def nxt(b, n):
        return b * nb + jnp.minimum(n + 1, nb - 1)

    return pl.pallas_call(
        functools.partial(_swa_latent_kernel, l=l),
        out_shape=jax.ShapeDtypeStruct((T_X, N_SWA_Q), F32),
        grid=(BATCH, nb),
        in_specs=[
            pl.BlockSpec(memory_space=pltpu.SMEM),
            pl.BlockSpec((w, 2 * N_SWA_Q), lambda b, n: (b * nb + n, 0)),
            pl.BlockSpec((w, 2 * N_SWA_KV2), lambda b, n: (prev(b, n), 0)),
            pl.BlockSpec((w, 2 * N_SWA_KV2), lambda b, n: (b * nb + n, 0)),
            pl.BlockSpec((w, 2 * N_SWA_KV2), lambda b, n: (nxt(b, n), 0)),
            pl.BlockSpec((CTX_LEN, 2 * N_SWA_KV2), lambda b, n: (ctx_blk0 + b, 0)),
            pl.BlockSpec((w, N_SWA_KV2), lambda b, n: (prev(b, n), 0)),
            pl.BlockSpec((w, N_SWA_KV2), lambda b, n: (b * nb + n, 0)),
            pl.BlockSpec((w, N_SWA_KV2), lambda b, n: (nxt(b, n), 0)),
            pl.BlockSpec((CTX_LEN, N_SWA_KV2), lambda b, n: (ctx_blk0 + b, 0)),
            pl.BlockSpec((w, 2 * LANE), lambda b, n: (n, 0)),
            pl.BlockSpec((w, 2 * LANE), lambda b, n: (jnp.maximum(n - 1, 0), 0)),
            pl.BlockSpec((w, 2 * LANE), lambda b, n: (jnp.minimum(n + 1, nb - 1), 0)),
        ],
        out_specs=pl.BlockSpec((w, N_SWA_Q), lambda b, n: (b * nb + n, 0)),
        compiler_params=_cparams(("parallel", "arbitrary")),
        name="swa_latent",
    )(sink, qs2, ks2, ks2, ks2, ks2, vs2, vs2, vs2, vs2, tab, tab, tab)


def _swa_ctx(sink, qs2, ks2, vs2, *, l):
    w = SWA_BLOCK
    blk0 = T_X // w
    per_b = CTX_LEN // w
    ctx_blk0 = T_X // CTX_LEN
    return pl.pallas_call(
        functools.partial(_swa_ctx_kernel, l=l),
        out_shape=jax.ShapeDtypeStruct((T_C, N_SWA_Q), F32),
        grid=(BATCH, per_b),
        in_specs=[
            pl.BlockSpec(memory_space=pltpu.SMEM),
            pl.BlockSpec((w, 2 * N_SWA_Q), lambda b, n: (blk0 + b * per_b + n, 0)),
            pl.BlockSpec((CTX_LEN, 2 * N_SWA_KV2), lambda b, n: (ctx_blk0 + b, 0)),
            pl.BlockSpec((CTX_LEN, N_SWA_KV2), lambda b, n: (ctx_blk0 + b, 0)),
        ],
        out_specs=pl.BlockSpec((w, N_SWA_Q), lambda b, n: (b * per_b + n, 0)),
        compiler_params=_cparams(("parallel", "arbitrary")),
        name="swa_ctx",
    )(sink, qs2, ks2, vs2)


def _outproj_kernel(h_ref, mod_ref, ssd_ref, mla_ref, mla_c_ref, swa_ref, swa_c_ref, gm_ref, gs_ref,
                    w_ref, o_ref, *, n_lat):
    n_mla = MLA_HEADS * MLA_V
    is_lat = pl.program_id(0) < n_lat
    acc = _dot(ssd_ref[...], w_ref[0:SSD_INNER, :])
    mla = jnp.where(is_lat, mla_ref[...], mla_c_ref[...])
    mla = (_rms(mla) * gm_ref[...]).astype(BF16)
    acc += _dot(mla, w_ref[SSD_INNER:SSD_INNER + n_mla, :])
    swa = jnp.where(is_lat, swa_ref[...], swa_c_ref[...])
    swa = (_rms(swa) * gs_ref[...]).astype(BF16)
    acc += _dot(swa, w_ref[SSD_INNER + n_mla:, :])
    o_ref[...] = h_ref[...] + mod_ref[5:6, :] * acc


def _outproj(h, mod_l, ssd, mla, mla_c, swa, swa_c, gm, gs, w_out, *, l, rows):
    tm = OUT_TM
    n_mla = MLA_HEADS * MLA_V
    n_lat = T_X // tm

    def lat(i):
        return (jnp.minimum(i, n_lat - 1), 0)

    def ctx(i):
        return (jnp.maximum(i - n_lat, 0), 0)

    return pl.pallas_call(
        functools.partial(_outproj_kernel, n_lat=n_lat),
        out_shape=jax.ShapeDtypeStruct((rows, D_MODEL), F32),
        grid=(rows // tm,),
        in_specs=[
            pl.BlockSpec((tm, D_MODEL), lambda i: (i, 0)),
            pl.BlockSpec((None, N_MOD, D_MODEL), lambda i: (_group_of_tile(i, tm), 0, 0)),
            pl.BlockSpec((tm, SSD_INNER), lambda i: (i, 0)),
            pl.BlockSpec((tm, n_mla), lat),
            pl.BlockSpec((tm, n_mla), ctx),
            pl.BlockSpec((tm, N_SWA_Q), lat),
            pl.BlockSpec((tm, N_SWA_Q), ctx),
            pl.BlockSpec((None, 1, n_mla), lambda i: (l, 0, 0)),
            pl.BlockSpec((None, 1, N_SWA_Q), lambda i: (l, 0, 0)),
            pl.BlockSpec((None, D_MIX, D_MODEL), lambda i: (l, 0, 0)),
        ],
        out_specs=pl.BlockSpec((tm, D_MODEL), lambda i: (i, 0)),
        compiler_params=_cparams(("parallel",)),
        name="outproj",
    )(h, mod_l, ssd, mla, mla_c, swa, swa_c, gm, gs, w_out)


def _cast_kernel(x_ref, o_ref):
    o_ref[...] = x_ref[...].astype(BF16)


def _to_bf16(w):
    r, c = w.shape[-2:]
    lead = int(np.prod(w.shape[:-2]))
    out = pl.pallas_call(
        _cast_kernel,
        out_shape=jax.ShapeDtypeStruct((lead, r, c), BF16),
        grid=(lead, r // CAST_ROWS),
        in_specs=[pl.BlockSpec((None, CAST_ROWS, c), lambda a, t: (a, t, 0))],
        out_specs=pl.BlockSpec((None, CAST_ROWS, c), lambda a, t: (a, t, 0)),
        compiler_params=_cparams(("parallel", "parallel")),
        name="to_bf16",
    )(w.reshape(lead, r, c))
    return out.reshape(w.shape)


def _rot_cols(w, dim):
    lead, n = w.shape[:-1], w.shape[-1]
    w5 = w.reshape(lead + (n // dim, 2, 2, dim // 4))
    w5 = lax.rev(w5, (w5.ndim - 2,)) * jnp.asarray([-1.0, 1.0], w.dtype)[:, None]
    return w5.reshape(lead + (n,))


def _dup_heads(w, dim):
    lead, n = w.shape[:-1], w.shape[-1]
    w3 = w.reshape(lead + (n // dim, 1, dim))
    return jnp.broadcast_to(w3, lead + (n // dim, 2, dim)).reshape(lead + (2 * n,))


def _w_in_relayout(w_in):
    offs = np.concatenate([[0], np.cumsum(IN_SIZES)])
    o_z, o_xbc, o_dt, o_cq, o_ckv, o_kr, o_qs, o_ks, o_vs, o_end = (int(o) - W_IN_MAIN for o in offs)
    tail = w_in[..., W_IN_MAIN:]
    qs = tail[..., o_qs:o_ks]
    ks = tail[..., o_ks:o_vs]
    vs = tail[..., o_vs:o_end]
    kr = tail[..., o_kr:o_qs]
    pieces = [tail[..., o_cq:o_kr],
              qs, _rot_cols(qs, SWA_HEADDIM),
              _dup_heads(ks, SWA_HEADDIM), _dup_heads(_rot_cols(ks, SWA_HEADDIM), SWA_HEADDIM),
              _dup_heads(vs, SWA_HEADDIM),
              tail[..., o_dt:o_cq], kr, _rot_cols(kr, MLA_ROPE),
              jnp.zeros(w_in.shape[:-1] + (MLA_ROPE,), w_in.dtype)]
    extra = jnp.concatenate(pieces, axis=-1)
    assert o_dt == 0 and extra.shape[-1] == W_IN_COLS - W_IN_MAIN
    return w_in[..., :W_IN_MAIN].astype(BF16), extra.astype(BF16)


def _mla_w_relayout(w_uq, w_ukv):
    lead = w_uq.shape[:-1]
    q = w_uq.reshape(lead + (MLA_HEADS, MLA_NOPE + MLA_ROPE))
    rope = q[..., MLA_NOPE:]
    wq = jnp.concatenate([q, _rot_cols(rope, MLA_ROPE)], axis=-1).reshape(lead + (MLA_HEADS * LANE,))
    lead = w_ukv.shape[:-1]
    kv = w_ukv.reshape(lead + (MLA_HEADS, MLA_NOPE + MLA_V))
    pad = jnp.zeros(lead + (MLA_HEADS, LANE - MLA_NOPE), w_ukv.dtype)
    wk = jnp.concatenate([kv[..., :MLA_NOPE], pad], axis=-1).reshape(lead + (MLA_HEADS * LANE,))
    wv = jnp.concatenate([kv[..., MLA_NOPE:], pad], axis=-1).reshape(lead + (MLA_HEADS * LANE,))
    return wq.astype(BF16), wk, wv.astype(BF16)


def _k_rope_expand():
    e = np.zeros((LANE, MLA_HEADS * LANE), np.float32)
    for h in range(MLA_HEADS):
        for i in range(MLA_ROPE):
            e[MLA_ROPE + i, h * LANE + MLA_NOPE + i] = 1.0
    return e


def _head_expand():
    e = np.zeros((SSD_HEADS, SSD_INNER), np.float32)
    for h in range(SSD_HEADS):
        e[h, h * SSD_HEADDIM:(h + 1) * SSD_HEADDIM] = 1.0
    return e


def _rope_angles(dim):
    pos = np.arange(SEQ)
    quarter = dim // 4
    inv_freq = ROPE_THETA ** (-jnp.arange(quarter, dtype=F32) / quarter)
    ang_r = jnp.asarray(pos // GRID_W, F32)[:, None] * inv_freq[None, :]
    ang_c = jnp.asarray(pos % GRID_W, F32)[:, None] * inv_freq[None, :]
    ang = jnp.concatenate([ang_r, ang_r, ang_c, ang_c], axis=-1)
    return jnp.cos(ang), jnp.sin(ang)


def _mla_rope_table():
    cos, sin = _rope_angles(MLA_ROPE)
    scale = (MLA_NOPE + MLA_ROPE) ** -0.5 * LOG2E
    n = SEQ + PREP_TM
    ident = jnp.ones((PREP_TM, MLA_ROPE), F32)
    zeros = jnp.zeros((PREP_TM, MLA_ROPE), F32)
    cos = jnp.concatenate([cos, ident], axis=0)
    sin = jnp.concatenate([sin, zeros], axis=0)
    z32 = jnp.zeros((n, MLA_ROPE), F32)
    cq = jnp.concatenate([jnp.ones((n, MLA_NOPE), F32), cos, z32], axis=1) * scale
    sq = jnp.concatenate([jnp.zeros((n, MLA_NOPE), F32), sin, z32], axis=1) * scale
    ck = jnp.concatenate([z32, cos, z32, z32], axis=1)
    sk = jnp.concatenate([z32, sin, z32, z32], axis=1)
    return jnp.concatenate([cq, sq, ck, sk], axis=1)


def _swa_rope_table():
    cos, sin = _rope_angles(SWA_HEADDIM)
    return jnp.concatenate([cos, cos, sin, sin], axis=1)


def kernel(x, c, ctx, c_ctx, w_mod, b_mod, norm_g, ffn_w_gate, ffn_w_up, ffn_w_down,
           w_in, w_out, ssd_conv_w, ssd_conv_b, ssd_dt_bias, ssd_a_log, ssd_d, ssd_norm,
           mla_q_norm, mla_w_uq, mla_kv_norm, mla_w_ukv, mla_out_norm, swa_sink,
           swa_out_norm, final_norm):
    w_main, w_extra = _w_in_relayout(w_in)
    wq, wk, wv = _mla_w_relayout(mla_w_uq, mla_w_ukv)
    k_expand = jnp.broadcast_to(jnp.asarray(_k_rope_expand()), (DEPTH, LANE, MLA_HEADS * LANE))
    wk = jnp.concatenate([wk, k_expand], axis=1).astype(BF16)
    wg = _to_bf16(ffn_w_gate)
    wu = _to_bf16(ffn_w_up)
    wd = _to_bf16(ffn_w_down)
    wo = _to_bf16(w_out)
    mla_tab = _mla_rope_table()
    swa_tab = _swa_rope_table()
    e16 = jnp.asarray(_head_expand(), BF16)
    dskip_e = jnp.repeat(ssd_d, SSD_HEADDIM, axis=-1).reshape(DEPTH, 1, SSD_INNER)
    ssd_gn = ssd_norm.reshape(DEPTH, 1, SSD_INNER)
    conv_b = ssd_conv_b.reshape(DEPTH, 1, SSD_CONV_CH)
    bias_r = ssd_dt_bias.reshape(DEPTH, 2, 1, SSD_HEADS)
    bias_c = ssd_dt_bias.reshape(DEPTH, 2, SSD_HEADS, 1)
    alog_r = ssd_a_log.reshape(DEPTH, 2, 1, SSD_HEADS)
    alog_c = ssd_a_log.reshape(DEPTH, 2, SSD_HEADS, 1)
    gq = mla_q_norm.reshape(DEPTH, 1, MLA_Q_RANK)
    gkv = mla_kv_norm.reshape(DEPTH, 1, MLA_KV_RANK)
    gm = mla_out_norm.reshape(DEPTH, 1, -1)
    gs = swa_out_norm.reshape(DEPTH, 1, -1)
    fin = final_norm.reshape(1, D_MODEL)

    c_all = jnp.concatenate([c, c_ctx[None, :], jnp.zeros((GROUP_PAD - N_GROUPS, D_MODEL), F32)], axis=0)
    mod_all = _modulation(c_all, w_mod, b_mod).reshape(DEPTH, GROUP_PAD, N_MOD, D_MODEL)

    h = x.reshape(T_X, D_MODEL)
    h_ctx = ctx.reshape(T_C, D_MODEL)

    for l in range(DEPTH):
        last = l == DEPTH - 1
        mod_l = mod_all[l]
        h = _ffn(h, h_ctx, mod_l, norm_g[l, 0].reshape(1, D_MODEL), wg, wu, wd, fin,
                 l=l, k=0, base=0, rows=T_ALL)
        h_ctx = None
        z, xbc, cq, ckv, qs2, ks2, vs2, small = _inproj(h, mod_l, norm_g[l, 1].reshape(1, D_MODEL),
                                                       w_main, w_extra, l=l)
        dt2 = small[:, :2 * SSD_HEADS].reshape(T_ALL, 2, SSD_HEADS)
        dt_col = jnp.transpose(dt2, (1, 0, 2))
        dt_row = jnp.transpose(dt2, (1, 2, 0))
        ssd_o = _ssd(xbc, z, dt_col, dt_row, ssd_conv_w, conv_b, bias_r, bias_c, alog_r, alog_c,
                     dskip_e, ssd_gn, e16, l=l)
        q_m, k_m, v_m = _mla_prep(cq, ckv, small, mla_tab, gq, gkv, wq, wk, wv, l=l)
        mla_x = _mla_attn_latent(q_m, k_m, v_m)
        swa_x = _swa_latent(swa_sink, qs2, ks2, vs2, swa_tab, l=l)
        if last:
            rows, mla_c, swa_c = T_X, mla_x, swa_x
        else:
            rows = T_ALL
            mla_c = _mla_attn_ctx(q_m, k_m, v_m)
            swa_c = _swa_ctx(swa_sink, qs2, ks2, vs2, l=l)
        h = _outproj(h, mod_l, ssd_o, mla_x, mla_c, swa_x, swa_c, gm, gs, wo, l=l, rows=rows)
        h = _ffn(h, None, mod_l, norm_g[l, 2].reshape(1, D_MODEL), wg, wu, wd, fin,
                 l=l, k=1, base=6, rows=rows, final_norm=last)
    return h.reshape(BATCH, SEQ, D_MODEL)
```

```python
import functools
import math

import jax
import jax.numpy as jnp
import numpy as np
from jax import lax
from jax.experimental import pallas as pl
from jax.experimental.pallas import tpu as pltpu

F32 = jnp.float32
BF16 = jnp.bfloat16
HIGHEST = lax.Precision.HIGHEST

D_MODEL = 2048
BATCH = 4
SEQ = 4096
DEPTH = 4
GRID_W = 64
CTX_LEN = 256
EPS = 1e-6
ROPE_THETA = 10000.0
N_MOD = 9
D_FF = 5632

SSD_HEADS = 16
SSD_HEADDIM = 64
SSD_INNER = SSD_HEADS * SSD_HEADDIM
SSD_GROUPS = 2
SSD_STATE = 128
SSD_CONV = 5
SSD_CONV_CH = SSD_INNER + 2 * SSD_GROUPS * SSD_STATE

MLA_HEADS = 8
MLA_Q_RANK = 384
MLA_KV_RANK = 256
MLA_NOPE = 64
MLA_ROPE = 32
MLA_V = 64

SWA_HEADS = 8
SWA_KV_HEADS = 2
SWA_HEADDIM = 64
SWA_WINDOW = 128
SWA_BLOCK = 128

D_MIX = SSD_INNER + MLA_HEADS * MLA_V + SWA_HEADS * SWA_HEADDIM
IN_SIZES = (SSD_INNER, SSD_CONV_CH, 2 * SSD_HEADS, MLA_Q_RANK, MLA_KV_RANK, MLA_ROPE,
            SWA_HEADS * SWA_HEADDIM, SWA_KV_HEADS * SWA_HEADDIM, SWA_KV_HEADS * SWA_HEADDIM)

T_X = BATCH * SEQ
T_C = BATCH * CTX_LEN
T_ALL = T_X + T_C
N_GROUPS = BATCH + 1
GROUP_PAD = 8

LANE = 128
SUBLANE = 8
VMEM_LIMIT = 56 * 1024 * 1024
NEG = -1e30
LOG2E = math.log2(math.e)

FFN_TM = 512
FFN_TF = 512
PROJ_TM = 512
OUT_TM = 512
PREP_TM = 512
ATT_TQ = 1024
ATT_TK = 2048
MOD_TN = 1024
CAST_ROWS = 512

SSD_Q = 128
SEQ_CHUNKS = SEQ // SSD_Q
CTX_CHUNKS = CTX_LEN // SSD_Q
SSD_STEPS = SEQ_CHUNKS + CTX_CHUNKS
HALO = 2 * SUBLANE
N_SWA_Q = SWA_HEADS * SWA_HEADDIM
N_SWA_KV2 = 2 * SWA_KV_HEADS * SWA_HEADDIM
SWA_QSCALE = SWA_HEADDIM ** -0.5 * LOG2E

SEG_Z = (0, 1024)
SEG_XBC = (1024, 2560)
SEG_CQ = (2560, 2944)
SEG_CKV = (2944, 3200)
SEG_QS = (3200, 4224)
SEG_KS = (4224, 4736)
SEG_VS = (4736, 4992)
SEG_SMALL = (4992, 5120)
W_IN_COLS = 5120
W_IN_MAIN = SEG_XBC[1]
IN_SEGS = (SEG_Z, SEG_XBC, SEG_CQ, SEG_CKV, SEG_QS, SEG_KS, SEG_VS, SEG_SMALL)


def _cparams(sem):
    return pltpu.CompilerParams(dimension_semantics=sem, vmem_limit_bytes=VMEM_LIMIT)


def _group_of_tile(i, tm):
    return jnp.minimum((i * tm) // SEQ, BATCH)


def _rms(x):
    return x * lax.rsqrt(jnp.mean(x * x, axis=-1, keepdims=True) + EPS)


def _silu(x):
    return x * jax.nn.sigmoid(x)


def _softplus(x):
    return jnp.maximum(x, 0.0) + jnp.log1p(jnp.exp(-jnp.abs(x)))


def _dot(a, b):
    return jnp.dot(a, b, preferred_element_type=F32)


def _dot_t(a, b):
    return lax.dot_general(a, b, (((1,), (1,)), ((), ())), preferred_element_type=F32)


def _mod_kernel(c_ref, w_ref, b_ref, o_ref):
    sc = _silu(c_ref[...]).astype(BF16)
    o_ref[...] = _dot(sc, w_ref[...].astype(BF16)) + b_ref[...]


def _modulation(c_all, w_mod, b_mod):
    n = N_MOD * D_MODEL
    return pl.pallas_call(
        _mod_kernel,
        out_shape=jax.ShapeDtypeStruct((DEPTH, GROUP_PAD, n), F32),
        grid=(DEPTH, n // MOD_TN),
        in_specs=[
            pl.BlockSpec((GROUP_PAD, D_MODEL), lambda l, j: (0, 0)),
            pl.BlockSpec((None, D_MODEL, MOD_TN), lambda l, j: (l, 0, j)),
            pl.BlockSpec((None, 1, MOD_TN), lambda l, j: (l, 0, j)),
        ],
        out_specs=pl.BlockSpec((None, GROUP_PAD, MOD_TN), lambda l, j: (l, 0, j)),
        compiler_params=_cparams(("parallel", "parallel")),
        name="modulation",
    )(c_all, w_mod, b_mod.reshape(DEPTH, 1, n))


def _norm_modulate(x, g_ref, mod_ref, base):
    y = _rms(x) * g_ref[...]
    return (y * (1.0 + mod_ref[base + 1:base + 2, :]) + mod_ref[base:base + 1, :]).astype(BF16)


def _ffn_kernel(*refs, split, base, n_ff, n_lat, final_norm):
    x_ref, xc_ref = (refs[0], refs[1]) if split else (refs[0], None)
    mod_ref, g_ref, wg_ref, wu_ref, wd_ref, fin_ref, o_ref, u_ref = refs[(2 if split else 1):]
    j = pl.program_id(1)

    def residual():
        if xc_ref is None:
            return x_ref[...]
        return jnp.where(pl.program_id(0) < n_lat, x_ref[...], xc_ref[...])

    @pl.when(j == 0)
    def _():
        u_ref[...] = _norm_modulate(residual(), g_ref, mod_ref, base)
        o_ref[...] = jnp.zeros_like(o_ref)

    u = u_ref[...]
    gate = _dot(u, wg_ref[...])
    up = _dot(u, wu_ref[...])
    a = (_silu(gate) * up).astype(BF16)
    o_ref[...] += _dot(a, wd_ref[...])

    @pl.when(j == n_ff - 1)
    def _():
        h = residual() + (0.5 * mod_ref[base + 2:base + 3, :]) * o_ref[...]
        if final_norm:
            h = _rms(h) * fin_ref[...]
        o_ref[...] = h


def _ffn(h, h_ctx, mod_l, g, wg, wu, wd, fin, *, l, k, base, rows, final_norm=False):
    n_ff = D_FF // FFN_TF
    tm = FFN_TM
    n_lat = T_X // tm
    split = h_ctx is not None
    kern = functools.partial(_ffn_kernel, split=split, base=base, n_ff=n_ff, n_lat=n_lat,
                             final_norm=final_norm)
    if split:
        x_specs = [pl.BlockSpec((tm, D_MODEL), lambda i, j: (jnp.minimum(i, n_lat - 1), 0)),
                   pl.BlockSpec((tm, D_MODEL), lambda i, j: (jnp.maximum(i - n_lat, 0), 0))]
        x_args = (h, h_ctx)
    else:
        x_specs = [pl.BlockSpec((tm, D_MODEL), lambda i, j: (i, 0))]
        x_args = (h,)
    return pl.pallas_call(
        kern,
        out_shape=jax.ShapeDtypeStruct((rows, D_MODEL), F32),
        grid=(rows // tm, n_ff),
        in_specs=x_specs + [
            pl.BlockSpec((None, N_MOD, D_MODEL), lambda i, j: (_group_of_tile(i, tm), 0, 0)),
            pl.BlockSpec((1, D_MODEL), lambda i, j: (0, 0)),
            pl.BlockSpec((None, None, D_MODEL, FFN_TF), lambda i, j: (l, k, 0, j)),
            pl.BlockSpec((None, None, D_MODEL, FFN_TF), lambda i, j: (l, k, 0, j)),
            pl.BlockSpec((None, None, FFN_TF, D_MODEL), lambda i, j: (l, k, j, 0)),
            pl.BlockSpec((1, D_MODEL), lambda i, j: (0, 0)),
        ],
        out_specs=pl.BlockSpec((tm, D_MODEL), lambda i, j: (i, 0)),
        scratch_shapes=[pltpu.VMEM((tm, D_MODEL), BF16)],
        compiler_params=_cparams(("parallel", "arbitrary")),
        name="ffn",
    )(*x_args, mod_l, g, wg, wu, wd, fin)


def _inproj_kernel(x_ref, mod_ref, g_ref, wm_ref, we_ref, *o_refs):
    u = _norm_modulate(x_ref[...], g_ref, mod_ref, 3)
    for (a, b), o_ref in zip(IN_SEGS, o_refs):
        if b <= W_IN_MAIN:
            o_ref[...] = _dot(u, wm_ref[:, a:b]).astype(o_ref.dtype)
        else:
            o_ref[...] = _dot(u, we_ref[:, a - W_IN_MAIN:b - W_IN_MAIN]).astype(o_ref.dtype)


def _inproj(h, mod_l, g, w_main, w_extra, *, l):
    tm = PROJ_TM
    return pl.pallas_call(
        _inproj_kernel,
        out_shape=[jax.ShapeDtypeStruct((T_ALL, b - a), F32 if (a, b) == SEG_SMALL else BF16)
                   for a, b in IN_SEGS],
        grid=(T_ALL // tm,),
        in_specs=[
            pl.BlockSpec((tm, D_MODEL), lambda i: (i, 0)),
            pl.BlockSpec((None, N_MOD, D_MODEL), lambda i: (_group_of_tile(i, tm), 0, 0)),
            pl.BlockSpec((1, D_MODEL), lambda i: (0, 0)),
            pl.BlockSpec((None, D_MODEL, W_IN_MAIN), lambda i: (l, 0, 0),
                         pipeline_mode=pl.Buffered(1)),
            pl.BlockSpec((None, D_MODEL, W_IN_COLS - W_IN_MAIN), lambda i: (l, 0, 0),
                         pipeline_mode=pl.Buffered(1)),
        ],
        out_specs=[pl.BlockSpec((tm, b - a), lambda i: (i, 0)) for a, b in IN_SEGS],
        compiler_params=_cparams(("parallel",)),
        name="inproj",
    )(h, mod_l, g, w_main, w_extra)


def _ssd_chunk_ids(phase, s):
    fwd = phase == 1
    is_ctx = s < CTX_CHUNKS
    cc = jnp.where(fwd, s, CTX_CHUNKS - 1 - s)
    lc = jnp.where(fwd, s - CTX_CHUNKS, SSD_STEPS - 1 - s)
    return is_ctx, cc, lc


def _ssd_row_block(b, phase, s):
    is_ctx, cc, lc = _ssd_chunk_ids(phase, s)
    return jnp.where(is_ctx, T_X // SSD_Q + b * CTX_CHUNKS + cc, b * SEQ_CHUNKS + lc)


def _ssd_kernel(xbc_ref, prev_ref, next_ref, z_ref, dtc_ref, dtr_ref, cw_ref, cb_ref,
                bias_r_ref, bias_c_ref, alog_r_ref, alog_c_ref, dskip_ref, gn_ref, e16_ref,
                o_ref, xe_ref, state_ref, ybwd_ref, y_ref, xcs_ref):
    phase = pl.program_id(1)
    s = pl.program_id(2)
    fwd = phase == 1
    is_ctx, cc, lc = _ssd_chunk_ids(phase, s)
    first = jnp.where(is_ctx, cc == 0, lc == 0)
    last = jnp.where(is_ctx, cc == CTX_CHUNKS - 1, lc == SEQ_CHUNKS - 1)
    voff = pl.multiple_of(jnp.where(is_ctx, cc, CTX_CHUNKS + lc) * SSD_Q, SSD_Q)

    @pl.when(s == 0)
    def _():
        state_ref[...] = jnp.zeros_like(state_ref)

    @pl.when(jnp.logical_not(fwd))
    def _():
        xe_ref[0:HALO, :] = jnp.where(first, 0.0, prev_ref[...].astype(F32))
        xe_ref[HALO:HALO + SSD_Q, :] = xbc_ref[...].astype(F32)
        xe_ref[HALO + SSD_Q:, :] = jnp.where(last, 0.0, next_ref[...].astype(F32))
        acc = cb_ref[...] + cw_ref[0:1, :] * xe_ref[HALO - 2:HALO - 2 + SSD_Q, :]
        for k in range(1, SSD_CONV):
            acc = acc + cw_ref[k:k + 1, :] * xe_ref[HALO - 2 + k:HALO - 2 + k + SSD_Q, :]
        xcs_ref[pl.ds(voff, SSD_Q), :] = _silu(acc).astype(BF16)

    xc = xcs_ref[pl.ds(voff, SSD_Q), :]
    xs = xc[:, :SSD_INNER].astype(F32)
    n_bc = SSD_GROUPS * SSD_STATE

    dt_c = _softplus(dtc_ref[...] + bias_r_ref[...])
    dt_r = _softplus(dtr_ref[...] + bias_c_ref[...])
    a_c = dt_c * -jnp.exp(alog_r_ref[...])
    a_r = dt_r * -jnp.exp(alog_c_ref[...])
    row = lax.broadcasted_iota(jnp.int32, (SSD_Q, SSD_Q), 0)
    col = lax.broadcasted_iota(jnp.int32, (SSD_Q, SSD_Q), 1)
    d = (row - col) * jnp.where(fwd, 1, -1)
    causal = d >= 0
    s_c = jnp.dot(causal.astype(F32), a_c, precision=HIGHEST, preferred_element_type=F32)
    s_r = jnp.dot(a_r, (d <= 0).astype(F32), precision=HIGHEST, preferred_element_type=F32)

    e16 = e16_ref[...]
    rem = jnp.concatenate([dt_c, s_c], axis=0)
    expanded = None
    for _ in range(3):
        term = rem.astype(BF16)
        rem = rem - term.astype(F32)
        part = _dot(term, e16)
        expanded = part if expanded is None else expanded + part
    dt_e = expanded[:SSD_Q]
    s_e = expanded[SSD_Q:]
    tot_e = jnp.where(fwd, s_e[SSD_Q - 1:SSD_Q, :], s_e[0:1, :])
    xdt = xs * dt_e
    xdt_end = (xdt * jnp.exp(tot_e - s_e)).astype(BF16)
    xdt16 = xdt.astype(BF16)
    exps_e = jnp.exp(s_e)
    dec_e = jnp.exp(tot_e)
    lane = lax.broadcasted_iota(jnp.int32, (SSD_Q, LANE), 1)

    hp = SSD_INNER // SSD_GROUPS
    heads_per_group = SSD_HEADS // SSD_GROUPS
    for g in range(SSD_GROUPS):
        bg = xc[:, SSD_INNER + g * SSD_STATE:SSD_INNER + (g + 1) * SSD_STATE]
        cg = xc[:, SSD_INNER + n_bc + g * SSD_STATE:SSD_INNER + n_bc + (g + 1) * SSD_STATE]
        bt16 = bg.astype(F32).T.astype(BF16)
        cg16 = cg
        cb = _dot(cg16, bt16)
        st = state_ref[:, g * hp:(g + 1) * hp]
        y_off = _dot(cg16, st.astype(BF16)) * exps_e[:, g * hp:(g + 1) * hp]
        state_ref[:, g * hp:(g + 1) * hp] = (dec_e[:, g * hp:(g + 1) * hp] * st
                                             + _dot(bt16, xdt_end[:, g * hp:(g + 1) * hp]))
        for j in range(heads_per_group // 2):
            h0 = g * heads_per_group + 2 * j
            m = []
            for h in (h0, h0 + 1):
                seg = s_c[:, h:h + 1] - s_r[h:h + 1, :]
                m.append(cb * jnp.exp(jnp.where(causal, seg, NEG)))
            m = jnp.concatenate(m, axis=0).astype(BF16)
            lo = h0 * SSD_HEADDIM
            yp = _dot(m, xdt16[:, lo:lo + LANE])
            y_ref[:, lo:lo + LANE] = (jnp.where(lane < SSD_HEADDIM, yp[:SSD_Q], yp[SSD_Q:])
                                      + y_off[:, lo - g * hp:lo - g * hp + LANE])

    @pl.when(jnp.logical_not(fwd))
    def _():
        ybwd_ref[pl.ds(voff, SSD_Q), :] = y_ref[...]

    @pl.when(fwd)
    def _():
        y = y_ref[...] + ybwd_ref[pl.ds(voff, SSD_Q), :] + dskip_ref[...] * xs
        y = y * _silu(z_ref[...].astype(F32))
        for g in range(SSD_GROUPS):
            yg = y[:, g * hp:(g + 1) * hp]
            o_ref[:, g * hp:(g + 1) * hp] = (_rms(yg) * gn_ref[:, g * hp:(g + 1) * hp]).astype(BF16)


def _ssd(xbc, z, dt_col, dt_row, cw, cb, bias_r, bias_c, alog_r, alog_c, dskip_e, gn, e16, *, l):
    n_halo = T_ALL // HALO
    per_blk = SSD_Q // HALO

    def blk(b, p, s):
        return _ssd_row_block(b, p, s)

    def conv_blk(b, p, s):
        return _ssd_row_block(b, 0, jnp.where(p == 0, s, SSD_STEPS - 1))

    def out_blk(b, p, s):
        return _ssd_row_block(b, 1, jnp.where(p == 1, s, 0))

    return pl.pallas_call(
        _ssd_kernel,
        out_shape=jax.ShapeDtypeStruct((T_ALL, SSD_INNER), BF16),
        grid=(BATCH, 2, SSD_STEPS),
        in_specs=[
            pl.BlockSpec((SSD_Q, SSD_CONV_CH), lambda b, p, s: (conv_blk(b, p, s), 0)),
            pl.BlockSpec((HALO, SSD_CONV_CH),
                         lambda b, p, s: (jnp.maximum(conv_blk(b, p, s) * per_blk - 1, 0), 0)),
            pl.BlockSpec((HALO, SSD_CONV_CH),
                         lambda b, p, s: (jnp.minimum((conv_blk(b, p, s) + 1) * per_blk, n_halo - 1), 0)),
            pl.BlockSpec((SSD_Q, SSD_INNER), lambda b, p, s: (out_blk(b, p, s), 0)),
            pl.BlockSpec((None, SSD_Q, SSD_HEADS), lambda b, p, s: (1 - p, blk(b, p, s), 0)),
            pl.BlockSpec((None, SSD_HEADS, SSD_Q), lambda b, p, s: (1 - p, 0, blk(b, p, s))),
            pl.BlockSpec((None, SSD_CONV, SSD_CONV_CH), lambda b, p, s: (l, 0, 0)),
            pl.BlockSpec((None, 1, SSD_CONV_CH), lambda b, p, s: (l, 0, 0)),
            pl.BlockSpec((None, None, 1, SSD_HEADS), lambda b, p, s: (l, 1 - p, 0, 0)),
            pl.BlockSpec((None, None, SSD_HEADS, 1), lambda b, p, s: (l, 1 - p, 0, 0)),
            pl.BlockSpec((None, None, 1, SSD_HEADS), lambda b, p, s: (l, 1 - p, 0, 0)),
            pl.BlockSpec((None, None, SSD_HEADS, 1), lambda b, p, s: (l, 1 - p, 0, 0)),
            pl.BlockSpec((None, 1, SSD_INNER), lambda b, p, s: (l, 0, 0)),
            pl.BlockSpec((None, 1, SSD_INNER), lambda b, p, s: (l, 0, 0)),
            pl.BlockSpec((SSD_HEADS, SSD_INNER), lambda b, p, s: (0, 0)),
        ],
        out_specs=pl.BlockSpec((SSD_Q, SSD_INNER), lambda b, p, s: (out_blk(b, p, s), 0)),
        scratch_shapes=[
            pltpu.VMEM((SSD_Q + 2 * HALO, SSD_CONV_CH), F32),
            pltpu.VMEM((SSD_STATE, SSD_INNER), F32),
            pltpu.VMEM((SSD_STEPS * SSD_Q, SSD_INNER), F32),
            pltpu.VMEM((SSD_Q, SSD_INNER), F32),
            pltpu.VMEM((SSD_STEPS * SSD_Q, SSD_CONV_CH), BF16),
        ],
        compiler_params=_cparams(("parallel", "arbitrary", "arbitrary")),
        name="ssd",
    )(xbc, xbc, xbc, z, dt_col, dt_row, cw, cb, bias_r, bias_c, alog_r, alog_c, dskip_e, gn, e16)


def _mla_prep_kernel(cq_ref, ckv_ref, sm_ref, tab_ref, gq_ref, gkv_ref, wq_ref, wk_ref, wv_ref,
                     q_ref, k_ref, v_ref):
    cq_tab = tab_ref[:, 0:LANE]
    sq_tab = tab_ref[:, LANE:2 * LANE]
    ck_tab = tab_ref[:, 2 * LANE:3 * LANE]
    sk_tab = tab_ref[:, 3 * LANE:4 * LANE]
    nq = (_rms(cq_ref[...].astype(F32)) * gq_ref[...]).astype(BF16)
    q = _dot(nq, wq_ref[...])
    for h in range(MLA_HEADS):
        qh = q[:, h * LANE:(h + 1) * LANE]
        qh = qh * cq_tab + pltpu.roll(qh, LANE - MLA_ROPE, 1) * sq_tab
        q_ref[:, h * LANE:(h + 1) * LANE] = qh.astype(BF16)
    nkv = _rms(ckv_ref[...].astype(F32)) * gkv_ref[...]
    sm = sm_ref[...]
    kr = sm * ck_tab + pltpu.roll(sm, LANE - MLA_ROPE, 1) * sk_tab
    lhs = jnp.concatenate([nkv, kr], axis=1).astype(BF16)
    k_ref[...] = _dot(lhs, wk_ref[...]).T.astype(BF16)
    v = _dot(lhs[:, :MLA_KV_RANK], wv_ref[...])
    lane = lax.broadcasted_iota(jnp.int32, v.shape, 1)
    v_ref[...] = jnp.where(lane % LANE == MLA_V, 1.0, v).astype(BF16)


def _mla_prep(cq, ckv, small, tab, gq, gkv, wq, wk, wv, *, l):
    tm = PREP_TM
    tiles_per_seq = SEQ // tm

    def tab_idx(i):
        return jnp.where(i < T_X // tm, i % tiles_per_seq, tiles_per_seq)

    return pl.pallas_call(
        _mla_prep_kernel,
        out_shape=[jax.ShapeDtypeStruct((T_ALL, MLA_HEADS * LANE), BF16),
                   jax.ShapeDtypeStruct((MLA_HEADS * LANE, T_ALL), BF16),
                   jax.ShapeDtypeStruct((T_ALL, MLA_HEADS * LANE), BF16)],
        grid=(T_ALL // tm,),
        in_specs=[
            pl.BlockSpec((tm, MLA_Q_RANK), lambda i: (i, 0)),
            pl.BlockSpec((tm, MLA_KV_RANK), lambda i: (i, 0)),
            pl.BlockSpec((tm, LANE), lambda i: (i, 0)),
            pl.BlockSpec((tm, 4 * LANE), lambda i: (tab_idx(i), 0)),
            pl.BlockSpec((None, 1, MLA_Q_RANK), lambda i: (l, 0, 0)),
            pl.BlockSpec((None, 1, MLA_KV_RANK), lambda i: (l, 0, 0)),
            pl.BlockSpec((None, MLA_Q_RANK, MLA_HEADS * LANE), lambda i: (l, 0, 0)),
            pl.BlockSpec((None, MLA_KV_RANK + LANE, MLA_HEADS * LANE), lambda i: (l, 0, 0)),
            pl.BlockSpec((None, MLA_KV_RANK, MLA_HEADS * LANE), lambda i: (l, 0, 0)),
        ],
        out_specs=[pl.BlockSpec((tm, MLA_HEADS * LANE), lambda i: (i, 0)),
                   pl.BlockSpec((MLA_HEADS * LANE, tm), lambda i: (0, i)),
                   pl.BlockSpec((tm, MLA_HEADS * LANE), lambda i: (i, 0))],
        compiler_params=_cparams(("parallel",)),
        name="mla_prep",
    )(cq, ckv, small, tab, gq, gkv, wq, wk, wv)


def _mla_attn_kernel(*refs, n_kv):
    q_ref = refs[0]
    k_refs = refs[1:1 + n_kv]
    v_refs = refs[1 + n_kv:1 + 2 * n_kv]
    o_ref = refs[1 + 2 * n_kv]
    outs = []
    for h in range(2):
        hl = slice(h * LANE, (h + 1) * LANE)
        q = q_ref[:, hl]
        m, acc = None, None
        for k_ref, v_ref in zip(k_refs, v_refs):
            n_keys = k_ref.shape[1]
            for c0 in range(0, n_keys, ATT_TK):
                c1 = min(c0 + ATT_TK, n_keys)
                s = _dot(q, k_ref[hl, c0:c1])
                m_new = s.max(axis=-1, keepdims=True)
                if m is not None:
                    m_new = jnp.maximum(m, m_new)
                    acc = acc * jnp.exp2(m - m_new)
                pv = _dot(jnp.exp2(s - m_new).astype(BF16), v_ref[c0:c1, hl])
                acc = pv if acc is None else acc + pv
                m = m_new
        outs.append(acc / acc[:, MLA_V:MLA_V + 1])
    lane = lax.broadcasted_iota(jnp.int32, outs[0].shape, 1)
    o_ref[...] = jnp.where(lane < MLA_V, outs[0], pltpu.roll(outs[1], MLA_V, 1))


def _mla_attn_latent(q, k, v):
    tq = ATT_TQ
    qt = SEQ // tq
    ctx_blk0 = T_X // CTX_LEN
    return pl.pallas_call(
        functools.partial(_mla_attn_kernel, n_kv=2),
        out_shape=jax.ShapeDtypeStruct((T_X, MLA_HEADS * MLA_V), F32),
        grid=(BATCH, MLA_HEADS // 2, qt),
        in_specs=[
            pl.BlockSpec((tq, 2 * LANE), lambda b, h, t: (b * qt + t, h)),
            pl.BlockSpec((2 * LANE, SEQ), lambda b, h, t: (h, b)),
            pl.BlockSpec((2 * LANE, CTX_LEN), lambda b, h, t: (h, ctx_blk0 + b)),
            pl.BlockSpec((SEQ, 2 * LANE), lambda b, h, t: (b, h)),
            pl.BlockSpec((CTX_LEN, 2 * LANE), lambda b, h, t: (ctx_blk0 + b, h)),
        ],
        out_specs=pl.BlockSpec((tq, LANE), lambda b, h, t: (b * qt + t, h)),
        compiler_params=_cparams(("parallel", "parallel", "arbitrary")),
        name="mla_attn_latent",
    )(q, k, k, v, v)


def _mla_attn_ctx(q, k, v):
    ctx_blk0 = T_X // CTX_LEN
    return pl.pallas_call(
        functools.partial(_mla_attn_kernel, n_kv=1),
        out_shape=jax.ShapeDtypeStruct((T_C, MLA_HEADS * MLA_V), F32),
        grid=(BATCH, MLA_HEADS // 2),
        in_specs=[
            pl.BlockSpec((CTX_LEN, 2 * LANE), lambda b, h: (ctx_blk0 + b, h)),
            pl.BlockSpec((2 * LANE, CTX_LEN), lambda b, h: (h, ctx_blk0 + b)),
            pl.BlockSpec((CTX_LEN, 2 * LANE), lambda b, h: (ctx_blk0 + b, h)),
        ],
        out_specs=pl.BlockSpec((CTX_LEN, LANE), lambda b, h: (b, h)),
        compiler_params=_cparams(("parallel", "parallel")),
        name="mla_attn_ctx",
    )(q, k, v)


def _swa_core(q, keys, vals, masks, sinks, o_ref):
    w = q.shape[0]
    lane = lax.broadcasted_iota(jnp.int32, (w, LANE), 1)
    low = lane < SWA_HEADDIM
    rep = SWA_HEADS // SWA_KV_HEADS
    k_all = jnp.concatenate(keys, axis=0).astype(BF16)
    v_all = jnp.concatenate(vals, axis=0).astype(BF16)
    bias = None
    if any(mask is not None for mask in masks):
        bias = jnp.concatenate(
            [jnp.zeros((w, kb.shape[0]), F32) if mask is None else jnp.where(mask, 0.0, NEG)
             for kb, mask in zip(keys, masks)], axis=1)
        bias = jnp.concatenate([bias] * rep, axis=0)
    for g in range(SWA_KV_HEADS):
        parts = []
        for j in range(rep // 2):
            qp = q[:, (g * rep // 2 + j) * LANE:(g * rep // 2 + j + 1) * LANE]
            parts.append(jnp.where(low, qp, 0.0))
            parts.append(jnp.where(low, 0.0, qp))
        lhs = jnp.concatenate(parts, axis=0).astype(BF16)
        sink = jnp.concatenate([jnp.full((w, 1), sinks[g * rep + r] * LOG2E, F32) for r in range(rep)],
                               axis=0)
        s = _dot_t(lhs, k_all[:, g * LANE:(g + 1) * LANE])
        if bias is not None:
            s = s + bias
        m = jnp.maximum(sink, s.max(axis=-1, keepdims=True))
        p = jnp.exp2(s - m)
        den = jnp.exp2(sink - m) + p.sum(axis=-1, keepdims=True)
        o = _dot(p.astype(BF16), v_all[:, g * LANE:(g + 1) * LANE]) / den
        for j in range(rep // 2):
            o_ref[:, (g * rep // 2 + j) * LANE:(g * rep // 2 + j + 1) * LANE] = jnp.where(
                low, o[2 * j * w:(2 * j + 1) * w], o[(2 * j + 1) * w:(2 * j + 2) * w])


def _rope_rows(x_ref, tab_ref, n, reps):
    cos = jnp.concatenate([tab_ref[:, :LANE]] * reps, axis=1)
    sin = jnp.concatenate([tab_ref[:, LANE:]] * reps, axis=1)
    return x_ref[:, :n].astype(F32) * cos + x_ref[:, n:].astype(F32) * sin


def _swa_latent_kernel(sink_ref, q_ref, kp_ref, kc_ref, kn_ref, kx_ref, vp_ref, vc_ref, vn_ref,
                       vx_ref, tq_ref, tp_ref, tn_ref, o_ref, *, l):
    n = pl.program_id(1)
    q = _rope_rows(q_ref, tq_ref, N_SWA_Q, N_SWA_Q // LANE) * SWA_QSCALE
    keys = [kx_ref[:, :N_SWA_KV2].astype(F32),
            _rope_rows(kp_ref, tp_ref, N_SWA_KV2, N_SWA_KV2 // LANE),
            _rope_rows(kc_ref, tq_ref, N_SWA_KV2, N_SWA_KV2 // LANE),
            _rope_rows(kn_ref, tn_ref, N_SWA_KV2, N_SWA_KV2 // LANE)]
    vals = [vx_ref[...], vp_ref[...], vc_ref[...], vn_ref[...]]
    qi = lax.broadcasted_iota(jnp.int32, (SWA_BLOCK, SWA_BLOCK), 0)
    kj = lax.broadcasted_iota(jnp.int32, (SWA_BLOCK, SWA_BLOCK), 1)
    prev_ok = jnp.logical_and(kj - qi >= SWA_BLOCK - SWA_WINDOW, n > 0)
    next_ok = jnp.logical_and(kj - qi <= SWA_WINDOW - SWA_BLOCK, n < SEQ // SWA_BLOCK - 1)
    sinks = [sink_ref[l, h] for h in range(SWA_HEADS)]
    _swa_core(q, keys, vals, [None, prev_ok, None, next_ok], sinks, o_ref)


def _swa_ctx_kernel(sink_ref, q_ref, kx_ref, vx_ref, o_ref, *, l):
    q = q_ref[:, :N_SWA_Q].astype(F32) * SWA_QSCALE
    sinks = [sink_ref[l, h] for h in range(SWA_HEADS)]
    _swa_core(q, [kx_ref[:, :N_SWA_KV2]], [vx_ref[...]], [None], sinks, o_ref)


def _swa_latent(sink, qs2, ks2, vs2, tab, *, l):
    nb = SEQ // SWA_BLOCK
    ctx_blk0 = T_X // CTX_LEN
    w = SWA_BLOCK

    def prev(b, n):
        return b * nb + jnp.maximum(n - 1, 0)

    def nxt(b, n):
        return b * nb + jnp.minimum(n + 1, nb - 1)

    return pl.pallas_call(
        functools.partial(_swa_latent_kernel, l=l),
        out_shape=jax.ShapeDtypeStruct((T_X, N_SWA_Q), F32),
        grid=(BATCH, nb),
        in_specs=[
            pl.BlockSpec(memory_space=pltpu.SMEM),
            pl.BlockSpec((w, 2 * N_SWA_Q), lambda b, n: (b * nb + n, 0)),
            pl.BlockSpec((w, 2 * N_SWA_KV2), lambda b, n: (prev(b, n), 0)),
            pl.BlockSpec((w, 2 * N_SWA_KV2), lambda b, n: (b * nb + n, 0)),
            pl.BlockSpec((w, 2 * N_SWA_KV2), lambda b, n: (nxt(b, n), 0)),
            pl.BlockSpec((CTX_LEN, 2 * N_SWA_KV2), lambda b, n: (ctx_blk0 + b, 0)),
            pl.BlockSpec((w, N_SWA_KV2), lambda b, n: (prev(b, n), 0)),
            pl.BlockSpec((w, N_SWA_KV2), lambda b, n: (b * nb + n, 0)),
            pl.BlockSpec((w, N_SWA_KV2), lambda b, n: (nxt(b, n), 0)),
            pl.BlockSpec((CTX_LEN, N_SWA_KV2), lambda b, n: (ctx_blk0 + b, 0)),
            pl.BlockSpec((w, 2 * LANE), lambda b, n: (n, 0)),
            pl.BlockSpec((w, 2 * LANE), lambda b, n: (jnp.maximum(n - 1, 0), 0)),
            pl.BlockSpec((w, 2 * LANE), lambda b, n: (jnp.minimum(n + 1, nb - 1), 0)),
        ],
        out_specs=pl.BlockSpec((w, N_SWA_Q), lambda b, n: (b * nb + n, 0)),
        compiler_params=_cparams(("parallel", "arbitrary")),
        name="swa_latent",
    )(sink, qs2, ks2, ks2, ks2, ks2, vs2, vs2, vs2, vs2, tab, tab, tab)


def _swa_ctx(sink, qs2, ks2, vs2, *, l):
    w = SWA_BLOCK
    blk0 = T_X // w
    per_b = CTX_LEN // w
    ctx_blk0 = T_X // CTX_LEN
    return pl.pallas_call(
        functools.partial(_swa_ctx_kernel, l=l),
        out_shape=jax.ShapeDtypeStruct((T_C, N_SWA_Q), F32),
        grid=(BATCH, per_b),
        in_specs=[
            pl.BlockSpec(memory_space=pltpu.SMEM),
            pl.BlockSpec((w, 2 * N_SWA_Q), lambda b, n: (blk0 + b * per_b + n, 0)),
            pl.BlockSpec((CTX_LEN, 2 * N_SWA_KV2), lambda b, n: (ctx_blk0 + b, 0)),
            pl.BlockSpec((CTX_LEN, N_SWA_KV2), lambda b, n: (ctx_blk0 + b, 0)),
        ],
        out_specs=pl.BlockSpec((w, N_SWA_Q), lambda b, n: (b * per_b + n, 0)),
        compiler_params=_cparams(("parallel", "arbitrary")),
        name="swa_ctx",
    )(sink, qs2, ks2, vs2)


def _outproj_kernel(h_ref, mod_ref, ssd_ref, mla_ref, mla_c_ref, swa_ref, swa_c_ref, gm_ref, gs_ref,
                    w_ref, o_ref, *, n_lat):
    n_mla = MLA_HEADS * MLA_V
    is_lat = pl.program_id(0) < n_lat
    acc = _dot(ssd_ref[...], w_ref[0:SSD_INNER, :])
    mla = jnp.where(is_lat, mla_ref[...], mla_c_ref[...])
    mla = (_rms(mla) * gm_ref[...]).astype(BF16)
    acc += _dot(mla, w_ref[SSD_INNER:SSD_INNER + n_mla, :])
    swa = jnp.where(is_lat, swa_ref[...], swa_c_ref[...])
    swa = (_rms(swa) * gs_ref[...]).astype(BF16)
    acc += _dot(swa, w_ref[SSD_INNER + n_mla:, :])
    o_ref[...] = h_ref[...] + mod_ref[5:6, :] * acc


def _outproj(h, mod_l, ssd, mla, mla_c, swa, swa_c, gm, gs, w_out, *, l, rows):
    tm = OUT_TM
    n_mla = MLA_HEADS * MLA_V
    n_lat = T_X // tm

    def lat(i):
        return (jnp.minimum(i, n_lat - 1), 0)

    def ctx(i):
        return (jnp.maximum(i - n_lat, 0), 0)

    return pl.pallas_call(
        functools.partial(_outproj_kernel, n_lat=n_lat),
        out_shape=jax.ShapeDtypeStruct((rows, D_MODEL), F32),
        grid=(rows // tm,),
        in_specs=[
            pl.BlockSpec((tm, D_MODEL), lambda i: (i, 0)),
            pl.BlockSpec((None, N_MOD, D_MODEL), lambda i: (_group_of_tile(i, tm), 0, 0)),
            pl.BlockSpec((tm, SSD_INNER), lambda i: (i, 0)),
            pl.BlockSpec((tm, n_mla), lat),
            pl.BlockSpec((tm, n_mla), ctx),
            pl.BlockSpec((tm, N_SWA_Q), lat),
            pl.BlockSpec((tm, N_SWA_Q), ctx),
            pl.BlockSpec((None, 1, n_mla), lambda i: (l, 0, 0)),
            pl.BlockSpec((None, 1, N_SWA_Q), lambda i: (l, 0, 0)),
            pl.BlockSpec((None, D_MIX, D_MODEL), lambda i: (l, 0, 0)),
        ],
        out_specs=pl.BlockSpec((tm, D_MODEL), lambda i: (i, 0)),
        compiler_params=_cparams(("parallel",)),
        name="outproj",
    )(h, mod_l, ssd, mla, mla_c, swa, swa_c, gm, gs, w_out)


def _cast_kernel(x_ref, o_ref):
    o_ref[...] = x_ref[...].astype(BF16)


def _to_bf16(w):
    r, c = w.shape[-2:]
    lead = int(np.prod(w.shape[:-2]))
    out = pl.pallas_call(
        _cast_kernel,
        out_shape=jax.ShapeDtypeStruct((lead, r, c), BF16),
        grid=(lead, r // CAST_ROWS),
        in_specs=[pl.BlockSpec((None, CAST_ROWS, c), lambda a, t: (a, t, 0))],
        out_specs=pl.BlockSpec((None, CAST_ROWS, c), lambda a, t: (a, t, 0)),
        compiler_params=_cparams(("parallel", "parallel")),
        name="to_bf16",
    )(w.reshape(lead, r, c))
    return out.reshape(w.shape)


def _rot_cols(w, dim):
    lead, n = w.shape[:-1], w.shape[-1]
    w5 = w.reshape(lead + (n // dim, 2, 2, dim // 4))
    w5 = lax.rev(w5, (w5.ndim - 2,)) * jnp.asarray([-1.0, 1.0], w.dtype)[:, None]
    return w5.reshape(lead + (n,))


def _w_in_kernel(x_ref, main_ref, extra_ref):
    rows = x_ref.shape[0]
    lane = lax.broadcasted_iota(jnp.int32, (rows, LANE), 1)
    low = lane < SWA_HEADDIM

    def tile(t):
        return x_ref[:, t * LANE:(t + 1) * LANE]

    def shifted(t, k):
        a = pltpu.roll(tile(t), LANE - k, 1)
        b = pltpu.roll(tile(t + 1), LANE - k, 1)
        return jnp.where(lane < LANE - k, a, b)

    def rot(v, dim):
        q = dim // 4
        return jnp.where(lane % (2 * q) < q, -pltpu.roll(v, LANE - q, 1), pltpu.roll(v, q, 1))

    def dup_hi(v):
        return jnp.where(low, pltpu.roll(v, SWA_HEADDIM, 1), v)

    def dup_lo(v):
        return jnp.where(low, v, pltpu.roll(v, SWA_HEADDIM, 1))

    main_ref[...] = x_ref[:, :W_IN_MAIN].astype(BF16)
    t_dt = W_IN_MAIN // LANE
    off_q = 2 * SSD_HEADS
    n_cqkv = (MLA_Q_RANK + MLA_KV_RANK) // LANE
    t_kr = t_dt + n_cqkv
    off_s = off_q + MLA_ROPE
    n_qs = N_SWA_Q // LANE
    t_k0 = t_kr + n_qs
    v1 = x_ref[:, (t_k0 + 2) * LANE:(t_k0 + 2) * LANE + SWA_HEADDIM]
    qs = [shifted(t_kr + j, off_s) for j in range(n_qs)]
    kr_tile = tile(t_kr)
    rot_kr = pltpu.roll(rot(kr_tile, MLA_ROPE), MLA_ROPE, 1)
    small = jnp.where(lane < off_q, tile(t_dt),
                      jnp.where(lane < off_s, kr_tile, jnp.where(lane < off_s + MLA_ROPE, rot_kr, 0.0)))
    pieces = ([shifted(t_dt + j, off_q) for j in range(n_cqkv)] + qs + [rot(v, SWA_HEADDIM) for v in qs]
              + [dup_hi(tile(t_k0)), dup_lo(tile(t_k0 + 1)),
                 dup_hi(rot(tile(t_k0), SWA_HEADDIM)), dup_lo(rot(tile(t_k0 + 1), SWA_HEADDIM)),
                 dup_hi(tile(t_k0 + 1)), jnp.concatenate([v1, v1], axis=1), small])
    extra_ref[...] = jnp.concatenate(pieces, axis=1).astype(BF16)


def _w_in_relayout(w_in):
    d_in = w_in.shape[-1]
    assert d_in == sum(IN_SIZES)
    return pl.pallas_call(
        _w_in_kernel,
        out_shape=[jax.ShapeDtypeStruct((DEPTH, D_MODEL, W_IN_MAIN), BF16),
                   jax.ShapeDtypeStruct((DEPTH, D_MODEL, W_IN_COLS - W_IN_MAIN), BF16)],
        grid=(DEPTH, D_MODEL // CAST_ROWS),
        in_specs=[pl.BlockSpec((None, CAST_ROWS, d_in), lambda a, t: (a, t, 0))],
        out_specs=[pl.BlockSpec((None, CAST_ROWS, W_IN_MAIN), lambda a, t: (a, t, 0)),
                   pl.BlockSpec((None, CAST_ROWS, W_IN_COLS - W_IN_MAIN), lambda a, t: (a, t, 0))],
        compiler_params=_cparams(("parallel", "parallel")),
        name="w_in_relayout",
    )(w_in)


def _mla_w_relayout(w_uq, w_ukv):
    lead = w_uq.shape[:-1]
    q = w_uq.reshape(lead + (MLA_HEADS, MLA_NOPE + MLA_ROPE))
    rope = q[..., MLA_NOPE:]
    wq = jnp.concatenate([q, _rot_cols(rope, MLA_ROPE)], axis=-1).reshape(lead + (MLA_HEADS * LANE,))
    lead = w_ukv.shape[:-1]
    kv = w_ukv.reshape(lead + (MLA_HEADS, MLA_NOPE + MLA_V))
    pad = jnp.zeros(lead + (MLA_HEADS, LANE - MLA_NOPE), w_ukv.dtype)
    wk = jnp.concatenate([kv[..., :MLA_NOPE], pad], axis=-1).reshape(lead + (MLA_HEADS * LANE,))
    wv = jnp.concatenate([kv[..., MLA_NOPE:], pad], axis=-1).reshape(lead + (MLA_HEADS * LANE,))
    return wq.astype(BF16), wk, wv.astype(BF16)


def _k_rope_expand():
    e = np.zeros((LANE, MLA_HEADS * LANE), np.float32)
    for h in range(MLA_HEADS):
        for i in range(MLA_ROPE):
            e[MLA_ROPE + i, h * LANE + MLA_NOPE + i] = 1.0
    return e


def _head_expand():
    e = np.zeros((SSD_HEADS, SSD_INNER), np.float32)
    for h in range(SSD_HEADS):
        e[h, h * SSD_HEADDIM:(h + 1) * SSD_HEADDIM] = 1.0
    return e


def _rope_angles(dim):
    pos = np.arange(SEQ)
    quarter = dim // 4
    inv_freq = ROPE_THETA ** (-jnp.arange(quarter, dtype=F32) / quarter)
    ang_r = jnp.asarray(pos // GRID_W, F32)[:, None] * inv_freq[None, :]
    ang_c = jnp.asarray(pos % GRID_W, F32)[:, None] * inv_freq[None, :]
    ang = jnp.concatenate([ang_r, ang_r, ang_c, ang_c], axis=-1)
    return jnp.cos(ang), jnp.sin(ang)


def _mla_rope_table():
    cos, sin = _rope_angles(MLA_ROPE)
    scale = (MLA_NOPE + MLA_ROPE) ** -0.5 * LOG2E
    n = SEQ + PREP_TM
    ident = jnp.ones((PREP_TM, MLA_ROPE), F32)
    zeros = jnp.zeros((PREP_TM, MLA_ROPE), F32)
    cos = jnp.concatenate([cos, ident], axis=0)
    sin = jnp.concatenate([sin, zeros], axis=0)
    z32 = jnp.zeros((n, MLA_ROPE), F32)
    cq = jnp.concatenate([jnp.ones((n, MLA_NOPE), F32), cos, z32], axis=1) * scale
    sq = jnp.concatenate([jnp.zeros((n, MLA_NOPE), F32), sin, z32], axis=1) * scale
    ck = jnp.concatenate([z32, cos, z32, z32], axis=1)
    sk = jnp.concatenate([z32, sin, z32, z32], axis=1)
    return jnp.concatenate([cq, sq, ck, sk], axis=1)


def _swa_rope_table():
    cos, sin = _rope_angles(SWA_HEADDIM)
    return jnp.concatenate([cos, cos, sin, sin], axis=1)


def kernel(x, c, ctx, c_ctx, w_mod, b_mod, norm_g, ffn_w_gate, ffn_w_up, ffn_w_down,
           w_in, w_out, ssd_conv_w, ssd_conv_b, ssd_dt_bias, ssd_a_log, ssd_d, ssd_norm,
           mla_q_norm, mla_w_uq, mla_kv_norm, mla_w_ukv, mla_out_norm, swa_sink,
           swa_out_norm, final_norm):
    w_main, w_extra = _w_in_relayout(w_in)
    wq, wk, wv = _mla_w_relayout(mla_w_uq, mla_w_ukv)
    k_expand = jnp.broadcast_to(jnp.asarray(_k_rope_expand()), (DEPTH, LANE, MLA_HEADS * LANE))
    wk = jnp.concatenate([wk, k_expand], axis=1).astype(BF16)
    wg = _to_bf16(ffn_w_gate)
    wu = _to_bf16(ffn_w_up)
    wd = _to_bf16(ffn_w_down)
    wo = _to_bf16(w_out)
    mla_tab = _mla_rope_table()
    swa_tab = _swa_rope_table()
    e16 = jnp.asarray(_head_expand(), BF16)
    dskip_e = jnp.repeat(ssd_d, SSD_HEADDIM, axis=-1).reshape(DEPTH, 1, SSD_INNER)
    ssd_gn = ssd_norm.reshape(DEPTH, 1, SSD_INNER)
    conv_b = ssd_conv_b.reshape(DEPTH, 1, SSD_CONV_CH)
    bias_r = ssd_dt_bias.reshape(DEPTH, 2, 1, SSD_HEADS)
    bias_c = ssd_dt_bias.reshape(DEPTH, 2, SSD_HEADS, 1)
    alog_r = ssd_a_log.reshape(DEPTH, 2, 1, SSD_HEADS)
    alog_c = ssd_a_log.reshape(DEPTH, 2, SSD_HEADS, 1)
    gq = mla_q_norm.reshape(DEPTH, 1, MLA_Q_RANK)
    gkv = mla_kv_norm.reshape(DEPTH, 1, MLA_KV_RANK)
    gm = mla_out_norm.reshape(DEPTH, 1, -1)
    gs = swa_out_norm.reshape(DEPTH, 1, -1)
    fin = final_norm.reshape(1, D_MODEL)

    c_all = jnp.concatenate([c, c_ctx[None, :], jnp.zeros((GROUP_PAD - N_GROUPS, D_MODEL), F32)], axis=0)
    mod_all = _modulation(c_all, w_mod, b_mod).reshape(DEPTH, GROUP_PAD, N_MOD, D_MODEL)

    h = x.reshape(T_X, D_MODEL)
    h_ctx = ctx.reshape(T_C, D_MODEL)

    for l in range(DEPTH):
        last = l == DEPTH - 1
        mod_l = mod_all[l]
        h = _ffn(h, h_ctx, mod_l, norm_g[l, 0].reshape(1, D_MODEL), wg, wu, wd, fin,
                 l=l, k=0, base=0, rows=T_ALL)
        h_ctx = None
        z, xbc, cq, ckv, qs2, ks2, vs2, small = _inproj(h, mod_l, norm_g[l, 1].reshape(1, D_MODEL),
                                                       w_main, w_extra, l=l)
        dt2 = small[:, :2 * SSD_HEADS].reshape(T_ALL, 2, SSD_HEADS)
        dt_col = jnp.transpose(dt2, (1, 0, 2))
        dt_row = jnp.transpose(dt2, (1, 2, 0))
        ssd_o = _ssd(xbc, z, dt_col, dt_row, ssd_conv_w, conv_b, bias_r, bias_c, alog_r, alog_c,
                     dskip_e, ssd_gn, e16, l=l)
        q_m, k_m, v_m = _mla_prep(cq, ckv, small, mla_tab, gq, gkv, wq, wk, wv, l=l)
        mla_x = _mla_attn_latent(q_m, k_m, v_m)
        swa_x = _swa_latent(swa_sink, qs2, ks2, vs2, swa_tab, l=l)
        if last:
            rows, mla_c, swa_c = T_X, mla_x, swa_x
        else:
            rows = T_ALL
            mla_c = _mla_attn_ctx(q_m, k_m, v_m)
            swa_c = _swa_ctx(swa_sink, qs2, ks2, vs2, l=l)
        h = _outproj(h, mod_l, ssd_o, mla_x, mla_c, swa_x, swa_c, gm, gs, wo, l=l, rows=rows)
        h = _ffn(h, None, mod_l, norm_g[l, 2].reshape(1, D_MODEL), wg, wu, wd, fin,
                 l=l, k=1, base=6, rows=rows, final_norm=last)
    return h.reshape(BATCH, SEQ, D_MODEL)
```

```python
import functools
import math

import jax
import jax.numpy as jnp
import numpy as np
from jax import lax
from jax.experimental import pallas as pl
from jax.experimental.pallas import tpu as pltpu

F32 = jnp.float32
BF16 = jnp.bfloat16
HIGHEST = lax.Precision.HIGHEST

D_MODEL = 2048
BATCH = 4
SEQ = 4096
DEPTH = 4
GRID_W = 64
CTX_LEN = 256
EPS = 1e-6
ROPE_THETA = 10000.0
N_MOD = 9
D_FF = 5632

SSD_HEADS = 16
SSD_HEADDIM = 64
SSD_INNER = SSD_HEADS * SSD_HEADDIM
SSD_GROUPS = 2
SSD_STATE = 128
SSD_CONV = 5
SSD_CONV_CH = SSD_INNER + 2 * SSD_GROUPS * SSD_STATE

MLA_HEADS = 8
MLA_Q_RANK = 384
MLA_KV_RANK = 256
MLA_NOPE = 64
MLA_ROPE = 32
MLA_V = 64

SWA_HEADS = 8
SWA_KV_HEADS = 2
SWA_HEADDIM = 64
SWA_WINDOW = 128
SWA_BLOCK = 128

D_MIX = SSD_INNER + MLA_HEADS * MLA_V + SWA_HEADS * SWA_HEADDIM
IN_SIZES = (SSD_INNER, SSD_CONV_CH, 2 * SSD_HEADS, MLA_Q_RANK, MLA_KV_RANK, MLA_ROPE,
            SWA_HEADS * SWA_HEADDIM, SWA_KV_HEADS * SWA_HEADDIM, SWA_KV_HEADS * SWA_HEADDIM)

T_X = BATCH * SEQ
T_C = BATCH * CTX_LEN
T_ALL = T_X + T_C
N_GROUPS = BATCH + 1
GROUP_PAD = 8

LANE = 128
SUBLANE = 8
VMEM_LIMIT = 56 * 1024 * 1024
NEG = -1e30
LOG2E = math.log2(math.e)

FFN_TM = 512
FFN_TF = 512
PROJ_TM = 512
OUT_TM = 512
PREP_TM = 512
ATT_TQ = 1024
ATT_TK = 2048
MOD_TN = 1024
CAST_ROWS = 512

SSD_Q = 128
SEQ_CHUNKS = SEQ // SSD_Q
CTX_CHUNKS = CTX_LEN // SSD_Q
SSD_STEPS = SEQ_CHUNKS + CTX_CHUNKS
HALO = 2 * SUBLANE
N_SWA_Q = SWA_HEADS * SWA_HEADDIM
N_SWA_KV2 = 2 * SWA_KV_HEADS * SWA_HEADDIM
SWA_QSCALE = SWA_HEADDIM ** -0.5 * LOG2E

SEG_Z = (0, 1024)
SEG_XBC = (1024, 2560)
SEG_CQ = (2560, 2944)
SEG_CKV = (2944, 3200)
SEG_QS = (3200, 4224)
SEG_KS = (4224, 4736)
SEG_VS = (4736, 4992)
SEG_SMALL = (4992, 5120)
W_IN_COLS = 5120
W_IN_MAIN = SEG_XBC[1]
IN_SEGS = (SEG_Z, SEG_XBC, SEG_CQ, SEG_CKV, SEG_QS, SEG_KS, SEG_VS, SEG_SMALL)


def _cparams(sem):
    return pltpu.CompilerParams(dimension_semantics=sem, vmem_limit_bytes=VMEM_LIMIT)


def _group_of_tile(i, tm):
    return jnp.minimum((i * tm) // SEQ, BATCH)


def _rms(x):
    return x * lax.rsqrt(jnp.mean(x * x, axis=-1, keepdims=True) + EPS)


def _silu(x):
    return x * jax.nn.sigmoid(x)


def _softplus(x):
    return jnp.maximum(x, 0.0) + jnp.log1p(jnp.exp(-jnp.abs(x)))


def _dot(a, b):
    return jnp.dot(a, b, preferred_element_type=F32)


def _dot_t(a, b):
    return lax.dot_general(a, b, (((1,), (1,)), ((), ())), preferred_element_type=F32)


def _mod_kernel(c_ref, w_ref, b_ref, o_ref):
    sc = _silu(c_ref[...]).astype(BF16)
    o_ref[...] = _dot(sc, w_ref[...].astype(BF16)) + b_ref[...]


def _modulation(c_all, w_mod, b_mod):
    n = N_MOD * D_MODEL
    return pl.pallas_call(
        _mod_kernel,
        out_shape=jax.ShapeDtypeStruct((DEPTH, GROUP_PAD, n), F32),
        grid=(DEPTH, n // MOD_TN),
        in_specs=[
            pl.BlockSpec((GROUP_PAD, D_MODEL), lambda l, j: (0, 0)),
            pl.BlockSpec((None, D_MODEL, MOD_TN), lambda l, j: (l, 0, j)),
            pl.BlockSpec((None, 1, MOD_TN), lambda l, j: (l, 0, j)),
        ],
        out_specs=pl.BlockSpec((None, GROUP_PAD, MOD_TN), lambda l, j: (l, 0, j)),
        compiler_params=_cparams(("parallel", "parallel")),
        name="modulation",
    )(c_all, w_mod, b_mod.reshape(DEPTH, 1, n))


def _norm_modulate(x, g_ref, mod_ref, base):
    y = _rms(x) * g_ref[...]
    return (y * (1.0 + mod_ref[base + 1:base + 2, :]) + mod_ref[base:base + 1, :]).astype(BF16)


def _ffn_kernel(*refs, split, base, n_ff, n_lat, final_norm):
    x_ref, xc_ref = (refs[0], refs[1]) if split else (refs[0], None)
    mod_ref, g_ref, wg_ref, wu_ref, wd_ref, fin_ref, o_ref, u_ref = refs[(2 if split else 1):]
    j = pl.program_id(1)

    def residual():
        if xc_ref is None:
            return x_ref[...]
        return jnp.where(pl.program_id(0) < n_lat, x_ref[...], xc_ref[...])

    @pl.when(j == 0)
    def _():
        u_ref[...] = _norm_modulate(residual(), g_ref, mod_ref, base)
        o_ref[...] = jnp.zeros_like(o_ref)

    u = u_ref[...]
    part = None
    for c0 in range(0, FFN_TF, FFN_TF // 2):
        cols = slice(c0, c0 + FFN_TF // 2)
        gate = _dot(u, wg_ref[:, cols])
        up = _dot(u, wu_ref[:, cols])
        a = (_silu(gate) * up).astype(BF16)
        d = _dot(a, wd_ref[cols, :])
        part = d if part is None else part + d
    o_ref[...] += part

    @pl.when(j == n_ff - 1)
    def _():
        h = residual() + (0.5 * mod_ref[base + 2:base + 3, :]) * o_ref[...]
        if final_norm:
            h = _rms(h) * fin_ref[...]
        o_ref[...] = h


def _ffn(h, h_ctx, mod_l, g, wg, wu, wd, fin, *, l, k, base, rows, final_norm=False):
    n_ff = D_FF // FFN_TF
    tm = FFN_TM
    n_lat = T_X // tm
    split = h_ctx is not None
    kern = functools.partial(_ffn_kernel, split=split, base=base, n_ff=n_ff, n_lat=n_lat,
                             final_norm=final_norm)
    if split:
        x_specs = [pl.BlockSpec((tm, D_MODEL), lambda i, j: (jnp.minimum(i, n_lat - 1), 0)),
                   pl.BlockSpec((tm, D_MODEL), lambda i, j: (jnp.maximum(i - n_lat, 0), 0))]
        x_args = (h, h_ctx)
    else:
        x_specs = [pl.BlockSpec((tm, D_MODEL), lambda i, j: (i, 0))]
        x_args = (h,)
    return pl.pallas_call(
        kern,
        out_shape=jax.ShapeDtypeStruct((rows, D_MODEL), F32),
        grid=(rows // tm, n_ff),
        in_specs=x_specs + [
            pl.BlockSpec((None, N_MOD, D_MODEL), lambda i, j: (_group_of_tile(i, tm), 0, 0)),
            pl.BlockSpec((1, D_MODEL), lambda i, j: (0, 0)),
            pl.BlockSpec((None, None, D_MODEL, FFN_TF), lambda i, j: (l, k, 0, j)),
            pl.BlockSpec((None, None, D_MODEL, FFN_TF), lambda i, j: (l, k, 0, j)),
            pl.BlockSpec((None, None, FFN_TF, D_MODEL), lambda i, j: (l, k, j, 0)),
            pl.BlockSpec((1, D_MODEL), lambda i, j: (0, 0)),
        ],
        out_specs=pl.BlockSpec((tm, D_MODEL), lambda i, j: (i, 0)),
        scratch_shapes=[pltpu.VMEM((tm, D_MODEL), BF16)],
        compiler_params=_cparams(("parallel", "arbitrary")),
        name="ffn",
    )(*x_args, mod_l, g, wg, wu, wd, fin)


def _inproj_kernel(x_ref, mod_ref, g_ref, wm_ref, we_ref, *o_refs):
    u = _norm_modulate(x_ref[...], g_ref, mod_ref, 3)
    for (a, b), o_ref in zip(IN_SEGS, o_refs):
        if b <= W_IN_MAIN:
            o_ref[...] = _dot(u, wm_ref[:, a:b]).astype(o_ref.dtype)
        else:
            o_ref[...] = _dot(u, we_ref[:, a - W_IN_MAIN:b - W_IN_MAIN]).astype(o_ref.dtype)


def _inproj(h, mod_l, g, w_main, w_extra, *, l):
    tm = PROJ_TM
    return pl.pallas_call(
        _inproj_kernel,
        out_shape=[jax.ShapeDtypeStruct((T_ALL, b - a), F32 if (a, b) == SEG_SMALL else BF16)
                   for a, b in IN_SEGS],
        grid=(T_ALL // tm,),
        in_specs=[
            pl.BlockSpec((tm, D_MODEL), lambda i: (i, 0)),
            pl.BlockSpec((None, N_MOD, D_MODEL), lambda i: (_group_of_tile(i, tm), 0, 0)),
            pl.BlockSpec((1, D_MODEL), lambda i: (0, 0)),
            pl.BlockSpec((None, D_MODEL, W_IN_MAIN), lambda i: (l, 0, 0),
                         pipeline_mode=pl.Buffered(1)),
            pl.BlockSpec((None, D_MODEL, W_IN_COLS - W_IN_MAIN), lambda i: (l, 0, 0),
                         pipeline_mode=pl.Buffered(1)),
        ],
        out_specs=[pl.BlockSpec((tm, b - a), lambda i: (i, 0)) for a, b in IN_SEGS],
        compiler_params=_cparams(("parallel",)),
        name="inproj",
    )(h, mod_l, g, w_main, w_extra)


def _ssd_chunk_ids(phase, s):
    fwd = phase == 1
    is_ctx = s < CTX_CHUNKS
    cc = jnp.where(fwd, s, CTX_CHUNKS - 1 - s)
    lc = jnp.where(fwd, s - CTX_CHUNKS, SSD_STEPS - 1 - s)
    return is_ctx, cc, lc


def _ssd_row_block(b, phase, s):
    is_ctx, cc, lc = _ssd_chunk_ids(phase, s)
    return jnp.where(is_ctx, T_X // SSD_Q + b * CTX_CHUNKS + cc, b * SEQ_CHUNKS + lc)


def _ssd_kernel(xbc_ref, prev_ref, next_ref, z_ref, dtc_ref, dtr_ref, cw_ref, cb_ref,
                bias_r_ref, bias_c_ref, alog_r_ref, alog_c_ref, dskip_ref, gn_ref, e16_ref,
                o_ref, xe_ref, state_ref, ybwd_ref, y_ref, xcs_ref):
    phase = pl.program_id(1)
    s = pl.program_id(2)
    fwd = phase == 1
    is_ctx, cc, lc = _ssd_chunk_ids(phase, s)
    first = jnp.where(is_ctx, cc == 0, lc == 0)
    last = jnp.where(is_ctx, cc == CTX_CHUNKS - 1, lc == SEQ_CHUNKS - 1)
    voff = pl.multiple_of(jnp.where(is_ctx, cc, CTX_CHUNKS + lc) * SSD_Q, SSD_Q)

    @pl.when(s == 0)
    def _():
        state_ref[...] = jnp.zeros_like(state_ref)

    @pl.when(jnp.logical_not(fwd))
    def _():
        xe_ref[0:HALO, :] = jnp.where(first, 0.0, prev_ref[...].astype(F32))
        xe_ref[HALO:HALO + SSD_Q, :] = xbc_ref[...].astype(F32)
        xe_ref[HALO + SSD_Q:, :] = jnp.where(last, 0.0, next_ref[...].astype(F32))
        acc = cb_ref[...] + cw_ref[0:1, :] * xe_ref[HALO - 2:HALO - 2 + SSD_Q, :]
        for k in range(1, SSD_CONV):
            acc = acc + cw_ref[k:k + 1, :] * xe_ref[HALO - 2 + k:HALO - 2 + k + SSD_Q, :]
        xcs_ref[pl.ds(voff, SSD_Q), :] = _silu(acc).astype(BF16)

    xc = xcs_ref[pl.ds(voff, SSD_Q), :]
    xs = xc[:, :SSD_INNER].astype(F32)
    n_bc = SSD_GROUPS * SSD_STATE

    dt_c = _softplus(dtc_ref[...] + bias_r_ref[...])
    dt_r = _softplus(dtr_ref[...] + bias_c_ref[...])
    a_c = dt_c * -jnp.exp(alog_r_ref[...])
    a_r = dt_r * -jnp.exp(alog_c_ref[...])
    row = lax.broadcasted_iota(jnp.int32, (SSD_Q, SSD_Q), 0)
    col = lax.broadcasted_iota(jnp.int32, (SSD_Q, SSD_Q), 1)
    d = (row - col) * jnp.where(fwd, 1, -1)
    causal = d >= 0
    s_c = jnp.dot(causal.astype(F32), a_c, precision=HIGHEST, preferred_element_type=F32)
    s_r = jnp.dot(a_r, (d <= 0).astype(F32), precision=HIGHEST, preferred_element_type=F32)

    e16 = e16_ref[...]
    rem = jnp.concatenate([dt_c, s_c], axis=0)
    expanded = None
    for _ in range(3):
        term = rem.astype(BF16)
        rem = rem - term.astype(F32)
        part = _dot(term, e16)
        expanded = part if expanded is None else expanded + part
    dt_e = expanded[:SSD_Q]
    s_e = expanded[SSD_Q:]
    tot_e = jnp.where(fwd, s_e[SSD_Q - 1:SSD_Q, :], s_e[0:1, :])
    xdt = xs * dt_e
    xdt_end = (xdt * jnp.exp(tot_e - s_e)).astype(BF16)
    xdt16 = xdt.astype(BF16)
    exps_e = jnp.exp(s_e)
    dec_e = jnp.exp(tot_e)
    lane = lax.broadcasted_iota(jnp.int32, (SSD_Q, LANE), 1)

    hp = SSD_INNER // SSD_GROUPS
    heads_per_group = SSD_HEADS // SSD_GROUPS
    for g in range(SSD_GROUPS):
        bg = xc[:, SSD_INNER + g * SSD_STATE:SSD_INNER + (g + 1) * SSD_STATE]
        cg = xc[:, SSD_INNER + n_bc + g * SSD_STATE:SSD_INNER + n_bc + (g + 1) * SSD_STATE]
        bt16 = bg.astype(F32).T.astype(BF16)
        cg16 = cg
        cb = _dot(cg16, bt16)
        st = state_ref[:, g * hp:(g + 1) * hp]
        y_off = _dot(cg16, st.astype(BF16)) * exps_e[:, g * hp:(g + 1) * hp]
        state_ref[:, g * hp:(g + 1) * hp] = (dec_e[:, g * hp:(g + 1) * hp] * st
                                             + _dot(bt16, xdt_end[:, g * hp:(g + 1) * hp]))
        for j in range(heads_per_group // 2):
            h0 = g * heads_per_group + 2 * j
            m = []
            for h in (h0, h0 + 1):
                seg = s_c[:, h:h + 1] - s_r[h:h + 1, :]
                m.append(cb * jnp.exp(jnp.where(causal, seg, NEG)))
            m = jnp.concatenate(m, axis=0).astype(BF16)
            lo = h0 * SSD_HEADDIM
            yp = _dot(m, xdt16[:, lo:lo + LANE])
            y_ref[:, lo:lo + LANE] = (jnp.where(lane < SSD_HEADDIM, yp[:SSD_Q], yp[SSD_Q:])
                                      + y_off[:, lo - g * hp:lo - g * hp + LANE])

    @pl.when(jnp.logical_not(fwd))
    def _():
        ybwd_ref[pl.ds(voff, SSD_Q), :] = y_ref[...]

    @pl.when(fwd)
    def _():
        y = y_ref[...] + ybwd_ref[pl.ds(voff, SSD_Q), :] + dskip_ref[...] * xs
        y = y * _silu(z_ref[...].astype(F32))
        for g in range(SSD_GROUPS):
            yg = y[:, g * hp:(g + 1) * hp]
            o_ref[:, g * hp:(g + 1) * hp] = (_rms(yg) * gn_ref[:, g * hp:(g + 1) * hp]).astype(BF16)


def _ssd(xbc, z, dt_col, dt_row, cw, cb, bias_r, bias_c, alog_r, alog_c, dskip_e, gn, e16, *, l):
    n_halo = T_ALL // HALO
    per_blk = SSD_Q // HALO

    def blk(b, p, s):
        return _ssd_row_block(b, p, s)

    def conv_blk(b, p, s):
        return _ssd_row_block(b, 0, jnp.where(p == 0, s, SSD_STEPS - 1))

    def out_blk(b, p, s):
        return _ssd_row_block(b, 1, jnp.where(p == 1, s, 0))

    return pl.pallas_call(
        _ssd_kernel,
        out_shape=jax.ShapeDtypeStruct((T_ALL, SSD_INNER), BF16),
        grid=(BATCH, 2, SSD_STEPS),
        in_specs=[
            pl.BlockSpec((SSD_Q, SSD_CONV_CH), lambda b, p, s: (conv_blk(b, p, s), 0)),
            pl.BlockSpec((HALO, SSD_CONV_CH),
                         lambda b, p, s: (jnp.maximum(conv_blk(b, p, s) * per_blk - 1, 0), 0)),
            pl.BlockSpec((HALO, SSD_CONV_CH),
                         lambda b, p, s: (jnp.minimum((conv_blk(b, p, s) + 1) * per_blk, n_halo - 1), 0)),
            pl.BlockSpec((SSD_Q, SSD_INNER), lambda b, p, s: (out_blk(b, p, s), 0)),
            pl.BlockSpec((None, SSD_Q, SSD_HEADS), lambda b, p, s: (1 - p, blk(b, p, s), 0)),
            pl.BlockSpec((None, SSD_HEADS, SSD_Q), lambda b, p, s: (1 - p, 0, blk(b, p, s))),
            pl.BlockSpec((None, SSD_CONV, SSD_CONV_CH), lambda b, p, s: (l, 0, 0)),
            pl.BlockSpec((None, 1, SSD_CONV_CH), lambda b, p, s: (l, 0, 0)),
            pl.BlockSpec((None, None, 1, SSD_HEADS), lambda b, p, s: (l, 1 - p, 0, 0)),
            pl.BlockSpec((None, None, SSD_HEADS, 1), lambda b, p, s: (l, 1 - p, 0, 0)),
            pl.BlockSpec((None, None, 1, SSD_HEADS), lambda b, p, s: (l, 1 - p, 0, 0)),
            pl.BlockSpec((None, None, SSD_HEADS, 1), lambda b, p, s: (l, 1 - p, 0, 0)),
            pl.BlockSpec((None, 1, SSD_INNER), lambda b, p, s: (l, 0, 0)),
            pl.BlockSpec((None, 1, SSD_INNER), lambda b, p, s: (l, 0, 0)),
            pl.BlockSpec((SSD_HEADS, SSD_INNER), lambda b, p, s: (0, 0)),
        ],
        out_specs=pl.BlockSpec((SSD_Q, SSD_INNER), lambda b, p, s: (out_blk(b, p, s), 0)),
        scratch_shapes=[
            pltpu.VMEM((SSD_Q + 2 * HALO, SSD_CONV_CH), F32),
            pltpu.VMEM((SSD_STATE, SSD_INNER), F32),
            pltpu.VMEM((SSD_STEPS * SSD_Q, SSD_INNER), F32),
            pltpu.VMEM((SSD_Q, SSD_INNER), F32),
            pltpu.VMEM((SSD_STEPS * SSD_Q, SSD_CONV_CH), BF16),
        ],
        compiler_params=_cparams(("parallel", "arbitrary", "arbitrary")),
        name="ssd",
    )(xbc, xbc, xbc, z, dt_col, dt_row, cw, cb, bias_r, bias_c, alog_r, alog_c, dskip_e, gn, e16)


def _mla_prep_kernel(cq_ref, ckv_ref, sm_ref, tab_ref, gq_ref, gkv_ref, wq_ref, wk_ref, wv_ref,
                     q_ref, k_ref, v_ref):
    cq_tab = tab_ref[:, 0:LANE]
    sq_tab = tab_ref[:, LANE:2 * LANE]
    ck_tab = tab_ref[:, 2 * LANE:3 * LANE]
    sk_tab = tab_ref[:, 3 * LANE:4 * LANE]
    nq = (_rms(cq_ref[...].astype(F32)) * gq_ref[...]).astype(BF16)
    q = _dot(nq, wq_ref[...])
    for h in range(MLA_HEADS):
        qh = q[:, h * LANE:(h + 1) * LANE]
        qh = qh * cq_tab + pltpu.roll(qh, LANE - MLA_ROPE, 1) * sq_tab
        q_ref[:, h * LANE:(h + 1) * LANE] = qh.astype(BF16)
    nkv = _rms(ckv_ref[...].astype(F32)) * gkv_ref[...]
    sm = sm_ref[...]
    kr = sm * ck_tab + pltpu.roll(sm, LANE - MLA_ROPE, 1) * sk_tab
    lhs = jnp.concatenate([nkv, kr], axis=1).astype(BF16)
    k_ref[...] = _dot(lhs, wk_ref[...]).T.astype(BF16)
    v = _dot(lhs[:, :MLA_KV_RANK], wv_ref[...])
    lane = lax.broadcasted_iota(jnp.int32, v.shape, 1)
    v_ref[...] = jnp.where(lane % LANE == MLA_V, 1.0, v).astype(BF16)


def _mla_prep(cq, ckv, small, tab, gq, gkv, wq, wk, wv, *, l):
    tm = PREP_TM
    tiles_per_seq = SEQ // tm

    def tab_idx(i):
        return jnp.where(i < T_X // tm, i % tiles_per_seq, tiles_per_seq)

    return pl.pallas_call(
        _mla_prep_kernel,
        out_shape=[jax.ShapeDtypeStruct((T_ALL, MLA_HEADS * LANE), BF16),
                   jax.ShapeDtypeStruct((MLA_HEADS * LANE, T_ALL), BF16),
                   jax.ShapeDtypeStruct((T_ALL, MLA_HEADS * LANE), BF16)],
        grid=(T_ALL // tm,),
        in_specs=[
            pl.BlockSpec((tm, MLA_Q_RANK), lambda i: (i, 0)),
            pl.BlockSpec((tm, MLA_KV_RANK), lambda i: (i, 0)),
            pl.BlockSpec((tm, LANE), lambda i: (i, 0)),
            pl.BlockSpec((tm, 4 * LANE), lambda i: (tab_idx(i), 0)),
            pl.BlockSpec((None, 1, MLA_Q_RANK), lambda i: (l, 0, 0)),
            pl.BlockSpec((None, 1, MLA_KV_RANK), lambda i: (l, 0, 0)),
            pl.BlockSpec((None, MLA_Q_RANK, MLA_HEADS * LANE), lambda i: (l, 0, 0)),
            pl.BlockSpec((None, MLA_KV_RANK + LANE, MLA_HEADS * LANE), lambda i: (l, 0, 0)),
            pl.BlockSpec((None, MLA_KV_RANK, MLA_HEADS * LANE), lambda i: (l, 0, 0)),
        ],
        out_specs=[pl.BlockSpec((tm, MLA_HEADS * LANE), lambda i: (i, 0)),
                   pl.BlockSpec((MLA_HEADS * LANE, tm), lambda i: (0, i)),
                   pl.BlockSpec((tm, MLA_HEADS * LANE), lambda i: (i, 0))],
        compiler_params=_cparams(("parallel",)),
        name="mla_prep",
    )(cq, ckv, small, tab, gq, gkv, wq, wk, wv)


def _mla_attn_kernel(*refs, n_kv):
    q_ref = refs[0]
    k_refs = refs[1:1 + n_kv]
    v_refs = refs[1 + n_kv:1 + 2 * n_kv]
    o_ref = refs[1 + 2 * n_kv]
    outs = []
    for h in range(2):
        hl = slice(h * LANE, (h + 1) * LANE)
        q = q_ref[:, hl]
        m, acc = None, None
        for k_ref, v_ref in zip(k_refs, v_refs):
            n_keys = k_ref.shape[1]
            for c0 in range(0, n_keys, ATT_TK):
                c1 = min(c0 + ATT_TK, n_keys)
                s = _dot(q, k_ref[hl, c0:c1])
                m_new = s.max(axis=-1, keepdims=True)
                if m is not None:
                    m_new = jnp.maximum(m, m_new)
                    acc = acc * jnp.exp2(m - m_new)
                pv = _dot(jnp.exp2(s - m_new).astype(BF16), v_ref[c0:c1, hl])
                acc = pv if acc is None else acc + pv
                m = m_new
        outs.append(acc / acc[:, MLA_V:MLA_V + 1])
    lane = lax.broadcasted_iota(jnp.int32, outs[0].shape, 1)
    o_ref[...] = jnp.where(lane < MLA_V, outs[0], pltpu.roll(outs[1], MLA_V, 1))


def _mla_attn_latent(q, k, v):
    tq = ATT_TQ
    qt = SEQ // tq
    ctx_blk0 = T_X // CTX_LEN
    return pl.pallas_call(
        functools.partial(_mla_attn_kernel, n_kv=2),
        out_shape=jax.ShapeDtypeStruct((T_X, MLA_HEADS * MLA_V), F32),
        grid=(BATCH, MLA_HEADS // 2, qt),
        in_specs=[
            pl.BlockSpec((tq, 2 * LANE), lambda b, h, t: (b * qt + t, h)),
            pl.BlockSpec((2 * LANE, SEQ), lambda b, h, t: (h, b)),
            pl.BlockSpec((2 * LANE, CTX_LEN), lambda b, h, t: (h, ctx_blk0 + b)),
            pl.BlockSpec((SEQ, 2 * LANE), lambda b, h, t: (b, h)),
            pl.BlockSpec((CTX_LEN, 2 * LANE), lambda b, h, t: (ctx_blk0 + b, h)),
        ],
        out_specs=pl.BlockSpec((tq, LANE), lambda b, h, t: (b * qt + t, h)),
        compiler_params=_cparams(("parallel", "parallel", "arbitrary")),
        name="mla_attn_latent",
    )(q, k, k, v, v)


def _mla_attn_ctx(q, k, v):
    ctx_blk0 = T_X // CTX_LEN
    return pl.pallas_call(
        functools.partial(_mla_attn_kernel, n_kv=1),
        out_shape=jax.ShapeDtypeStruct((T_C, MLA_HEADS * MLA_V), F32),
        grid=(BATCH, MLA_HEADS // 2),
        in_specs=[
            pl.BlockSpec((CTX_LEN, 2 * LANE), lambda b, h: (ctx_blk0 + b, h)),
            pl.BlockSpec((2 * LANE, CTX_LEN), lambda b, h: (h, ctx_blk0 + b)),
            pl.BlockSpec((CTX_LEN, 2 * LANE), lambda b, h: (ctx_blk0 + b, h)),
        ],
        out_specs=pl.BlockSpec((CTX_LEN, LANE), lambda b, h: (b, h)),
        compiler_params=_cparams(("parallel", "parallel")),
        name="mla_attn_ctx",
    )(q, k, v)


def _swa_core(q, keys, vals, masks, sinks, o_ref):
    w = q.shape[0]
    lane = lax.broadcasted_iota(jnp.int32, (w, LANE), 1)
    low = lane < SWA_HEADDIM
    rep = SWA_HEADS // SWA_KV_HEADS
    k_all = jnp.concatenate(keys, axis=0).astype(BF16)
    v_all = jnp.concatenate(vals, axis=0).astype(BF16)
    bias = None
    if any(mask is not None for mask in masks):
        bias = jnp.concatenate(
            [jnp.zeros((w, kb.shape[0]), F32) if mask is None else jnp.where(mask, 0.0, NEG)
             for kb, mask in zip(keys, masks)], axis=1)
        bias = jnp.concatenate([bias] * rep, axis=0)
    for g in range(SWA_KV_HEADS):
        parts = []
        for j in range(rep // 2):
            qp = q[:, (g * rep // 2 + j) * LANE:(g * rep // 2 + j + 1) * LANE]
            parts.append(jnp.where(low, qp, 0.0))
            parts.append(jnp.where(low, 0.0, qp))
        lhs = jnp.concatenate(parts, axis=0).astype(BF16)
        sink = jnp.concatenate([jnp.full((w, 1), sinks[g * rep + r] * LOG2E, F32) for r in range(rep)],
                               axis=0)
        s = _dot_t(lhs, k_all[:, g * LANE:(g + 1) * LANE])
        if bias is not None:
            s = s + bias
        m = jnp.maximum(sink, s.max(axis=-1, keepdims=True))
        p = jnp.exp2(s - m)
        den = jnp.exp2(sink - m) + p.sum(axis=-1, keepdims=True)
        o = _dot(p.astype(BF16), v_all[:, g * LANE:(g + 1) * LANE]) / den
        for j in range(rep // 2):
            o_ref[:, (g * rep // 2 + j) * LANE:(g * rep // 2 + j + 1) * LANE] = jnp.where(
                low, o[2 * j * w:(2 * j + 1) * w], o[(2 * j + 1) * w:(2 * j + 2) * w])


def _rope_rows(x_ref, tab_ref, n, reps):
    cos = jnp.concatenate([tab_ref[:, :LANE]] * reps, axis=1)
    sin = jnp.concatenate([tab_ref[:, LANE:]] * reps, axis=1)
    return x_ref[:, :n].astype(F32) * cos + x_ref[:, n:].astype(F32) * sin


def _swa_latent_kernel(sink_ref, q_ref, kp_ref, kc_ref, kn_ref, kx_ref, vp_ref, vc_ref, vn_ref,
                       vx_ref, tq_ref, tp_ref, tn_ref, o_ref, *, l):
    n = pl.program_id(1)
    q = _rope_rows(q_ref, tq_ref, N_SWA_Q, N_SWA_Q // LANE) * SWA_QSCALE
    keys = [kx_ref[:, :N_SWA_KV2].astype(F32),
            _rope_rows(kp_ref, tp_ref, N_SWA_KV2, N_SWA_KV2 // LANE),
            _rope_rows(kc_ref, tq_ref, N_SWA_KV2, N_SWA_KV2 // LANE),
            _rope_rows(kn_ref, tn_ref, N_SWA_KV2, N_SWA_KV2 // LANE)]
    vals = [vx_ref[...], vp_ref[...], vc_ref[...], vn_ref[...]]
    qi = lax.broadcasted_iota(jnp.int32, (SWA_BLOCK, SWA_BLOCK), 0)
    kj = lax.broadcasted_iota(jnp.int32, (SWA_BLOCK, SWA_BLOCK), 1)
    prev_ok = jnp.logical_and(kj - qi >= SWA_BLOCK - SWA_WINDOW, n > 0)
    next_ok = jnp.logical_and(kj - qi <= SWA_WINDOW - SWA_BLOCK, n < SEQ // SWA_BLOCK - 1)
    sinks = [sink_ref[l, h] for h in range(SWA_HEADS)]
    _swa_core(q, keys, vals, [None, prev_ok, None, next_ok], sinks, o_ref)


def _swa_ctx_kernel(sink_ref, q_ref, kx_ref, vx_ref, o_ref, *, l):
    q = q_ref[:, :N_SWA_Q].astype(F32) * SWA_QSCALE
    sinks = [sink_ref[l, h] for h in range(SWA_HEADS)]
    _swa_core(q, [kx_ref[:, :N_SWA_KV2]], [vx_ref[...]], [None], sinks, o_ref)


def _swa_latent(sink, qs2, ks2, vs2, tab, *, l):
    nb = SEQ // SWA_BLOCK
    ctx_blk0 = T_X // CTX_LEN
    w = SWA_BLOCK

    def prev(b, n):
        return b * nb + jnp.maximum(n - 1, 0)

    def nxt(b, n):
        return b * nb + jnp.minimum(n + 1, nb - 1)

    return pl.pallas_call(
        functools.partial(_swa_latent_kernel, l=l),
        out_shape=jax.ShapeDtypeStruct((T_X, N_SWA_Q), F32),
        grid=(BATCH, nb),
        in_specs=[
            pl.BlockSpec(memory_space=pltpu.SMEM),
            pl.BlockSpec((w, 2 * N_SWA_Q), lambda b, n: (b * nb + n, 0)),
            pl.BlockSpec((w, 2 * N_SWA_KV2), lambda b, n: (prev(b, n), 0)),
            pl.BlockSpec((w, 2 * N_SWA_KV2), lambda b, n: (b * nb + n, 0)),
            pl.BlockSpec((w, 2 * N_SWA_KV2), lambda b, n: (nxt(b, n), 0)),
            pl.BlockSpec((CTX_LEN, 2 * N_SWA_KV2), lambda b, n: (ctx_blk0 + b, 0)),
            pl.BlockSpec((w, N_SWA_KV2), lambda b, n: (prev(b, n), 0)),
            pl.BlockSpec((w, N_SWA_KV2), lambda b, n: (b * nb + n, 0)),
            pl.BlockSpec((w, N_SWA_KV2), lambda b, n: (nxt(b, n), 0)),
            pl.BlockSpec((CTX_LEN, N_SWA_KV2), lambda b, n: (ctx_blk0 + b, 0)),
            pl.BlockSpec((w, 2 * LANE), lambda b, n: (n, 0)),
            pl.BlockSpec((w, 2 * LANE), lambda b, n: (jnp.maximum(n - 1, 0), 0)),
            pl.BlockSpec((w, 2 * LANE), lambda b, n: (jnp.minimum(n + 1, nb - 1), 0)),
        ],
        out_specs=pl.BlockSpec((w, N_SWA_Q), lambda b, n: (b * nb + n, 0)),
        compiler_params=_cparams(("parallel", "arbitrary")),
        name="swa_latent",
    )(sink, qs2, ks2, ks2, ks2, ks2, vs2, vs2, vs2, vs2, tab, tab, tab)


def _swa_ctx(sink, qs2, ks2, vs2, *, l):
    w = SWA_BLOCK
    blk0 = T_X // w
    per_b = CTX_LEN // w
    ctx_blk0 = T_X // CTX_LEN
    return pl.pallas_call(
        functools.partial(_swa_ctx_kernel, l=l),
        out_shape=jax.ShapeDtypeStruct((T_C, N_SWA_Q), F32),
        grid=(BATCH, per_b),
        in_specs=[
            pl.BlockSpec(memory_space=pltpu.SMEM),
            pl.BlockSpec((w, 2 * N_SWA_Q), lambda b, n: (blk0 + b * per_b + n, 0)),
            pl.BlockSpec((CTX_LEN, 2 * N_SWA_KV2), lambda b, n: (ctx_blk0 + b, 0)),
            pl.BlockSpec((CTX_LEN, N_SWA_KV2), lambda b, n: (ctx_blk0 + b, 0)),
        ],
        out_specs=pl.BlockSpec((w, N_SWA_Q), lambda b, n: (b * per_b + n, 0)),
        compiler_params=_cparams(("parallel", "arbitrary")),
        name="swa_ctx",
    )(sink, qs2, ks2, vs2)


def _outproj_kernel(h_ref, mod_ref, ssd_ref, mla_ref, mla_c_ref, swa_ref, swa_c_ref, gm_ref, gs_ref,
                    w_ref, o_ref, *, n_lat):
    n_mla = MLA_HEADS * MLA_V
    is_lat = pl.program_id(0) < n_lat
    acc = _dot(ssd_ref[...], w_ref[0:SSD_INNER, :])
    mla = jnp.where(is_lat, mla_ref[...], mla_c_ref[...])
    mla = (_rms(mla) * gm_ref[...]).astype(BF16)
    acc += _dot(mla, w_ref[SSD_INNER:SSD_INNER + n_mla, :])
    swa = jnp.where(is_lat, swa_ref[...], swa_c_ref[...])
    swa = (_rms(swa) * gs_ref[...]).astype(BF16)
    acc += _dot(swa, w_ref[SSD_INNER + n_mla:, :])
    o_ref[...] = h_ref[...] + mod_ref[5:6, :] * acc


def _outproj(h, mod_l, ssd, mla, mla_c, swa, swa_c, gm, gs, w_out, *, l, rows):
    tm = OUT_TM
    n_mla = MLA_HEADS * MLA_V
    n_lat = T_X // tm

    def lat(i):
        return (jnp.minimum(i, n_lat - 1), 0)

    def ctx(i):
        return (jnp.maximum(i - n_lat, 0), 0)

    return pl.pallas_call(
        functools.partial(_outproj_kernel, n_lat=n_lat),
        out_shape=jax.ShapeDtypeStruct((rows, D_MODEL), F32),
        grid=(rows // tm,),
        in_specs=[
            pl.BlockSpec((tm, D_MODEL), lambda i: (i, 0)),
            pl.BlockSpec((None, N_MOD, D_MODEL), lambda i: (_group_of_tile(i, tm), 0, 0)),
            pl.BlockSpec((tm, SSD_INNER), lambda i: (i, 0)),
            pl.BlockSpec((tm, n_mla), lat),
            pl.BlockSpec((tm, n_mla), ctx),
            pl.BlockSpec((tm, N_SWA_Q), lat),
            pl.BlockSpec((tm, N_SWA_Q), ctx),
            pl.BlockSpec((None, 1, n_mla), lambda i: (l, 0, 0)),
            pl.BlockSpec((None, 1, N_SWA_Q), lambda i: (l, 0, 0)),
            pl.BlockSpec((None, D_MIX, D_MODEL), lambda i: (l, 0, 0)),
        ],
        out_specs=pl.BlockSpec((tm, D_MODEL), lambda i: (i, 0)),
        compiler_params=_cparams(("parallel",)),
        name="outproj",
    )(h, mod_l, ssd, mla, mla_c, swa, swa_c, gm, gs, w_out)


def _cast_kernel(x_ref, o_ref):
    o_ref[...] = x_ref[...].astype(BF16)


def _to_bf16(w):
    r, c = w.shape[-2:]
    lead = int(np.prod(w.shape[:-2]))
    out = pl.pallas_call(
        _cast_kernel,
        out_shape=jax.ShapeDtypeStruct((lead, r, c), BF16),
        grid=(lead, r // CAST_ROWS),
        in_specs=[pl.BlockSpec((None, CAST_ROWS, c), lambda a, t: (a, t, 0))],
        out_specs=pl.BlockSpec((None, CAST_ROWS, c), lambda a, t: (a, t, 0)),
        compiler_params=_cparams(("parallel", "parallel")),
        name="to_bf16",
    )(w.reshape(lead, r, c))
    return out.reshape(w.shape)


def _rot_cols(w, dim):
    lead, n = w.shape[:-1], w.shape[-1]
    w5 = w.reshape(lead + (n // dim, 2, 2, dim // 4))
    w5 = lax.rev(w5, (w5.ndim - 2,)) * jnp.asarray([-1.0, 1.0], w.dtype)[:, None]
    return w5.reshape(lead + (n,))


def _w_in_kernel(x_ref, main_ref, extra_ref):
    rows = x_ref.shape[0]
    lane = lax.broadcasted_iota(jnp.int32, (rows, LANE), 1)
    low = lane < SWA_HEADDIM

    def tile(t):
        return x_ref[:, t * LANE:(t + 1) * LANE]

    def shifted(t, k):
        a = pltpu.roll(tile(t), LANE - k, 1)
        b = pltpu.roll(tile(t + 1), LANE - k, 1)
        return jnp.where(lane < LANE - k, a, b)

    def rot(v, dim):
        q = dim // 4
        return jnp.where(lane % (2 * q) < q, -pltpu.roll(v, LANE - q, 1), pltpu.roll(v, q, 1))

    def dup_hi(v):
        return jnp.where(low, pltpu.roll(v, SWA_HEADDIM, 1), v)

    def dup_lo(v):
        return jnp.where(low, v, pltpu.roll(v, SWA_HEADDIM, 1))

    main_ref[...] = x_ref[:, :W_IN_MAIN].astype(BF16)
    t_dt = W_IN_MAIN // LANE
    off_q = 2 * SSD_HEADS
    n_cqkv = (MLA_Q_RANK + MLA_KV_RANK) // LANE
    t_kr = t_dt + n_cqkv
    off_s = off_q + MLA_ROPE
    n_qs = N_SWA_Q // LANE
    t_k0 = t_kr + n_qs
    v1 = x_ref[:, (t_k0 + 2) * LANE:(t_k0 + 2) * LANE + SWA_HEADDIM]
    qs = [shifted(t_kr + j, off_s) for j in range(n_qs)]
    kr_tile = tile(t_kr)
    rot_kr = pltpu.roll(rot(kr_tile, MLA_ROPE), MLA_ROPE, 1)
    small = jnp.where(lane < off_q, tile(t_dt),
                      jnp.where(lane < off_s, kr_tile, jnp.where(lane < off_s + MLA_ROPE, rot_kr, 0.0)))
    pieces = ([shifted(t_dt + j, off_q) for j in range(n_cqkv)] + qs + [rot(v, SWA_HEADDIM) for v in qs]
              + [dup_hi(tile(t_k0)), dup_lo(tile(t_k0 + 1)),
                 dup_hi(rot(tile(t_k0), SWA_HEADDIM)), dup_lo(rot(tile(t_k0 + 1), SWA_HEADDIM)),
                 dup_hi(tile(t_k0 + 1)), jnp.concatenate([v1, v1], axis=1), small])
    extra_ref[...] = jnp.concatenate(pieces, axis=1).astype(BF16)


def _w_in_relayout(w_in):
    d_in = w_in.shape[-1]
    assert d_in == sum(IN_SIZES)
    return pl.pallas_call(
        _w_in_kernel,
        out_shape=[jax.ShapeDtypeStruct((DEPTH, D_MODEL, W_IN_MAIN), BF16),
                   jax.ShapeDtypeStruct((DEPTH, D_MODEL, W_IN_COLS - W_IN_MAIN), BF16)],
        grid=(DEPTH, D_MODEL // CAST_ROWS),
        in_specs=[pl.BlockSpec((None, CAST_ROWS, d_in), lambda a, t: (a, t, 0))],
        out_specs=[pl.BlockSpec((None, CAST_ROWS, W_IN_MAIN), lambda a, t: (a, t, 0)),
                   pl.BlockSpec((None, CAST_ROWS, W_IN_COLS - W_IN_MAIN), lambda a, t: (a, t, 0))],
        compiler_params=_cparams(("parallel", "parallel")),
        name="w_in_relayout",
    )(w_in)


def _mla_w_relayout(w_uq, w_ukv):
    lead = w_uq.shape[:-1]
    q = w_uq.reshape(lead + (MLA_HEADS, MLA_NOPE + MLA_ROPE))
    rope = q[..., MLA_NOPE:]
    wq = jnp.concatenate([q, _rot_cols(rope, MLA_ROPE)], axis=-1).reshape(lead + (MLA_HEADS * LANE,))
    lead = w_ukv.shape[:-1]
    kv = w_ukv.reshape(lead + (MLA_HEADS, MLA_NOPE + MLA_V))
    pad = jnp.zeros(lead + (MLA_HEADS, LANE - MLA_NOPE), w_ukv.dtype)
    wk = jnp.concatenate([kv[..., :MLA_NOPE], pad], axis=-1).reshape(lead + (MLA_HEADS * LANE,))
    wv = jnp.concatenate([kv[..., MLA_NOPE:], pad], axis=-1).reshape(lead + (MLA_HEADS * LANE,))
    return wq.astype(BF16), wk, wv.astype(BF16)


def _k_rope_expand():
    e = np.zeros((LANE, MLA_HEADS * LANE), np.float32)
    for h in range(MLA_HEADS):
        for i in range(MLA_ROPE):
            e[MLA_ROPE + i, h * LANE + MLA_NOPE + i] = 1.0
    return e


def _head_expand():
    e = np.zeros((SSD_HEADS, SSD_INNER), np.float32)
    for h in range(SSD_HEADS):
        e[h, h * SSD_HEADDIM:(h + 1) * SSD_HEADDIM] = 1.0
    return e


def _rope_angles(dim):
    pos = np.arange(SEQ)
    quarter = dim // 4
    inv_freq = ROPE_THETA ** (-jnp.arange(quarter, dtype=F32) / quarter)
    ang_r = jnp.asarray(pos // GRID_W, F32)[:, None] * inv_freq[None, :]
    ang_c = jnp.asarray(pos % GRID_W, F32)[:, None] * inv_freq[None, :]
    ang = jnp.concatenate([ang_r, ang_r, ang_c, ang_c], axis=-1)
    return jnp.cos(ang), jnp.sin(ang)


def _mla_rope_table():
    cos, sin = _rope_angles(MLA_ROPE)
    scale = (MLA_NOPE + MLA_ROPE) ** -0.5 * LOG2E
    n = SEQ + PREP_TM
    ident = jnp.ones((PREP_TM, MLA_ROPE), F32)
    zeros = jnp.zeros((PREP_TM, MLA_ROPE), F32)
    cos = jnp.concatenate([cos, ident], axis=0)
    sin = jnp.concatenate([sin, zeros], axis=0)
    z32 = jnp.zeros((n, MLA_ROPE), F32)
    cq = jnp.concatenate([jnp.ones((n, MLA_NOPE), F32), cos, z32], axis=1) * scale
    sq = jnp.concatenate([jnp.zeros((n, MLA_NOPE), F32), sin, z32], axis=1) * scale
    ck = jnp.concatenate([z32, cos, z32, z32], axis=1)
    sk = jnp.concatenate([z32, sin, z32, z32], axis=1)
    return jnp.concatenate([cq, sq, ck, sk], axis=1)


def _swa_rope_table():
    cos, sin = _rope_angles(SWA_HEADDIM)
    return jnp.concatenate([cos, cos, sin, sin], axis=1)


def kernel(x, c, ctx, c_ctx, w_mod, b_mod, norm_g, ffn_w_gate, ffn_w_up, ffn_w_down,
           w_in, w_out, ssd_conv_w, ssd_conv_b, ssd_dt_bias, ssd_a_log, ssd_d, ssd_norm,
           mla_q_norm, mla_w_uq, mla_kv_norm, mla_w_ukv, mla_out_norm, swa_sink,
           swa_out_norm, final_norm):
    w_main, w_extra = _w_in_relayout(w_in)
    wq, wk, wv = _mla_w_relayout(mla_w_uq, mla_w_ukv)
    k_expand = jnp.broadcast_to(jnp.asarray(_k_rope_expand()), (DEPTH, LANE, MLA_HEADS * LANE))
    wk = jnp.concatenate([wk, k_expand], axis=1).astype(BF16)
    wg = _to_bf16(ffn_w_gate)
    wu = _to_bf16(ffn_w_up)
    wd = _to_bf16(ffn_w_down)
    wo = _to_bf16(w_out)
    mla_tab = _mla_rope_table()
    swa_tab = _swa_rope_table()
    e16 = jnp.asarray(_head_expand(), BF16)
    dskip_e = jnp.repeat(ssd_d, SSD_HEADDIM, axis=-1).reshape(DEPTH, 1, SSD_INNER)
    ssd_gn = ssd_norm.reshape(DEPTH, 1, SSD_INNER)
    conv_b = ssd_conv_b.reshape(DEPTH, 1, SSD_CONV_CH)
    bias_r = ssd_dt_bias.reshape(DEPTH, 2, 1, SSD_HEADS)
    bias_c = ssd_dt_bias.reshape(DEPTH, 2, SSD_HEADS, 1)
    alog_r = ssd_a_log.reshape(DEPTH, 2, 1, SSD_HEADS)
    alog_c = ssd_a_log.reshape(DEPTH, 2, SSD_HEADS, 1)
    gq = mla_q_norm.reshape(DEPTH, 1, MLA_Q_RANK)
    gkv = mla_kv_norm.reshape(DEPTH, 1, MLA_KV_RANK)
    gm = mla_out_norm.reshape(DEPTH, 1, -1)
    gs = swa_out_norm.reshape(DEPTH, 1, -1)
    fin = final_norm.reshape(1, D_MODEL)

    c_all = jnp.concatenate([c, c_ctx[None, :], jnp.zeros((GROUP_PAD - N_GROUPS, D_MODEL), F32)], axis=0)
    mod_all = _modulation(c_all, w_mod, b_mod).reshape(DEPTH, GROUP_PAD, N_MOD, D_MODEL)

    h = x.reshape(T_X, D_MODEL)
    h_ctx = ctx.reshape(T_C, D_MODEL)

    for l in range(DEPTH):
        last = l == DEPTH - 1
        mod_l = mod_all[l]
        h = _ffn(h, h_ctx, mod_l, norm_g[l, 0].reshape(1, D_MODEL), wg, wu, wd, fin,
                 l=l, k=0, base=0, rows=T_ALL)
        h_ctx = None
        z, xbc, cq, ckv, qs2, ks2, vs2, small = _inproj(h, mod_l, norm_g[l, 1].reshape(1, D_MODEL),
                                                       w_main, w_extra, l=l)
        dt2 = small[:, :2 * SSD_HEADS].reshape(T_ALL, 2, SSD_HEADS)
        dt_col = jnp.transpose(dt2, (1, 0, 2))
        dt_row = jnp.transpose(dt2, (1, 2, 0))
        ssd_o = _ssd(xbc, z, dt_col, dt_row, ssd_conv_w, conv_b, bias_r, bias_c, alog_r, alog_c,
                     dskip_e, ssd_gn, e16, l=l)
        q_m, k_m, v_m = _mla_prep(cq, ckv, small, mla_tab, gq, gkv, wq, wk, wv, l=l)
        mla_x = _mla_attn_latent(q_m, k_m, v_m)
        swa_x = _swa_latent(swa_sink, qs2, ks2, vs2, swa_tab, l=l)
        if last:
            rows, mla_c, swa_c = T_X, mla_x, swa_x
        else:
            rows = T_ALL
            mla_c = _mla_attn_ctx(q_m, k_m, v_m)
            swa_c = _swa_ctx(swa_sink, qs2, ks2, vs2, l=l)
        h = _outproj(h, mod_l, ssd_o, mla_x, mla_c, swa_x, swa_c, gm, gs, wo, l=l, rows=rows)
        h = _ffn(h, None, mod_l, norm_g[l, 2].reshape(1, D_MODEL), wg, wu, wd, fin,
                 l=l, k=1, base=6, rows=rows, final_norm=last)
    return h.reshape(BATCH, SEQ, D_MODEL)
```
